```python
import math
import jax
import jax.numpy as jnp
from jax import lax
import numpy as np

D_MODEL = 1024
BATCH = 16
SEQ = 2048
DEPTH = 4

GRID_W = 64
CTX_LEN = 256

HEAD_DIM = 64
N_Q_HEADS = 8
N_KV_HEADS = 2
Q_GROUP = N_Q_HEADS // N_KV_HEADS
ATT_Q_W = N_Q_HEADS * HEAD_DIM
ATT_KV_W = N_KV_HEADS * HEAD_DIM
Q_BLOCK = 128
ROPE_THETA = 10000.0
ROPE_AXIS_DIM = HEAD_DIM // 2

HY_W = D_MODEL // 4
HY_ORDER = 2
HY_SHORT = 3
HY_BANDS = 16
HY_EMB = 2 * HY_BANDS + 1
HY_FFN = 64
HY_DECAY_TARGET = 1e-2
HY_DECAY_SHORT_PCT = 0.3
HY_DECAY_LONG_PCT = 1.5

FN_GROUPS = 4
FN_GROUP_W = D_MODEL // 16
FN_W = FN_GROUPS * FN_GROUP_W

MIX_W = ATT_Q_W + HY_W + FN_W
N_BRANCH = 3

OFF_Q = 0
OFF_K = OFF_Q + ATT_Q_W
OFF_V = OFF_K + ATT_KV_W
OFF_HY = OFF_V + ATT_KV_W
OFF_FN = OFF_HY + (HY_ORDER + 1) * HY_W
OFF_GATE = OFF_FN + FN_W
IN_W = OFF_GATE + N_BRANCH * D_MODEL

N_EXPERTS = 16
EXPERT_FF = 2 * D_MODEL
CAPACITY_FACTOR = 2

EPS = 1e-6

kernel_name = "hybrid_hyena_fnet_gqa_ecmoe_dit"


def rms_norm(x, g):
    xf = x.astype(jnp.float32)
    y = xf * lax.rsqrt(jnp.mean(xf * xf, axis=-1, keepdims=True) + EPS)
    return (y * g.astype(jnp.float32)).astype(x.dtype)


def modulate(h, shift, scale):
    return h * (1.0 + scale) + shift


def axial_rope_tables(n_tokens):
    rows = n_tokens // GRID_W
    row = jnp.repeat(jnp.arange(rows, dtype=jnp.float32), GRID_W)
    col = jnp.tile(jnp.arange(GRID_W, dtype=jnp.float32), rows)
    inv = ROPE_THETA ** (-jnp.arange(0, ROPE_AXIS_DIM, 2, dtype=jnp.float32) / ROPE_AXIS_DIM)
    ang = jnp.concatenate([row[:, None] * inv, col[:, None] * inv], axis=-1)
    return jnp.cos(ang), jnp.sin(ang)


def apply_rope(u, cos, sin):
    uf = u.astype(jnp.float32).reshape(*u.shape[:-1], HEAD_DIM // 2, 2)
    u0, u1 = uf[..., 0], uf[..., 1]
    c = cos[None, :, None, :]
    s = sin[None, :, None, :]
    out = jnp.stack([u0 * c - u1 * s, u0 * s + u1 * c], axis=-1)
    return out.reshape(u.shape).astype(u.dtype)


def qk_heads(u, n_heads, gain, rope):
    b, n, _ = u.shape
    u = rms_norm(u.reshape(b, n, n_heads, HEAD_DIM), gain)
    if rope is not None:
        u = apply_rope(u, rope[0], rope[1])
    return u


def block_attention(q, k, v):
    b, n = q.shape[0], q.shape[1]
    nb = n // Q_BLOCK
    qb = q.reshape(b, nb, Q_BLOCK, N_KV_HEADS, Q_GROUP, HEAD_DIM).transpose(1, 0, 2, 3, 4, 5)
    scale = HEAD_DIM ** -0.5

    def attend(qblk):
        s = jnp.einsum('bqkgd,bskd->bkgqs', qblk, k, preferred_element_type=jnp.float32) * scale
        p = jax.nn.softmax(s, axis=-1).astype(v.dtype)
        return jnp.einsum('bkgqs,bskd->bqkgd', p, v)

    o = lax.map(attend, qb)
    return o.transpose(1, 0, 2, 3, 4, 5).reshape(b, n, ATT_Q_W)


def short_conv(u, w, b):
    n = u.shape[1]
    pad = HY_SHORT // 2
    up = jnp.pad(u, ((0, 0), (pad, pad), (0, 0)))
    out = b
    for j in range(HY_SHORT):
        out = out + up[:, j:j + n] * w[j]
    return out


def hyena_spectrum(n, w1, b1, w2, b2, w3, freq):
    f32 = jnp.float32
    w1, b1, w2, b2, w3, freq = (a.astype(f32) for a in (w1, b1, w2, b2, w3, freq))
    pos = jnp.arange(n, dtype=f32)
    t = pos / (n - 1)
    bands = jnp.linspace(1e-4, HY_BANDS - 1, HY_BANDS, dtype=f32)
    ang = (2.0 * math.pi / n) * pos[:, None] * bands[None, :]
    feats = jnp.concatenate([t[:, None], jnp.cos(ang), -jnp.sin(ang)], axis=-1)
    h = jnp.sin(freq * (feats @ w1 + b1))
    h = jnp.sin(freq * (h @ w2 + b2))
    h = (h @ w3).reshape(n, 2, HY_ORDER, HY_W)
    deltas = jnp.abs(jnp.linspace(math.log(HY_DECAY_TARGET) / HY_DECAY_LONG_PCT,
                                  math.log(HY_DECAY_TARGET) / HY_DECAY_SHORT_PCT, HY_W, dtype=f32))
    h = h * jnp.exp(-t[:, None] * deltas[None, :])[:, None, None, :]
    filt = jnp.concatenate([h[:, 0], jnp.zeros((1, HY_ORDER, HY_W), f32), h[1:, 1][::-1]], axis=0)
    filt = filt / jnp.sum(jnp.abs(filt), axis=0, keepdims=True)
    return jnp.fft.rfft(filt, axis=0)


def long_conv(z, spec, d):
    n = z.shape[1]
    zf = z.astype(jnp.float32)
    y = jnp.fft.irfft(jnp.fft.rfft(zf, n=2 * n, axis=1) * spec[None], n=2 * n, axis=1)[:, :n]
    return (y + zf * d.astype(jnp.float32)).astype(z.dtype)


def hyena_mixer(u, sw, sb, spec, bias):
    u = short_conv(u, sw, sb)
    v, g1, g2 = jnp.split(u, HY_ORDER + 1, axis=-1)
    z = g1 * long_conv(v, spec[:, 0], bias[0])
    return g2 * long_conv(z, spec[:, 1], bias[1])


def fourier_mix(u):
    b, n, _ = u.shape
    ug = u.astype(jnp.float32).reshape(b, n, FN_GROUPS, FN_GROUP_W)
    y = jnp.fft.fft2(ug, axes=(1, 3), norm='ortho').real
    return y.reshape(b, n, FN_W).astype(u.dtype)


def merge_branches(att, hy, fn, gates, w_branch, w_out):
    ga, gh, gf = jnp.split(gates, N_BRANCH, axis=-1)
    ya = att @ w_branch[:ATT_Q_W]
    yh = hy @ w_branch[ATT_Q_W:ATT_Q_W + HY_W]
    yf = fn @ w_branch[ATT_Q_W + HY_W:]
    m = jax.nn.sigmoid(ga) * ya + jax.nn.sigmoid(gh) * yh + jax.nn.sigmoid(gf) * yf
    return m @ w_out


def mixer_sublayer(h, k_ext, v_ext, rope, w_in, q_gain, k_gain, sw, sb, spec, hy_bias, w_branch, w_out):
    b, n, _ = h.shape
    p = h @ w_in
    q = qk_heads(p[..., OFF_Q:OFF_K], N_Q_HEADS, q_gain, rope)
    k = qk_heads(p[..., OFF_K:OFF_V], N_KV_HEADS, k_gain, rope)
    v = p[..., OFF_V:OFF_HY].reshape(b, n, N_KV_HEADS, HEAD_DIM)
    if k_ext is None:
        att = block_attention(q, k, v)
    else:
        att = block_attention(q, jnp.concatenate([k, k_ext], axis=1), jnp.concatenate([v, v_ext], axis=1))
    hy = hyena_mixer(p[..., OFF_HY:OFF_FN], sw, sb, spec, hy_bias)
    fn = fourier_mix(p[..., OFF_FN:OFF_GATE])
    return merge_branches(att, hy, fn, p[..., OFF_GATE:], w_branch, w_out), k, v


def expert_choice_moe(h, w_router, w_gate, w_up, w_down):
    b, n, d = h.shape
    cap = CAPACITY_FACTOR * n // N_EXPERTS
    logits = jnp.einsum('bld,de->ble', h, w_router, preferred_element_type=jnp.float32)
    aff = jax.nn.softmax(logits, axis=-1)
    top_w, top_i = lax.top_k(aff.transpose(0, 2, 1), cap)
    xg = jax.vmap(lambda hb, ib: hb[ib])(h, top_i)
    a = jnp.einsum('becd,edf->becf', xg, w_gate)
    u = jnp.einsum('becd,edf->becf', xg, w_up)
    y = jnp.einsum('becf,efd->becd', jax.nn.silu(a) * u, w_down)
    y = y * top_w[..., None].astype(y.dtype)
    return jax.vmap(lambda yb, ib: jnp.zeros((n, d), y.dtype).at[ib.reshape(-1)].add(yb.reshape(-1, d)))(y, top_i)


def setup_inputs(seed: int = 0) -> dict:
    key = jax.random.key(seed)
    ks = jax.random.split(key, 27)
    f32 = jnp.float32
    D = D_MODEL

    def nrm(k, shape, scale):
        return jax.random.normal(k, shape, f32) * scale

    return {
        "x": nrm(ks[0], (BATCH, SEQ, D), 1.0),
        "c": nrm(ks[1], (BATCH, D), 1.0),
        "ctx": nrm(ks[2], (BATCH, CTX_LEN, D), 1.0),
        "c_ctx": nrm(ks[3], (D,), 1.0),
        "w_mod": nrm(ks[4], (DEPTH, D, 6 * D), 0.5 * D ** -0.5),
        "b_mod": nrm(ks[5], (DEPTH, 6 * D), 0.02),
        "norm1_g": 1.0 + nrm(ks[6], (DEPTH, D), 0.02),
        "norm2_g": 1.0 + nrm(ks[7], (DEPTH, D), 0.02),
        "w_in": nrm(ks[8], (DEPTH, D, IN_W), D ** -0.5),
        "q_gain": 1.0 + nrm(ks[9], (DEPTH, HEAD_DIM), 0.02),
        "k_gain": 1.0 + nrm(ks[10], (DEPTH, HEAD_DIM), 0.02),
        "hy_short_w": nrm(ks[11], (DEPTH, HY_SHORT, (HY_ORDER + 1) * HY_W), HY_SHORT ** -0.5),
        "hy_short_b": nrm(ks[12], (DEPTH, (HY_ORDER + 1) * HY_W), 0.02),
        "hy_f_w1": nrm(ks[13], (DEPTH, HY_EMB, HY_FFN), HY_EMB ** -0.5),
        "hy_f_b1": nrm(ks[14], (DEPTH, HY_FFN), 0.1),
        "hy_f_w2": nrm(ks[15], (DEPTH, HY_FFN, HY_FFN), HY_FFN ** -0.5),
        "hy_f_b2": nrm(ks[16], (DEPTH, HY_FFN), 0.1),
        "hy_f_w3": nrm(ks[17], (DEPTH, HY_FFN, 2 * HY_ORDER * HY_W), HY_FFN ** -0.5),
        "hy_f_freq": 1.0 + nrm(ks[18], (DEPTH, HY_FFN), 0.02),
        "hy_bias": nrm(ks[19], (DEPTH, HY_ORDER, HY_W), 1.0),
        "w_branch": nrm(ks[20], (DEPTH, MIX_W, D), (MIX_W // 4) ** -0.5),
        "w_out": nrm(ks[21], (DEPTH, D, D), D ** -0.5),
        "w_router": nrm(ks[22], (DEPTH, D, N_EXPERTS), D ** -0.5),
        "w_gate": nrm(ks[23], (DEPTH, N_EXPERTS, D, EXPERT_FF), D ** -0.5),
        "w_up": nrm(ks[24], (DEPTH, N_EXPERTS, D, EXPERT_FF), D ** -0.5),
        "w_down": nrm(ks[25], (DEPTH, N_EXPERTS, EXPERT_FF, D), EXPERT_FF ** -0.5),
        "final_g": 1.0 + nrm(ks[26], (D,), 0.02),
    }


def reference(x, c, ctx, c_ctx, w_mod, b_mod, norm1_g, norm2_g, w_in, q_gain, k_gain,
              hy_short_w, hy_short_b, hy_f_w1, hy_f_b1, hy_f_w2, hy_f_b2, hy_f_w3, hy_f_freq,
              hy_bias, w_branch, w_out, w_router, w_gate, w_up, w_down, final_g):
    n_lat = x.shape[1]
    n_ctx = ctx.shape[1]
    rope = axial_rope_tables(n_lat)
    sc = jax.nn.silu(c)
    scc = jax.nn.silu(c_ctx)
    for l in range(DEPTH):
        last = l == DEPTH - 1
        mx = (sc @ w_mod[l] + b_mod[l])[:, None, :]
        mc = scc @ w_mod[l] + b_mod[l]
        sh1, sc1, g1, sh2, sc2, g2 = jnp.split(mx, 6, axis=-1)
        csh1, csc1, cg1, csh2, csc2, cg2 = jnp.split(mc, 6, axis=-1)

        hc = modulate(rms_norm(ctx, norm1_g[l]), csh1, csc1)
        if last:
            pkv = hc @ w_in[l][:, OFF_K:OFF_HY]
            kc = qk_heads(pkv[..., :ATT_KV_W], N_KV_HEADS, k_gain[l], None)
            vc = pkv[..., ATT_KV_W:].reshape(ctx.shape[0], n_ctx, N_KV_HEADS, HEAD_DIM)
        else:
            spec_c = hyena_spectrum(n_ctx, hy_f_w1[l], hy_f_b1[l], hy_f_w2[l], hy_f_b2[l], hy_f_w3[l], hy_f_freq[l])
            yc, kc, vc = mixer_sublayer(hc, None, None, None, w_in[l], q_gain[l], k_gain[l],
                                        hy_short_w[l], hy_short_b[l], spec_c, hy_bias[l], w_branch[l], w_out[l])

        spec_x = hyena_spectrum(n_lat, hy_f_w1[l], hy_f_b1[l], hy_f_w2[l], hy_f_b2[l], hy_f_w3[l], hy_f_freq[l])
        hx = modulate(rms_norm(x, norm1_g[l]), sh1, sc1)
        yx, _, _ = mixer_sublayer(hx, kc, vc, rope, w_in[l], q_gain[l], k_gain[l],
                                  hy_short_w[l], hy_short_b[l], spec_x, hy_bias[l], w_branch[l], w_out[l])
        x = x + g1 * yx
        h2 = modulate(rms_norm(x, norm2_g[l]), sh2, sc2)
        x = x + g2 * expert_choice_moe(h2, w_router[l], w_gate[l], w_up[l], w_down[l])

        if not last:
            ctx = ctx + cg1 * yc
            hc2 = modulate(rms_norm(ctx, norm2_g[l]), csh2, csc2)
            ctx = ctx + cg2 * expert_choice_moe(hc2, w_router[l], w_gate[l], w_up[l], w_down[l])
    return rms_norm(x, final_g)
```

```python
import functools
import math

import jax
import jax.numpy as jnp
from jax import lax
from jax.experimental import pallas as pl
from jax.experimental.pallas import tpu as pltpu

F32 = jnp.float32
BF16 = jnp.bfloat16

D_MODEL = 1024
GRID_W = 64
HEAD_DIM = 64
N_Q_HEADS = 8
N_KV_HEADS = 2
Q_GROUP = N_Q_HEADS // N_KV_HEADS
ATT_Q_W = N_Q_HEADS * HEAD_DIM
ATT_KV_W = N_KV_HEADS * HEAD_DIM
ROPE_THETA = 10000.0
ROPE_AXIS_DIM = HEAD_DIM // 2
HY_W = D_MODEL // 4
HY_ORDER = 2
HY_SHORT = 3
HY_BANDS = 16
HY_EMB = 2 * HY_BANDS + 1
HY_FFN = 64
HY_DECAY_TARGET = 1e-2
HY_DECAY_SHORT_PCT = 0.3
HY_DECAY_LONG_PCT = 1.5
FN_GROUPS = 4
FN_GROUP_W = D_MODEL // 16
FN_W = FN_GROUPS * FN_GROUP_W
MIX_W = ATT_Q_W + HY_W + FN_W
N_BRANCH = 3
OFF_Q = 0
OFF_K = OFF_Q + ATT_Q_W
OFF_V = OFF_K + ATT_KV_W
OFF_HY = OFF_V + ATT_KV_W
OFF_FN = OFF_HY + (HY_ORDER + 1) * HY_W
OFF_GATE = OFF_FN + FN_W
N_EXPERTS = 16
CAPACITY_FACTOR = 2
EPS = 1e-6

LANES = 128
VMEM_LIMIT = 56 * 1024 * 1024

A_Q, A_K, A_V, A_HY, A_FN, A_END = 0, 512, 640, 768, 1536, 1792


def _cparams(sem):
    return pltpu.CompilerParams(dimension_semantics=sem, vmem_limit_bytes=VMEM_LIMIT)


def _dot(a, b):
    return jnp.dot(a, b, preferred_element_type=F32)


def _split_bf16(x):
    hi = x.astype(BF16)
    lo = (x - hi.astype(F32)).astype(BF16)
    return hi, lo


def _rms_mod(x, g, shift, scale):
    y = x * lax.rsqrt(jnp.mean(x * x, axis=-1, keepdims=True) + EPS)
    return (y * g) * (1.0 + scale) + shift


def _mod_kernel(c_ref, w_ref, b_ref, o_ref):
    c = c_ref[...]
    sc = c * (1.0 / (1.0 + jnp.exp(-c)))
    o_ref[...] = jnp.dot(sc, w_ref[...], preferred_element_type=F32,
                         precision=lax.Precision.HIGHEST) + b_ref[...]


def _modulation(c_all, w_mod, b_mod):
    depth, d, n6 = w_mod.shape
    rows = c_all.shape[0]
    tn = 1536
    return pl.pallas_call(
        _mod_kernel,
        grid=(depth, n6 // tn),
        in_specs=[pl.BlockSpec((rows, d), lambda l, j: (0, 0)),
                  pl.BlockSpec((None, d, tn), lambda l, j: (l, 0, j)),
                  pl.BlockSpec((None, 1, tn), lambda l, j: (l, 0, j))],
        out_specs=pl.BlockSpec((None, rows, tn), lambda l, j: (l, 0, j)),
        out_shape=jax.ShapeDtypeStruct((depth, rows, n6), F32),
        compiler_params=_cparams(("arbitrary", "arbitrary")),
        name="modulation",
    )(c_all, w_mod, b_mod.reshape(depth, 1, n6))


def _head_norm(u, gmat_ref, width):
    hi, lo = _split_bf16(u * u)
    gm = gmat_ref[0:width, 0:width]
    ms = (_dot(hi, gm) + _dot(lo, gm)) * (1.0 / HEAD_DIM)
    return lax.rsqrt(ms + EPS)


def _inproj_kernel(*refs, rope, combine):
    it = iter(refs)
    x_ref = next(it)
    if combine:
        xb_ref = next(it)
    mod_ref = next(it)
    g_ref = next(it)
    w_ref = next(it)
    if rope:
        wsw_ref = next(it)
        cos_ref = next(it)
        sin_ref = next(it)
    qg_ref = next(it)
    kg_ref = next(it)
    gmat_ref = next(it)
    if combine:
        xo_ref = next(it)
    q_ref = next(it)
    k_ref = next(it)
    v_ref = next(it)
    hy_ref = next(it)
    fn_ref = next(it)

    x = x_ref[...]
    if combine:
        x = x + mod_ref[5:6, :] * xb_ref[...]
        xo_ref[...] = x
    hb = _rms_mod(x, g_ref[...], mod_ref[0:1, :], mod_ref[1:2, :]).astype(BF16)

    def qk(lo, width, gain_ref, out_ref, scale):
        u = _dot(hb, w_ref[:, lo:lo + width])
        r = _head_norm(u, gmat_ref, width)
        un = u * r * gain_ref[0:1, :]
        if rope:
            off = lo - A_Q if lo == A_Q else ATT_Q_W
            usw = _dot(hb, wsw_ref[:, off:off + width])
            reps = width // LANES
            cos_t = jnp.concatenate([cos_ref[...]] * reps, axis=1) if reps > 1 else cos_ref[...]
            sin_t = jnp.concatenate([sin_ref[...]] * reps, axis=1) if reps > 1 else sin_ref[...]
            un = un * cos_t + (usw * r * gain_ref[1:2, :]) * sin_t
        out_ref[...] = (un * scale).astype(out_ref.dtype)

    qk(A_Q, ATT_Q_W, qg_ref, q_ref, HEAD_DIM ** -0.5)
    qk(A_K, ATT_KV_W, kg_ref, k_ref, 1.0)
    v_ref[...] = _dot(hb, w_ref[:, A_V:A_HY]).astype(v_ref.dtype)
    hy_ref[...] = _dot(hb, w_ref[:, A_HY:A_FN]).astype(hy_ref.dtype)
    fn_ref[...] = _dot(hb, w_ref[:, A_FN:A_END]).astype(fn_ref.dtype)


def _inproj(x, xb, mod, norm_g, w_main, w_sw, cos_t, sin_t, qg, kg, gmat, n, tm):
    m, d = x.shape
    rope = w_sw is not None
    combine = xb is not None
    per_b = mod.shape[0] > 1
    tpb = n // tm
    row = lambda i: (i, 0)
    const = lambda i: (0, 0)
    mod_map = (lambda i: (i // tpb, 0, 0)) if per_b else (lambda i: (0, 0, 0))
    args, specs = [x], [pl.BlockSpec((tm, d), row)]
    if combine:
        args.append(xb)
        specs.append(pl.BlockSpec((tm, d), row))
    args += [mod, norm_g, w_main]
    specs += [pl.BlockSpec((None, 8, d), mod_map), pl.BlockSpec((1, d), const),
              pl.BlockSpec(w_main.shape, const)]
    if rope:
        args += [w_sw, cos_t, sin_t]
        specs += [pl.BlockSpec(w_sw.shape, const),
                  pl.BlockSpec((tm, LANES), lambda i: (i % tpb, 0)),
                  pl.BlockSpec((tm, LANES), lambda i: (i % tpb, 0))]
    args += [qg, kg, gmat]
    specs += [pl.BlockSpec(qg.shape, const), pl.BlockSpec(kg.shape, const), pl.BlockSpec(gmat.shape, const)]
    widths = [ATT_Q_W, ATT_KV_W, ATT_KV_W, (HY_ORDER + 1) * HY_W, FN_W]
    out_shape = [jax.ShapeDtypeStruct((m, w), BF16) for w in widths]
    out_specs = [pl.BlockSpec((tm, w), row) for w in widths]
    if combine:
        out_shape.insert(0, jax.ShapeDtypeStruct((m, d), F32))
        out_specs.insert(0, pl.BlockSpec((tm, d), row))
    return pl.pallas_call(
        functools.partial(_inproj_kernel, rope=rope, combine=combine),
        grid=(m // tm,), in_specs=specs, out_specs=out_specs, out_shape=out_shape,
        compiler_params=_cparams(("arbitrary",)), name="inproj",
    )(*args)


def _attn_kernel(q_ref, k_ref, v_ref, o_ref):
    outs = []
    for h in range(N_Q_HEADS):
        g = h // Q_GROUP
        qh = q_ref[:, h * HEAD_DIM:(h + 1) * HEAD_DIM]
        kg = k_ref[:, g * HEAD_DIM:(g + 1) * HEAD_DIM]
        vg = v_ref[:, g * HEAD_DIM:(g + 1) * HEAD_DIM]
        s = lax.dot_general(qh, kg, (((1,), (1,)), ((), ())), preferred_element_type=F32)
        p = jnp.exp(s - jnp.max(s, axis=-1, keepdims=True))
        l = jnp.sum(p, axis=-1, keepdims=True)
        outs.append(_dot(p.astype(BF16), vg) / l)
    o_ref[...] = jnp.concatenate(outs, axis=1).astype(o_ref.dtype)


def _attention(q, kcat, vcat, n, tq):
    m = q.shape[0]
    b, s, _ = kcat.shape
    tpb = n // tq
    return pl.pallas_call(
        _attn_kernel,
        grid=(b, tpb),
        in_specs=[pl.BlockSpec((tq, ATT_Q_W), lambda bi, i: (bi * tpb + i, 0)),
                  pl.BlockSpec((None, s, ATT_KV_W), lambda bi, i: (bi, 0, 0)),
                  pl.BlockSpec((None, s, ATT_KV_W), lambda bi, i: (bi, 0, 0))],
        out_specs=pl.BlockSpec((tq, ATT_Q_W), lambda bi, i: (bi * tpb + i, 0)),
        out_shape=jax.ShapeDtypeStruct((m, ATT_Q_W), BF16),
        compiler_params=_cparams(("arbitrary", "arbitrary")), name="attention",
    )(q, kcat, vcat)


def _hyena_filter_kernel(feat_ref, w1_ref, b1_ref, w2_ref, b2_ref, w3_ref, freq_ref, delta_ref,
                         cn_ref, sn_ref, hr_ref, hs_ref, hn_ref):
    n = feat_ref.shape[0]
    hp = lax.Precision.HIGHEST
    freq = freq_ref[...]
    h = jnp.sin(freq * (jnp.dot(feat_ref[...], w1_ref[...], preferred_element_type=F32, precision=hp)
                        + b1_ref[...]))
    h = jnp.sin(freq * (jnp.dot(h, w2_ref[...], preferred_element_type=F32, precision=hp) + b2_ref[...]))
    h = jnp.dot(h, w3_ref[...], preferred_element_type=F32, precision=hp)
    w2o = HY_ORDER * HY_W
    pos = lax.broadcasted_iota(jnp.int32, (n, w2o), 0)
    t = pos.astype(F32) / (n - 1)
    decay = jnp.exp(-t * delta_ref[...])
    hf = h[:, 0:w2o] * decay
    hb = jnp.where(pos == 0, 0.0, h[:, w2o:2 * w2o] * decay)
    inv = 1.0 / (jnp.sum(jnp.abs(hf), axis=0, keepdims=True) + jnp.sum(jnp.abs(hb), axis=0, keepdims=True))
    hsum = (hf + hb) * inv
    hdif = (hf - hb) * inv
    sign = jnp.where(pos % 2 == 0, 1.0, -1.0)
    wk = jnp.where(pos == 0, 1.0, 2.0) * (1.0 / (2 * n))
    s_hi, s_lo = _split_bf16(hsum)
    d_hi, d_lo = _split_bf16(hdif)
    hr_ref[...] = (_dot(cn_ref[...], s_hi) + _dot(cn_ref[...], s_lo)) * wk
    hs_ref[...] = (_dot(sn_ref[...], d_hi) + _dot(sn_ref[...], d_lo)) * wk
    hn_ref[...] = jnp.sum(hsum * sign, axis=0, keepdims=True) * (1.0 / (2 * n))


def _hyena_filter(feats, w1, b1, w2, b2, w3, freq, deltas, cn, sn):
    n = feats.shape[0]
    w2o = HY_ORDER * HY_W
    args = (feats, w1, b1, w2, b2, w3, freq, deltas, cn, sn)
    return pl.pallas_call(
        _hyena_filter_kernel,
        grid=(1,),
        in_specs=[pl.BlockSpec(a.shape, lambda i: (0, 0)) for a in args],
        out_specs=[pl.BlockSpec((n, w2o), lambda i: (0, 0)), pl.BlockSpec((n, w2o), lambda i: (0, 0)),
                   pl.BlockSpec((1, w2o), lambda i: (0, 0))],
        out_shape=[jax.ShapeDtypeStruct((n, w2o), F32), jax.ShapeDtypeStruct((n, w2o), F32),
                   jax.ShapeDtypeStruct((1, w2o), F32)],
        compiler_params=_cparams(("arbitrary",)), name="hyena_filter",
    )(*args)


def _hyena_kernel(u_ref, sw_ref, sb_ref, hr_ref, hs_ref, hn_ref, d_ref, cn_ref, sn_ref, o_ref):
    n = u_ref.shape[0]
    pos = lax.broadcasted_iota(jnp.int32, (n, HY_W), 0)
    sign = jnp.where(pos % 2 == 0, 1.0, -1.0)

    def short(j):
        u = u_ref[:, j * HY_W:(j + 1) * HY_W].astype(F32)
        up = jnp.where(pos == 0, 0.0, pltpu.roll(u, 1, 0))
        un = jnp.where(pos == n - 1, 0.0, pltpu.roll(u, n - 1, 0))
        c = slice(j * HY_W, (j + 1) * HY_W)
        return sb_ref[0:1, c] + up * sw_ref[0:1, c] + u * sw_ref[1:2, c] + un * sw_ref[2:3, c]

    def long(z, o):
        c = slice(o * HY_W, (o + 1) * HY_W)
        zb = z.astype(BF16)
        xr = _dot(cn_ref[...], zb)
        xs = _dot(sn_ref[...], zb)
        hr = hr_ref[:, c]
        hs = hs_ref[:, c]
        yr = (xr * hr - xs * hs).astype(BF16)
        ys = (xr * hs + xs * hr).astype(BF16)
        y = _dot(cn_ref[...], yr) + _dot(sn_ref[...], ys)
        xn = jnp.sum(z * sign, axis=0, keepdims=True)
        return y + sign * (xn * hn_ref[0:1, c]) + z * d_ref[o:o + 1, :]

    z = short(1) * long(short(0), 0)
    o_ref[...] = (short(2) * long(z, 1)).astype(o_ref.dtype)


def _hyena(hy, sw, sb, hr, hs, hn, dbias, cn, sn, n):
    m = hy.shape[0]
    const = lambda b: (0, 0)
    big = lambda a: pl.BlockSpec(a.shape, const, pipeline_mode=pl.Buffered(1))
    return pl.pallas_call(
        _hyena_kernel,
        grid=(m // n,),
        in_specs=[pl.BlockSpec((n, hy.shape[1]), lambda b: (b, 0)),
                  pl.BlockSpec(sw.shape, const), pl.BlockSpec(sb.shape, const),
                  pl.BlockSpec(hr.shape, const), pl.BlockSpec(hs.shape, const), pl.BlockSpec(hn.shape, const),
                  pl.BlockSpec(dbias.shape, const), big(cn), big(sn)],
        out_specs=pl.BlockSpec((n, HY_W), lambda b: (b, 0)),
        out_shape=jax.ShapeDtypeStruct((m, HY_W), BF16),
        compiler_params=_cparams(("arbitrary",)), name="hyena",
    )(hy, sw, sb, hr, hs, hn, dbias, cn, sn)


def _fnet_kernel(u_ref, bc_ref, bs_ref, cf_ref, sf_ref, o_ref):
    n = u_ref.shape[0]
    u = u_ref[...]
    uc = _dot(u, bc_ref[...]).astype(BF16)
    us = _dot(u, bs_ref[...]).astype(BF16)
    y = _dot(cf_ref[...], uc) - _dot(sf_ref[...], us)
    o_ref[...] = (y * (1.0 / math.sqrt(n * FN_GROUP_W))).astype(o_ref.dtype)


def _fnet(fn, bc, bs, cf, sf, n):
    m = fn.shape[0]
    const = lambda b: (0, 0)
    big = lambda a: pl.BlockSpec(a.shape, const, pipeline_mode=pl.Buffered(1))
    return pl.pallas_call(
        _fnet_kernel,
        grid=(m // n,),
        in_specs=[pl.BlockSpec((n, FN_W), lambda b: (b, 0)), pl.BlockSpec(bc.shape, const),
                  pl.BlockSpec(bs.shape, const), big(cf), big(sf)],
        out_specs=pl.BlockSpec((n, FN_W), lambda b: (b, 0)),
        out_shape=jax.ShapeDtypeStruct((m, FN_W), BF16),
        compiler_params=_cparams(("arbitrary",)), name="fnet",
    )(fn, bc, bs, cf, sf)


def _merge_kernel(x_ref, mod_ref, g1_ref, g2_ref, att_ref, hy_ref, fn_ref, wg_ref, wb_ref, wo_ref,
                  wrh_ref, wrl_ref, x1_ref, h2_ref, lg_ref):
    x = x_ref[...]
    hb = _rms_mod(x, g1_ref[...], mod_ref[0:1, :], mod_ref[1:2, :]).astype(BF16)
    d = x.shape[1]
    branches = ((att_ref, 0, ATT_Q_W), (hy_ref, ATT_Q_W, ATT_Q_W + HY_W), (fn_ref, ATT_Q_W + HY_W, MIX_W))
    mix = None
    for i, (b_ref, lo, hi) in enumerate(branches):
        gate = _dot(hb, wg_ref[:, i * d:(i + 1) * d])
        gate = 1.0 / (1.0 + jnp.exp(-gate))
        term = gate * _dot(b_ref[...], wb_ref[lo:hi, :])
        mix = term if mix is None else mix + term
    y = _dot(mix.astype(BF16), wo_ref[...])
    x1 = x + mod_ref[2:3, :] * y
    x1_ref[...] = x1
    h2 = _rms_mod(x1, g2_ref[...], mod_ref[3:4, :], mod_ref[4:5, :])
    hi, lo = _split_bf16(h2)
    h2_ref[...] = hi
    lg_ref[...] = _dot(hi, wrh_ref[...]) + _dot(lo, wrh_ref[...]) + _dot(hi, wrl_ref[...])


def _merge(x, mod, g1, g2, att, hyo, fno, wg, wb, wo, wrh, wrl, n, tm):
    m, d = x.shape
    per_b = mod.shape[0] > 1
    tpb = n // tm
    row = lambda i: (i, 0)
    const = lambda i: (0, 0)
    mod_map = (lambda i: (i // tpb, 0, 0)) if per_b else (lambda i: (0, 0, 0))
    return pl.pallas_call(
        _merge_kernel,
        grid=(m // tm,),
        in_specs=[pl.BlockSpec((tm, d), row), pl.BlockSpec((None, 8, d), mod_map),
                  pl.BlockSpec((1, d), const), pl.BlockSpec((1, d), const),
                  pl.BlockSpec((tm, ATT_Q_W), row), pl.BlockSpec((tm, HY_W), row), pl.BlockSpec((tm, FN_W), row),
                  pl.BlockSpec(wg.shape, const), pl.BlockSpec(wb.shape, const), pl.BlockSpec(wo.shape, const),
                  pl.BlockSpec(wrh.shape, const), pl.BlockSpec(wrl.shape, const)],
        out_specs=[pl.BlockSpec((tm, d), row), pl.BlockSpec((tm, d), row), pl.BlockSpec((tm, LANES), row)],
        out_shape=[jax.ShapeDtypeStruct((m, d), F32), jax.ShapeDtypeStruct((m, d), BF16),
                   jax.ShapeDtypeStruct((m, LANES), F32)],
        compiler_params=_cparams(("arbitrary",)), name="merge",
    )(x, mod, g1, g2, att, hyo, fno, wg, wb, wo, wrh, wrl)


def _route_kernel(lg_ref, tri_ref, rank_ref, aff_ref, *, cap):
    n = lg_ref.shape[0]
    lt = lg_ref[...].T[0:N_EXPERTS, :]
    e = jnp.exp(lt - jnp.max(lt, axis=0, keepdims=True))
    aff = e / jnp.sum(e, axis=0, keepdims=True)
    bits = pltpu.bitcast(aff, jnp.int32)

    def step(i, thr):
        cand = thr | jnp.left_shift(jnp.int32(1), 30 - i)
        cnt = jnp.sum(jnp.where(bits >= cand, 1.0, 0.0), axis=1, keepdims=True)
        return jnp.where(cnt >= cap, cand, thr)

    thr = lax.fori_loop(0, 31, step, jnp.zeros((N_EXPERTS, 1), jnp.int32))
    gt = bits > thr
    eq = bits == thr
    need = cap - jnp.sum(jnp.where(gt, 1.0, 0.0), axis=1, keepdims=True)

    def excl_cumsum(mask):
        parts = []
        off = jnp.zeros((N_EXPERTS, 1), F32)
        for c in range(n // LANES):
            blk = mask[:, c * LANES:(c + 1) * LANES]
            parts.append(_dot(blk.astype(BF16), tri_ref[...]) + off)
            off = off + jnp.sum(blk, axis=1, keepdims=True)
        return jnp.concatenate(parts, axis=1)

    tie_rank = excl_cumsum(jnp.where(eq, 1.0, 0.0))
    sel = jnp.where(gt, 1.0, jnp.where(eq, jnp.where(tie_rank < need, 1.0, 0.0), 0.0))
    rank = excl_cumsum(sel)
    rank_ref[...] = jnp.where(sel > 0.5, rank, -1.0).astype(jnp.int32)
    aff_ref[...] = aff


def _route(logits, tri, n, cap):
    m = logits.shape[0]
    b = m // n
    return pl.pallas_call(
        functools.partial(_route_kernel, cap=cap),
        grid=(b,),
        in_specs=[pl.BlockSpec((n, LANES), lambda i: (i, 0)), pl.BlockSpec(tri.shape, lambda i: (0, 0))],
        out_specs=[pl.BlockSpec((None, N_EXPERTS, n), lambda i: (i, 0, 0)),
                   pl.BlockSpec((None, N_EXPERTS, n), lambda i: (i, 0, 0))],
        out_shape=[jax.ShapeDtypeStruct((b, N_EXPERTS, n), jnp.int32),
                   jax.ShapeDtypeStruct((b, N_EXPERTS, n), F32)],
        compiler_params=_cparams(("arbitrary",)), name="route",
    )(logits, tri)


def _moe_kernel(h_ref, rank_ref, aff_ref, wg_ref, wu_ref, wd_ref, o_ref, *, nb, n, cap):
    e = pl.program_id(1)
    d = h_ref.shape[1]
    slot = lax.broadcasted_iota(jnp.int32, (cap, n), 0)
    sels, xgs, wcols = [], [], []
    for b in range(nb):
        hit = slot == rank_ref[b]
        sel = jnp.where(hit, 1.0, 0.0).astype(BF16)
        sels.append(sel)
        xgs.append(_dot(sel, h_ref[b * n:(b + 1) * n, :]).astype(BF16))
        wcols.append(jnp.sum(jnp.where(hit, aff_ref[b], 0.0), axis=1, keepdims=True))
    xg = jnp.concatenate(xgs, axis=0) if nb > 1 else xgs[0]
    a = _dot(xg, wg_ref[...])
    u = _dot(xg, wu_ref[...])
    hm = (a * (1.0 / (1.0 + jnp.exp(-a))) * u).astype(BF16)
    y = _dot(hm, wd_ref[...])
    tn = 256
    for b in range(nb):
        yw = (y[b * cap:(b + 1) * cap, :] * wcols[b]).astype(BF16)
        for c in range(d // tn):
            contrib = lax.dot_general(sels[b], yw[:, c * tn:(c + 1) * tn], (((0,), (0,)), ((), ())),
                                      preferred_element_type=F32)
            rows, cols = slice(b * n, (b + 1) * n), slice(c * tn, (c + 1) * tn)

            @pl.when(e == 0)
            def _():
                o_ref[rows, cols] = contrib

            @pl.when(e > 0)
            def _():
                o_ref[rows, cols] += contrib


def _moe(h2, rank, aff, wg, wu, wd, n, cap, nb):
    m, d = h2.shape
    b = m // n
    ne, _, ff = wg.shape
    rank4 = rank.reshape(b, ne, 1, n)
    aff4 = aff.reshape(b, ne, 1, n)
    single = pl.Buffered(1)
    return pl.pallas_call(
        functools.partial(_moe_kernel, nb=nb, n=n, cap=cap),
        grid=(b // nb, ne),
        in_specs=[pl.BlockSpec((nb * n, d), lambda i, e: (i, 0), pipeline_mode=single),
                  pl.BlockSpec((nb, None, 1, n), lambda i, e: (i, e, 0, 0)),
                  pl.BlockSpec((nb, None, 1, n), lambda i, e: (i, e, 0, 0)),
                  pl.BlockSpec((None, d, ff), lambda i, e: (e, 0, 0)),
                  pl.BlockSpec((None, d, ff), lambda i, e: (e, 0, 0)),
                  pl.BlockSpec((None, ff, d), lambda i, e: (e, 0, 0))],
        out_specs=pl.BlockSpec((nb * n, d), lambda i, e: (i, 0)),
        out_shape=jax.ShapeDtypeStruct((m, d), F32),
        compiler_params=_cparams(("arbitrary", "arbitrary")), name="moe",
    )(h2, rank4, aff4, wg, wu, wd)


def _combine_kernel(x_ref, y_ref, mod_ref, g_ref, o_ref, *, norm):
    x = x_ref[...] + mod_ref[5:6, :] * y_ref[...]
    if norm:
        x = x * lax.rsqrt(jnp.mean(x * x, axis=-1, keepdims=True) + EPS) * g_ref[...]
    o_ref[...] = x


def _combine(x, y, mod, g, n, tm, norm):
    m, d = x.shape
    per_b = mod.shape[0] > 1
    tpb = n // tm
    row = lambda i: (i, 0)
    mod_map = (lambda i: (i // tpb, 0, 0)) if per_b else (lambda i: (0, 0, 0))
    return pl.pallas_call(
        functools.partial(_combine_kernel, norm=norm),
        grid=(m // tm,),
        in_specs=[pl.BlockSpec((tm, d), row), pl.BlockSpec((tm, d), row),
                  pl.BlockSpec((None, 8, d), mod_map), pl.BlockSpec((1, d), lambda i: (0, 0))],
        out_specs=pl.BlockSpec((tm, d), row),
        out_shape=jax.ShapeDtypeStruct((m, d), F32),
        compiler_params=_cparams(("arbitrary",)), name="combine",
    )(x, y, mod, g)


def _dft_tables(n, period):
    j = jnp.arange(n, dtype=jnp.int32)
    jk = (j[:, None] * j[None, :]) % period
    ang = jk.astype(F32) * (2.0 * math.pi / period)
    return jnp.cos(ang).astype(BF16), jnp.sin(ang).astype(BF16)


def _rope_tables(n):
    rows = n // GRID_W
    row = jnp.repeat(jnp.arange(rows, dtype=F32), GRID_W)
    col = jnp.tile(jnp.arange(GRID_W, dtype=F32), rows)
    inv = ROPE_THETA ** (-jnp.arange(0, ROPE_AXIS_DIM, 2, dtype=F32) / ROPE_AXIS_DIM)
    ang = jnp.concatenate([row[:, None] * inv, col[:, None] * inv], axis=-1)
    cos = jnp.repeat(jnp.cos(ang), 2, axis=1)
    sin = jnp.repeat(jnp.sin(ang), 2, axis=1) * jnp.tile(jnp.array([-1.0, 1.0], F32), HEAD_DIM // 2)
    return jnp.tile(cos, (1, LANES // HEAD_DIM)), jnp.tile(sin, (1, LANES // HEAD_DIM))


def _hyena_feats(n):
    pos = jnp.arange(n, dtype=F32)
    t = pos / (n - 1)
    bands = jnp.linspace(1e-4, HY_BANDS - 1, HY_BANDS, dtype=F32)
    ang = (2.0 * math.pi / n) * pos[:, None] * bands[None, :]
    feats = jnp.concatenate([t[:, None], jnp.cos(ang), -jnp.sin(ang)], axis=-1)
    return jnp.pad(feats, ((0, 0), (0, LANES - HY_EMB)))


def _pair_swap(a):
    s = a.shape
    return a.reshape(*s[:-1], s[-1] // 2, 2)[..., ::-1].reshape(s)


def _side(x, xb, n, mod, l, P, tabs, kv_ext, rope, tm, last_ctx):
    b = x.shape[0] // n
    outs = _inproj(x, xb, mod, P["norm1_g"][l], P["w_main"][l], P["w_sw"][l] if rope else None,
                   tabs["cos"] if rope else None, tabs["sin"] if rope else None,
                   P["qg"][l], P["kg"][l], P["gmat"], n, tm)
    if xb is not None:
        x, outs = outs[0], outs[1:]
    q, k, v, hy, fn = outs
    k3, v3 = k.reshape(b, n, ATT_KV_W), v.reshape(b, n, ATT_KV_W)
    if last_ctx:
        return None, None, k3, v3
    if kv_ext is not None:
        kc, vc = jnp.concatenate([k3, kv_ext[0]], axis=1), jnp.concatenate([v3, kv_ext[1]], axis=1)
    else:
        kc, vc = k3, v3
    att = _attention(q, kc, vc, n, min(n, 128))
    hr, hs, hn = _hyena_filter(tabs["feats"], P["hy_w1"][l], P["hy_b1"][l], P["hy_w2"][l], P["hy_b2"][l],
                               P["hy_w3"][l], P["hy_freq"][l], tabs["deltas"], tabs["cn"], tabs["sn"])
    hyo = _hyena(hy, P["hy_sw"][l], P["hy_sb"][l], hr, hs, hn, P["hy_bias"][l], tabs["cn"], tabs["sn"], n)
    fno = _fnet(fn, tabs["bc"], tabs["bs"], tabs["cf"], tabs["sf"], n)
    x1, h2, logits = _merge(x, mod, P["norm1_g"][l], P["norm2_g"][l], att, hyo, fno, P["w_gates"][l],
                            P["w_branch"][l], P["w_out"][l], P["wr_hi"][l], P["wr_lo"][l], n, tm)
    cap = CAPACITY_FACTOR * n // N_EXPERTS
    rank, aff = _route(logits, tabs["tri"], n, cap)
    nb = max(1, min(b, 512 // cap, 2048 // n))
    y = _moe(h2, rank, aff, P["w_gate"][l], P["w_up"][l], P["w_down"][l], n, cap, nb)
    return x1, y, k3, v3


def _tables(n, rope):
    cn, sn = _dft_tables(n, 2 * n)
    cf, sf = _dft_tables(n, n)
    a = jnp.arange(FN_W, dtype=jnp.int32)
    same = (a[:, None] // FN_GROUP_W) == (a[None, :] // FN_GROUP_W)
    ang = ((a[:, None] * a[None, :]) % FN_GROUP_W).astype(F32) * (2.0 * math.pi / FN_GROUP_W)
    deltas = jnp.abs(jnp.linspace(math.log(HY_DECAY_TARGET) / HY_DECAY_LONG_PCT,
                                  math.log(HY_DECAY_TARGET) / HY_DECAY_SHORT_PCT, HY_W, dtype=F32))
    i = jnp.arange(LANES, dtype=jnp.int32)
    tabs = dict(cn=cn, sn=sn, cf=cf, sf=sf,
                bc=jnp.where(same, jnp.cos(ang), 0.0).astype(BF16),
                bs=jnp.where(same, jnp.sin(ang), 0.0).astype(BF16),
                feats=_hyena_feats(n), deltas=jnp.tile(deltas, HY_ORDER)[None, :],
                tri=(i[:, None] < i[None, :]).astype(BF16))
    if rope:
        tabs["cos"], tabs["sin"] = _rope_tables(n)
    return tabs


def kernel(x, c, ctx, c_ctx, w_mod, b_mod, norm1_g, norm2_g, w_in, q_gain, k_gain, hy_short_w, hy_short_b,
           hy_f_w1, hy_f_b1, hy_f_w2, hy_f_b2, hy_f_w3, hy_f_freq, hy_bias, w_branch, w_out, w_router,
           w_gate, w_up, w_down, final_g):
    bsz, n_lat, d = x.shape
    n_ctx = ctx.shape[1]
    depth = w_mod.shape[0]
    assert d == D_MODEL and n_lat % LANES == 0 and n_ctx % LANES == 0

    wq, wk = w_in[:, :, OFF_Q:OFF_K], w_in[:, :, OFF_K:OFF_V]
    hid = jnp.arange(ATT_Q_W, dtype=jnp.int32) // HEAD_DIM
    P = dict(
        norm1_g=norm1_g[:, None, :], norm2_g=norm2_g[:, None, :],
        w_main=w_in[:, :, :OFF_GATE].astype(BF16),
        w_sw=jnp.concatenate([_pair_swap(wq), _pair_swap(wk)], axis=-1).astype(BF16),
        w_gates=w_in[:, :, OFF_GATE:].astype(BF16),
        qg=jnp.stack([jnp.tile(q_gain, (1, N_Q_HEADS)), jnp.tile(_pair_swap(q_gain), (1, N_Q_HEADS))], axis=1),
        kg=jnp.stack([jnp.tile(k_gain, (1, N_KV_HEADS)), jnp.tile(_pair_swap(k_gain), (1, N_KV_HEADS))], axis=1),
        gmat=(hid[:, None] == hid[None, :]).astype(BF16),
        hy_sw=hy_short_w, hy_sb=hy_short_b[:, None, :],
        hy_w1=jnp.pad(hy_f_w1, ((0, 0), (0, LANES - HY_EMB), (0, 0))), hy_b1=hy_f_b1[:, None, :],
        hy_w2=hy_f_w2, hy_b2=hy_f_b2[:, None, :], hy_w3=hy_f_w3, hy_freq=hy_f_freq[:, None, :],
        hy_bias=hy_bias,
        w_branch=w_branch.astype(BF16), w_out=w_out.astype(BF16),
        w_gate=w_gate.astype(BF16), w_up=w_up.astype(BF16), w_down=w_down.astype(BF16),
    )
    wr = jnp.pad(w_router, ((0, 0), (0, 0), (0, LANES - N_EXPERTS)))
    P["wr_hi"] = wr.astype(BF16)
    P["wr_lo"] = (wr - P["wr_hi"].astype(F32)).astype(BF16)

    tab_x = _tables(n_lat, True)
    tab_c = _tables(n_ctx, False)

    rows = -(-(bsz + 1) // 8) * 8
    c_all = jnp.concatenate([c, c_ctx[None, :], jnp.zeros((rows - bsz - 1, d), F32)], axis=0)
    mod = _modulation(c_all, w_mod, b_mod).reshape(depth, rows, 6, d)
    mod = jnp.pad(mod, ((0, 0), (0, 0), (0, 2), (0, 0)))

    tm_x, tm_c = min(n_lat, 256), min(n_ctx, 256)
    xs, xpend = x.reshape(bsz * n_lat, d), None
    cs, cpend = ctx.reshape(bsz * n_ctx, d), None
    for l in range(depth):
        last = l == depth - 1
        mod_x, mod_c = mod[l, :bsz], mod[l, bsz:bsz + 1]
        pm_x = None if l == 0 else jnp.concatenate([mod_x[:, :5], mod[l - 1, :bsz, 5:6], mod_x[:, 6:]], axis=1)
        pm_c = None if l == 0 else jnp.concatenate([mod_c[:, :5], mod[l - 1, bsz:bsz + 1, 5:6], mod_c[:, 6:]],
                                                   axis=1)
        c1, cy, kc, vc = _side(cs, cpend, n_ctx, mod_c if l == 0 else pm_c, l, P, tab_c, None, False, tm_c, last)
        x1, xy, _, _ = _side(xs, xpend, n_lat, mod_x if l == 0 else pm_x, l, P, tab_x, (kc, vc), True, tm_x, False)
        xs, xpend = x1, xy
        if not last:
            cs, cpend = c1, cy
    out = _combine(xs, xpend, mod[depth - 1, :bsz], final_g[None, :], n_lat, tm_x, True)
    return out.reshape(bsz, n_lat, d)
```

```python
import functools
import math

import jax
import jax.numpy as jnp
from jax import lax
from jax.experimental import pallas as pl
from jax.experimental.pallas import tpu as pltpu

F32 = jnp.float32
BF16 = jnp.bfloat16

D_MODEL = 1024
GRID_W = 64
HEAD_DIM = 64
N_Q_HEADS = 8
N_KV_HEADS = 2
Q_GROUP = N_Q_HEADS // N_KV_HEADS
ATT_Q_W = N_Q_HEADS * HEAD_DIM
ATT_KV_W = N_KV_HEADS * HEAD_DIM
ROPE_THETA = 10000.0
ROPE_AXIS_DIM = HEAD_DIM // 2
HY_W = D_MODEL // 4
HY_ORDER = 2
HY_SHORT = 3
HY_BANDS = 16
HY_EMB = 2 * HY_BANDS + 1
HY_FFN = 64
HY_DECAY_TARGET = 1e-2
HY_DECAY_SHORT_PCT = 0.3
HY_DECAY_LONG_PCT = 1.5
FN_GROUPS = 4
FN_GROUP_W = D_MODEL // 16
FN_W = FN_GROUPS * FN_GROUP_W
MIX_W = ATT_Q_W + HY_W + FN_W
N_BRANCH = 3
OFF_Q = 0
OFF_K = OFF_Q + ATT_Q_W
OFF_V = OFF_K + ATT_KV_W
OFF_HY = OFF_V + ATT_KV_W
OFF_FN = OFF_HY + (HY_ORDER + 1) * HY_W
OFF_GATE = OFF_FN + FN_W
N_EXPERTS = 16
CAPACITY_FACTOR = 2
EPS = 1e-6

LANES = 128
VMEM_LIMIT = 56 * 1024 * 1024

A_Q, A_K, A_V, A_HY, A_FN, A_END = 0, 512, 640, 768, 1536, 1792


def _cparams(sem):
    return pltpu.CompilerParams(dimension_semantics=sem, vmem_limit_bytes=VMEM_LIMIT)


def _dot(a, b):
    return jnp.dot(a, b, preferred_element_type=F32)


def _split_bf16(x):
    hi = x.astype(BF16)
    lo = (x - hi.astype(F32)).astype(BF16)
    return hi, lo


def _rms_mod(x, g, shift, scale):
    y = x * lax.rsqrt(jnp.mean(x * x, axis=-1, keepdims=True) + EPS)
    return (y * g) * (1.0 + scale) + shift


def _mod_kernel(c_ref, w_ref, b_ref, o_ref):
    c = c_ref[...]
    sc = c * (1.0 / (1.0 + jnp.exp(-c)))
    o_ref[...] = jnp.dot(sc, w_ref[...], preferred_element_type=F32,
                         precision=lax.Precision.HIGHEST) + b_ref[...]


def _modulation(c_all, w_mod, b_mod):
    depth, d, n6 = w_mod.shape
    rows = c_all.shape[0]
    tn = 1536
    return pl.pallas_call(
        _mod_kernel,
        grid=(depth, n6 // tn),
        in_specs=[pl.BlockSpec((rows, d), lambda l, j: (0, 0)),
                  pl.BlockSpec((None, d, tn), lambda l, j: (l, 0, j)),
                  pl.BlockSpec((None, 1, tn), lambda l, j: (l, 0, j))],
        out_specs=pl.BlockSpec((None, rows, tn), lambda l, j: (l, 0, j)),
        out_shape=jax.ShapeDtypeStruct((depth, rows, n6), F32),
        compiler_params=_cparams(("arbitrary", "arbitrary")),
        name="modulation",
    )(c_all, w_mod, b_mod.reshape(depth, 1, n6))


def _head_norm(u, gmat_ref, width):
    hi, lo = _split_bf16(u * u)
    gm = gmat_ref[0:width, 0:width]
    ms = (_dot(hi, gm) + _dot(lo, gm)) * (1.0 / HEAD_DIM)
    return lax.rsqrt(ms + EPS)


def _inproj_kernel(*refs, rope, combine):
    it = iter(refs)
    x_ref = next(it)
    if combine:
        xb_ref = next(it)
    mod_ref = next(it)
    g_ref = next(it)
    w_ref = next(it)
    if rope:
        wsw_ref = next(it)
        cos_ref = next(it)
        sin_ref = next(it)
    qg_ref = next(it)
    kg_ref = next(it)
    gmat_ref = next(it)
    if combine:
        xo_ref = next(it)
    q_ref = next(it)
    k_ref = next(it)
    v_ref = next(it)
    hy_ref = next(it)
    fn_ref = next(it)

    x = x_ref[...]
    if combine:
        x = x + mod_ref[5:6, :] * xb_ref[...]
        xo_ref[...] = x
    hb = _rms_mod(x, g_ref[...], mod_ref[0:1, :], mod_ref[1:2, :]).astype(BF16)

    def qk(lo, width, gain_ref, out_ref, scale):
        u = _dot(hb, w_ref[:, lo:lo + width])
        r = _head_norm(u, gmat_ref, width)
        un = u * r * gain_ref[0:1, :]
        if rope:
            off = lo - A_Q if lo == A_Q else ATT_Q_W
            usw = _dot(hb, wsw_ref[:, off:off + width])
            reps = width // LANES
            cos_t = jnp.concatenate([cos_ref[...]] * reps, axis=1) if reps > 1 else cos_ref[...]
            sin_t = jnp.concatenate([sin_ref[...]] * reps, axis=1) if reps > 1 else sin_ref[...]
            un = un * cos_t + (usw * r * gain_ref[1:2, :]) * sin_t
        out_ref[...] = (un * scale).astype(out_ref.dtype)

    qk(A_Q, ATT_Q_W, qg_ref, q_ref, HEAD_DIM ** -0.5 * math.log2(math.e))
    qk(A_K, ATT_KV_W, kg_ref, k_ref, 1.0)
    v_ref[...] = _dot(hb, w_ref[:, A_V:A_HY]).astype(v_ref.dtype)
    hy_ref[...] = _dot(hb, w_ref[:, A_HY:A_FN]).astype(hy_ref.dtype)
    fn_ref[...] = _dot(hb, w_ref[:, A_FN:A_END]).astype(fn_ref.dtype)


def _inproj(x, xb, mod, norm_g, w_main, w_sw, cos_t, sin_t, qg, kg, gmat, n, tm):
    m, d = x.shape
    rope = w_sw is not None
    combine = xb is not None
    per_b = mod.shape[0] > 1
    tpb = n // tm
    row = lambda i: (i, 0)
    const = lambda i: (0, 0)
    mod_map = (lambda i: (i // tpb, 0, 0)) if per_b else (lambda i: (0, 0, 0))
    args, specs = [x], [pl.BlockSpec((tm, d), row)]
    if combine:
        args.append(xb)
        specs.append(pl.BlockSpec((tm, d), row))
    args += [mod, norm_g, w_main]
    specs += [pl.BlockSpec((None, 8, d), mod_map), pl.BlockSpec((1, d), const),
              pl.BlockSpec(w_main.shape, const)]
    if rope:
        args += [w_sw, cos_t, sin_t]
        specs += [pl.BlockSpec(w_sw.shape, const),
                  pl.BlockSpec((tm, LANES), lambda i: (i % tpb, 0)),
                  pl.BlockSpec((tm, LANES), lambda i: (i % tpb, 0))]
    args += [qg, kg, gmat]
    specs += [pl.BlockSpec(qg.shape, const), pl.BlockSpec(kg.shape, const), pl.BlockSpec(gmat.shape, const)]
    widths = [ATT_Q_W, ATT_KV_W, ATT_KV_W, (HY_ORDER + 1) * HY_W, FN_W]
    out_shape = [jax.ShapeDtypeStruct((m, w), BF16) for w in widths]
    out_specs = [pl.BlockSpec((tm, w), row) for w in widths]
    if combine:
        out_shape.insert(0, jax.ShapeDtypeStruct((m, d), F32))
        out_specs.insert(0, pl.BlockSpec((tm, d), row))
    return pl.pallas_call(
        functools.partial(_inproj_kernel, rope=rope, combine=combine),
        grid=(m // tm,), in_specs=specs, out_specs=out_specs, out_shape=out_shape,
        compiler_params=_cparams(("arbitrary",)), name="inproj",
    )(*args)


def _attn_kernel(q_ref, kt_ref, v_ref, o_ref):
    tq = q_ref.shape[0]
    outs = []
    for g in range(N_KV_HEADS):
        heads = range(g * Q_GROUP, (g + 1) * Q_GROUP)
        qg = jnp.concatenate([q_ref[:, h * HEAD_DIM:(h + 1) * HEAD_DIM] for h in heads], axis=0)
        s = _dot(qg, kt_ref[g * HEAD_DIM:(g + 1) * HEAD_DIM, :])
        p = jnp.exp2(s - jnp.max(s, axis=-1, keepdims=True)).astype(BF16)
        o = _dot(p, v_ref[g])
        for j in range(Q_GROUP):
            oh = o[j * tq:(j + 1) * tq, :]
            outs.append(oh[:, 0:HEAD_DIM] / oh[:, HEAD_DIM:HEAD_DIM + 1])
    o_ref[...] = jnp.concatenate(outs, axis=1).astype(o_ref.dtype)


def _attention(q, kt, vaug, n, tq):
    m = q.shape[0]
    b, _, s = kt.shape
    tpb = n // tq
    return pl.pallas_call(
        _attn_kernel,
        grid=(b, tpb),
        in_specs=[pl.BlockSpec((tq, ATT_Q_W), lambda bi, i: (bi * tpb + i, 0)),
                  pl.BlockSpec((None, ATT_KV_W, s), lambda bi, i: (bi, 0, 0)),
                  pl.BlockSpec((None, N_KV_HEADS, s, LANES), lambda bi, i: (bi, 0, 0, 0))],
        out_specs=pl.BlockSpec((tq, ATT_Q_W), lambda bi, i: (bi * tpb + i, 0)),
        out_shape=jax.ShapeDtypeStruct((m, ATT_Q_W), BF16),
        compiler_params=_cparams(("arbitrary", "arbitrary")), name="attention",
    )(q, kt, vaug)


def _hyena_filter_kernel(feat_ref, w1_ref, b1_ref, w2_ref, b2_ref, w3_ref, freq_ref, delta_ref,
                         cn_ref, sn_ref, hr_ref, hs_ref, hn_ref):
    n = feat_ref.shape[0]
    hp = lax.Precision.HIGHEST
    freq = freq_ref[...]
    h = jnp.sin(freq * (jnp.dot(feat_ref[...], w1_ref[...], preferred_element_type=F32, precision=hp)
                        + b1_ref[...]))
    h = jnp.sin(freq * (jnp.dot(h, w2_ref[...], preferred_element_type=F32, precision=hp) + b2_ref[...]))
    h = jnp.dot(h, w3_ref[...], preferred_element_type=F32, precision=hp)
    w2o = HY_ORDER * HY_W
    pos = lax.broadcasted_iota(jnp.int32, (n, w2o), 0)
    t = pos.astype(F32) / (n - 1)
    decay = jnp.exp(-t * delta_ref[...])
    hf = h[:, 0:w2o] * decay
    hb = jnp.where(pos == 0, 0.0, h[:, w2o:2 * w2o] * decay)
    inv = 1.0 / (jnp.sum(jnp.abs(hf), axis=0, keepdims=True) + jnp.sum(jnp.abs(hb), axis=0, keepdims=True))
    hsum = (hf + hb) * inv
    hdif = (hf - hb) * inv
    sign = jnp.where(pos % 2 == 0, 1.0, -1.0)
    wk = jnp.where(pos == 0, 1.0, 2.0) * (1.0 / (2 * n))
    s_hi, s_lo = _split_bf16(hsum)
    d_hi, d_lo = _split_bf16(hdif)
    hr_ref[...] = (_dot(cn_ref[...], s_hi) + _dot(cn_ref[...], s_lo)) * wk
    hs_ref[...] = (_dot(sn_ref[...], d_hi) + _dot(sn_ref[...], d_lo)) * wk
    hn_ref[...] = jnp.sum(hsum * sign, axis=0, keepdims=True) * (1.0 / (2 * n))


def _hyena_filter(feats, w1, b1, w2, b2, w3, freq, deltas, cn, sn):
    n = feats.shape[0]
    w2o = HY_ORDER * HY_W
    args = (feats, w1, b1, w2, b2, w3, freq, deltas, cn, sn)
    return pl.pallas_call(
        _hyena_filter_kernel,
        grid=(1,),
        in_specs=[pl.BlockSpec(a.shape, lambda i: (0, 0)) for a in args],
        out_specs=[pl.BlockSpec((n, w2o), lambda i: (0, 0)), pl.BlockSpec((n, w2o), lambda i: (0, 0)),
                   pl.BlockSpec((1, w2o), lambda i: (0, 0))],
        out_shape=[jax.ShapeDtypeStruct((n, w2o), F32), jax.ShapeDtypeStruct((n, w2o), F32),
                   jax.ShapeDtypeStruct((1, w2o), F32)],
        compiler_params=_cparams(("arbitrary",)), name="hyena_filter",
    )(*args)


def _hyena_kernel(u_ref, sw_ref, sb_ref, hr_ref, hs_ref, hn_ref, d_ref, cn_ref, sn_ref, o_ref):
    n = u_ref.shape[0]
    pos = lax.broadcasted_iota(jnp.int32, (n, HY_W), 0)
    sign = jnp.where(pos % 2 == 0, 1.0, -1.0)

    def short(j):
        u = u_ref[:, j * HY_W:(j + 1) * HY_W].astype(F32)
        up = jnp.where(pos == 0, 0.0, pltpu.roll(u, 1, 0))
        un = jnp.where(pos == n - 1, 0.0, pltpu.roll(u, n - 1, 0))
        c = slice(j * HY_W, (j + 1) * HY_W)
        return sb_ref[0:1, c] + up * sw_ref[0:1, c] + u * sw_ref[1:2, c] + un * sw_ref[2:3, c]

    def long(z, o):
        c = slice(o * HY_W, (o + 1) * HY_W)
        zb = z.astype(BF16)
        xr = _dot(cn_ref[...], zb)
        xs = _dot(sn_ref[...], zb)
        hr = hr_ref[:, c]
        hs = hs_ref[:, c]
        yr = (xr * hr - xs * hs).astype(BF16)
        ys = (xr * hs + xs * hr).astype(BF16)
        y = _dot(cn_ref[...], yr) + _dot(sn_ref[...], ys)
        xn = jnp.sum(z * sign, axis=0, keepdims=True)
        return y + sign * (xn * hn_ref[0:1, c]) + z * d_ref[o:o + 1, :]

    z = short(1) * long(short(0), 0)
    o_ref[...] = (short(2) * long(z, 1)).astype(o_ref.dtype)


def _hyena(hy, sw, sb, hr, hs, hn, dbias, cn, sn, n):
    m = hy.shape[0]
    const = lambda b: (0, 0)
    big = lambda a: pl.BlockSpec(a.shape, const, pipeline_mode=pl.Buffered(1))
    return pl.pallas_call(
        _hyena_kernel,
        grid=(m // n,),
        in_specs=[pl.BlockSpec((n, hy.shape[1]), lambda b: (b, 0)),
                  pl.BlockSpec(sw.shape, const), pl.BlockSpec(sb.shape, const),
                  pl.BlockSpec(hr.shape, const), pl.BlockSpec(hs.shape, const), pl.BlockSpec(hn.shape, const),
                  pl.BlockSpec(dbias.shape, const), big(cn), big(sn)],
        out_specs=pl.BlockSpec((n, HY_W), lambda b: (b, 0)),
        out_shape=jax.ShapeDtypeStruct((m, HY_W), BF16),
        compiler_params=_cparams(("arbitrary",)), name="hyena",
    )(hy, sw, sb, hr, hs, hn, dbias, cn, sn)


def _fnet_kernel(u_ref, bc_ref, bs_ref, cf_ref, sf_ref, o_ref):
    n = u_ref.shape[0]
    u = u_ref[...]
    uc = _dot(u, bc_ref[...]).astype(BF16)
    us = _dot(u, bs_ref[...]).astype(BF16)
    y = _dot(cf_ref[...], uc) - _dot(sf_ref[...], us)
    o_ref[...] = (y * (1.0 / math.sqrt(n * FN_GROUP_W))).astype(o_ref.dtype)


def _fnet(fn, bc, bs, cf, sf, n):
    m = fn.shape[0]
    const = lambda b: (0, 0)
    big = lambda a: pl.BlockSpec(a.shape, const, pipeline_mode=pl.Buffered(1))
    return pl.pallas_call(
        _fnet_kernel,
        grid=(m // n,),
        in_specs=[pl.BlockSpec((n, FN_W), lambda b: (b, 0)), pl.BlockSpec(bc.shape, const),
                  pl.BlockSpec(bs.shape, const), big(cf), big(sf)],
        out_specs=pl.BlockSpec((n, FN_W), lambda b: (b, 0)),
        out_shape=jax.ShapeDtypeStruct((m, FN_W), BF16),
        compiler_params=_cparams(("arbitrary",)), name="fnet",
    )(fn, bc, bs, cf, sf)


def _merge_kernel(x_ref, mod_ref, g1_ref, g2_ref, att_ref, hy_ref, fn_ref, wg_ref, wb_ref, wo_ref,
                  wrh_ref, wrl_ref, x1_ref, h2_ref, lg_ref):
    x = x_ref[...]
    hb = _rms_mod(x, g1_ref[...], mod_ref[0:1, :], mod_ref[1:2, :]).astype(BF16)
    d = x.shape[1]
    branches = ((att_ref, 0, ATT_Q_W), (hy_ref, ATT_Q_W, ATT_Q_W + HY_W), (fn_ref, ATT_Q_W + HY_W, MIX_W))
    mix = None
    for i, (b_ref, lo, hi) in enumerate(branches):
        gate = _dot(hb, wg_ref[:, i * d:(i + 1) * d])
        gate = 1.0 / (1.0 + jnp.exp(-gate))
        term = gate * _dot(b_ref[...], wb_ref[lo:hi, :])
        mix = term if mix is None else mix + term
    y = _dot(mix.astype(BF16), wo_ref[...])
    x1 = x + mod_ref[2:3, :] * y
    x1_ref[...] = x1
    h2 = _rms_mod(x1, g2_ref[...], mod_ref[3:4, :], mod_ref[4:5, :])
    hi, lo = _split_bf16(h2)
    h2_ref[...] = hi
    lg_ref[...] = _dot(hi, wrh_ref[...]) + _dot(lo, wrh_ref[...]) + _dot(hi, wrl_ref[...])


def _merge(x, mod, g1, g2, att, hyo, fno, wg, wb, wo, wrh, wrl, n, tm):
    m, d = x.shape
    per_b = mod.shape[0] > 1
    tpb = n // tm
    row = lambda i: (i, 0)
    const = lambda i: (0, 0)
    mod_map = (lambda i: (i // tpb, 0, 0)) if per_b else (lambda i: (0, 0, 0))
    return pl.pallas_call(
        _merge_kernel,
        grid=(m // tm,),
        in_specs=[pl.BlockSpec((tm, d), row), pl.BlockSpec((None, 8, d), mod_map),
                  pl.BlockSpec((1, d), const), pl.BlockSpec((1, d), const),
                  pl.BlockSpec((tm, ATT_Q_W), row), pl.BlockSpec((tm, HY_W), row), pl.BlockSpec((tm, FN_W), row),
                  pl.BlockSpec(wg.shape, const), pl.BlockSpec(wb.shape, const), pl.BlockSpec(wo.shape, const),
                  pl.BlockSpec(wrh.shape, const), pl.BlockSpec(wrl.shape, const)],
        out_specs=[pl.BlockSpec((tm, d), row), pl.BlockSpec((tm, d), row), pl.BlockSpec((tm, LANES), row)],
        out_shape=[jax.ShapeDtypeStruct((m, d), F32), jax.ShapeDtypeStruct((m, d), BF16),
                   jax.ShapeDtypeStruct((m, LANES), F32)],
        compiler_params=_cparams(("arbitrary",)), name="merge",
    )(x, mod, g1, g2, att, hyo, fno, wg, wb, wo, wrh, wrl)


def _route_kernel(lg_ref, tri_ref, rank_ref, aff_ref, rankt_ref, *, cap):
    n = lg_ref.shape[0]
    lt = lg_ref[...].T[0:N_EXPERTS, :]
    e = jnp.exp(lt - jnp.max(lt, axis=0, keepdims=True))
    aff = e / jnp.sum(e, axis=0, keepdims=True)
    bits = pltpu.bitcast(aff, jnp.int32)

    def step(i, thr):
        cand = thr | jnp.left_shift(jnp.int32(1), 30 - i)
        cnt = jnp.sum(jnp.where(bits >= cand, 1.0, 0.0), axis=1, keepdims=True)
        return jnp.where(cnt >= cap, cand, thr)

    thr = lax.fori_loop(0, 31, step, jnp.zeros((N_EXPERTS, 1), jnp.int32))
    gt = bits > thr
    eq = bits == thr
    need = cap - jnp.sum(jnp.where(gt, 1.0, 0.0), axis=1, keepdims=True)

    def excl_cumsum(mask):
        parts = []
        off = jnp.zeros((N_EXPERTS, 1), F32)
        for c in range(n // LANES):
            blk = mask[:, c * LANES:(c + 1) * LANES]
            parts.append(_dot(blk.astype(BF16), tri_ref[...]) + off)
            off = off + jnp.sum(blk, axis=1, keepdims=True)
        return jnp.concatenate(parts, axis=1)

    tie_rank = excl_cumsum(jnp.where(eq, 1.0, 0.0))
    sel = jnp.where(gt, 1.0, jnp.where(eq, jnp.where(tie_rank < need, 1.0, 0.0), 0.0))
    rank = jnp.where(sel > 0.5, excl_cumsum(sel), -1.0)
    rank_ref[...] = rank.astype(jnp.int32)
    aff_ref[...] = aff
    pad = jnp.full((LANES - N_EXPERTS, n), -1.0, F32)
    rankt_ref[...] = jnp.concatenate([rank, pad], axis=0).T.astype(BF16)


def _route(logits, tri, n, cap):
    m = logits.shape[0]
    b = m // n
    assert cap <= 256
    return pl.pallas_call(
        functools.partial(_route_kernel, cap=cap),
        grid=(b,),
        in_specs=[pl.BlockSpec((n, LANES), lambda i: (i, 0)), pl.BlockSpec(tri.shape, lambda i: (0, 0))],
        out_specs=[pl.BlockSpec((None, N_EXPERTS, n), lambda i: (i, 0, 0)),
                   pl.BlockSpec((None, N_EXPERTS, n), lambda i: (i, 0, 0)),
                   pl.BlockSpec((n, LANES), lambda i: (i, 0))],
        out_shape=[jax.ShapeDtypeStruct((b, N_EXPERTS, n), jnp.int32),
                   jax.ShapeDtypeStruct((b, N_EXPERTS, n), F32),
                   jax.ShapeDtypeStruct((m, LANES), BF16)],
        compiler_params=_cparams(("arbitrary",)), name="route",
    )(logits, tri)


def _gather_kernel(h_ref, rank_ref, aff_ref, xg_ref, w_ref, sel_ref, *, cap):
    n, d = h_ref.shape
    slot = lax.broadcasted_iota(jnp.int32, (cap, n), 0)
    for e in range(N_EXPERTS):
        hit = slot == rank_ref[e:e + 1, :]
        sel_ref[e * cap:(e + 1) * cap, :] = jnp.where(hit, 1.0, 0.0).astype(BF16)
        wcol = jnp.sum(jnp.where(hit, aff_ref[e:e + 1, :], 0.0), axis=1, keepdims=True)
        w_ref[e] = jnp.broadcast_to(wcol, (cap, LANES))
    tn = 256
    for c in range(d // tn):
        xg = _dot(sel_ref[...], h_ref[:, c * tn:(c + 1) * tn]).astype(BF16)
        for e in range(N_EXPERTS):
            xg_ref[e, :, c * tn:(c + 1) * tn] = xg[e * cap:(e + 1) * cap, :]


def _gather(h2, rank, aff, n, cap):
    m, d = h2.shape
    b = m // n
    return pl.pallas_call(
        functools.partial(_gather_kernel, cap=cap),
        grid=(b,),
        in_specs=[pl.BlockSpec((n, d), lambda i: (i, 0)),
                  pl.BlockSpec((None, N_EXPERTS, n), lambda i: (i, 0, 0)),
                  pl.BlockSpec((None, N_EXPERTS, n), lambda i: (i, 0, 0))],
        out_specs=[pl.BlockSpec((N_EXPERTS, None, cap, d), lambda i: (0, i, 0, 0)),
                   pl.BlockSpec((N_EXPERTS, None, cap, LANES), lambda i: (0, i, 0, 0))],
        out_shape=[jax.ShapeDtypeStruct((N_EXPERTS, b, cap, d), BF16),
                   jax.ShapeDtypeStruct((N_EXPERTS, b, cap, LANES), F32)],
        scratch_shapes=[pltpu.VMEM((N_EXPERTS * cap, n), BF16)],
        compiler_params=_cparams(("arbitrary",)), name="moe_gather",
    )(h2, rank, aff)


def _ffn_kernel(x_ref, w_ref, wg_ref, wu_ref, wd_ref, o_ref):
    x = x_ref[...]
    ff = wg_ref.shape[1]
    tf = 512
    y = None
    for c in range(ff // tf):
        cols = slice(c * tf, (c + 1) * tf)
        a = _dot(x, wg_ref[:, cols])
        u = _dot(x, wu_ref[:, cols])
        hm = (a * (1.0 / (1.0 + jnp.exp(-a))) * u).astype(BF16)
        t = _dot(hm, wd_ref[cols, :])
        y = t if y is None else y + t
    wt = jnp.concatenate([w_ref[...]] * (y.shape[1] // LANES), axis=1)
    o_ref[...] = (y * wt).astype(o_ref.dtype)


def _ffn(xg, w, wg, wu, wd, tm):
    ne, rows, d = xg.shape
    ff = wg.shape[2]
    return pl.pallas_call(
        _ffn_kernel,
        grid=(ne, rows // tm),
        in_specs=[pl.BlockSpec((None, tm, d), lambda e, i: (e, i, 0)),
                  pl.BlockSpec((None, tm, LANES), lambda e, i: (e, i, 0)),
                  pl.BlockSpec((None, d, ff), lambda e, i: (e, 0, 0)),
                  pl.BlockSpec((None, d, ff), lambda e, i: (e, 0, 0)),
                  pl.BlockSpec((None, ff, d), lambda e, i: (e, 0, 0))],
        out_specs=pl.BlockSpec((None, tm, d), lambda e, i: (e, i, 0)),
        out_shape=jax.ShapeDtypeStruct((ne, rows, d), BF16),
        compiler_params=_cparams(("arbitrary", "arbitrary")), name="moe_ffn",
    )(xg, w, wg, wu, wd)


def _scatter_kernel(rankt_ref, y_ref, expand_ref, slot_ref, o_ref, selt_ref, *, cap):
    d = o_ref.shape[1]
    width = N_EXPERTS * cap
    tk = min(width, 1024)
    for c in range(width // tk):
        cols = slice(c * tk, (c + 1) * tk)
        r = _dot(rankt_ref[...], expand_ref[:, cols])
        selt_ref[:, cols] = jnp.where(r == slot_ref[:, cols], 1.0, 0.0).astype(BF16)
    y = y_ref[...].reshape(width, d)
    tn = 256
    for c in range(d // tn):
        o_ref[:, c * tn:(c + 1) * tn] = _dot(selt_ref[...], y[:, c * tn:(c + 1) * tn])


def _scatter(rankt, yw, expand, slotpat, n, cap):
    ne, b, _, d = yw.shape
    return pl.pallas_call(
        functools.partial(_scatter_kernel, cap=cap),
        grid=(b,),
        in_specs=[pl.BlockSpec((n, LANES), lambda i: (i, 0)),
                  pl.BlockSpec((ne, None, cap, d), lambda i: (0, i, 0, 0), pipeline_mode=pl.Buffered(1)),
                  pl.BlockSpec(expand.shape, lambda i: (0, 0)),
                  pl.BlockSpec(slotpat.shape, lambda i: (0, 0))],
        out_specs=pl.BlockSpec((n, d), lambda i: (i, 0)),
        out_shape=jax.ShapeDtypeStruct((b * n, d), F32),
        scratch_shapes=[pltpu.VMEM((n, ne * cap), BF16)],
        compiler_params=_cparams(("arbitrary",)), name="moe_scatter",
    )(rankt, yw, expand, slotpat)


def _moe(h2, rank, aff, rankt, wg, wu, wd, tabs, n, cap):
    b = h2.shape[0] // n
    d = h2.shape[1]
    xg, w = _gather(h2, rank, aff, n, cap)
    rows = b * cap
    yw = _ffn(xg.reshape(N_EXPERTS, rows, d), w.reshape(N_EXPERTS, rows, LANES), wg, wu, wd, min(rows, 1024))
    return _scatter(rankt, yw.reshape(N_EXPERTS, b, cap, d), tabs["expand"], tabs["slotpat"], n, cap)


def _combine_kernel(x_ref, y_ref, mod_ref, g_ref, o_ref, *, norm):
    x = x_ref[...] + mod_ref[5:6, :] * y_ref[...]
    if norm:
        x = x * lax.rsqrt(jnp.mean(x * x, axis=-1, keepdims=True) + EPS) * g_ref[...]
    o_ref[...] = x


def _combine(x, y, mod, g, n, tm, norm):
    m, d = x.shape
    per_b = mod.shape[0] > 1
    tpb = n // tm
    row = lambda i: (i, 0)
    mod_map = (lambda i: (i // tpb, 0, 0)) if per_b else (lambda i: (0, 0, 0))
    return pl.pallas_call(
        functools.partial(_combine_kernel, norm=norm),
        grid=(m // tm,),
        in_specs=[pl.BlockSpec((tm, d), row), pl.BlockSpec((tm, d), row),
                  pl.BlockSpec((None, 8, d), mod_map), pl.BlockSpec((1, d), lambda i: (0, 0))],
        out_specs=pl.BlockSpec((tm, d), row),
        out_shape=jax.ShapeDtypeStruct((m, d), F32),
        compiler_params=_cparams(("arbitrary",)), name="combine",
    )(x, y, mod, g)


def _dft_tables(n, period):
    j = jnp.arange(n, dtype=jnp.int32)
    jk = (j[:, None] * j[None, :]) % period
    ang = jk.astype(F32) * (2.0 * math.pi / period)
    return jnp.cos(ang).astype(BF16), jnp.sin(ang).astype(BF16)


def _rope_tables(n):
    rows = n // GRID_W
    row = jnp.repeat(jnp.arange(rows, dtype=F32), GRID_W)
    col = jnp.tile(jnp.arange(GRID_W, dtype=F32), rows)
    inv = ROPE_THETA ** (-jnp.arange(0, ROPE_AXIS_DIM, 2, dtype=F32) / ROPE_AXIS_DIM)
    ang = jnp.concatenate([row[:, None] * inv, col[:, None] * inv], axis=-1)
    cos = jnp.repeat(jnp.cos(ang), 2, axis=1)
    sin = jnp.repeat(jnp.sin(ang), 2, axis=1) * jnp.tile(jnp.array([-1.0, 1.0], F32), HEAD_DIM // 2)
    return jnp.tile(cos, (1, LANES // HEAD_DIM)), jnp.tile(sin, (1, LANES // HEAD_DIM))


def _hyena_feats(n):
    pos = jnp.arange(n, dtype=F32)
    t = pos / (n - 1)
    bands = jnp.linspace(1e-4, HY_BANDS - 1, HY_BANDS, dtype=F32)
    ang = (2.0 * math.pi / n) * pos[:, None] * bands[None, :]
    feats = jnp.concatenate([t[:, None], jnp.cos(ang), -jnp.sin(ang)], axis=-1)
    return jnp.pad(feats, ((0, 0), (0, LANES - HY_EMB)))


def _pair_swap(a):
    s = a.shape
    return a.reshape(*s[:-1], s[-1] // 2, 2)[..., ::-1].reshape(s)


def _side(x, xb, n, mod, l, P, tabs, kv_ext, rope, tm, last_ctx):
    b = x.shape[0] // n
    outs = _inproj(x, xb, mod, P["norm1_g"][l], P["w_main"][l], P["w_sw"][l] if rope else None,
                   tabs["cos"] if rope else None, tabs["sin"] if rope else None,
                   P["qg"][l], P["kg"][l], P["gmat"], n, tm)
    if xb is not None:
        x, outs = outs[0], outs[1:]
    q, k, v, hy, fn = outs
    k3, v3 = k.reshape(b, n, ATT_KV_W), v.reshape(b, n, ATT_KV_W)
    if last_ctx:
        return None, None, k3, v3
    if kv_ext is not None:
        kc, vc = jnp.concatenate([k3, kv_ext[0]], axis=1), jnp.concatenate([v3, kv_ext[1]], axis=1)
    else:
        kc, vc = k3, v3
    s = kc.shape[1]
    fill = jnp.concatenate([jnp.ones((b, s, 1), BF16), jnp.zeros((b, s, LANES - HEAD_DIM - 1), BF16)], axis=-1)
    vaug = jnp.stack([jnp.concatenate([vc[..., g * HEAD_DIM:(g + 1) * HEAD_DIM], fill], axis=-1)
                      for g in range(N_KV_HEADS)], axis=1)
    att = _attention(q, jnp.swapaxes(kc, 1, 2), vaug, n, min(n, 256))
    hr, hs, hn = _hyena_filter(tabs["feats"], P["hy_w1"][l], P["hy_b1"][l], P["hy_w2"][l], P["hy_b2"][l],
                               P["hy_w3"][l], P["hy_freq"][l], tabs["deltas"], tabs["cn"], tabs["sn"])
    hyo = _hyena(hy, P["hy_sw"][l], P["hy_sb"][l], hr, hs, hn, P["hy_bias"][l], tabs["cn"], tabs["sn"], n)
    fno = _fnet(fn, tabs["bc"], tabs["bs"], tabs["cf"], tabs["sf"], n)
    x1, h2, logits = _merge(x, mod, P["norm1_g"][l], P["norm2_g"][l], att, hyo, fno, P["w_gates"][l],
                            P["w_branch"][l], P["w_out"][l], P["wr_hi"][l], P["wr_lo"][l], n, tm)
    cap = CAPACITY_FACTOR * n // N_EXPERTS
    rank, aff, rankt = _route(logits, tabs["tri"], n, cap)
    y = _moe(h2, rank, aff, rankt, P["w_gate"][l], P["w_up"][l], P["w_down"][l], tabs, n, cap)
    return x1, y, k3, v3


def _tables(n, rope):
    cn, sn = _dft_tables(n, 2 * n)
    cf, sf = _dft_tables(n, n)
    a = jnp.arange(FN_W, dtype=jnp.int32)
    same = (a[:, None] // FN_GROUP_W) == (a[None, :] // FN_GROUP_W)
    ang = ((a[:, None] * a[None, :]) % FN_GROUP_W).astype(F32) * (2.0 * math.pi / FN_GROUP_W)
    deltas = jnp.abs(jnp.linspace(math.log(HY_DECAY_TARGET) / HY_DECAY_LONG_PCT,
                                  math.log(HY_DECAY_TARGET) / HY_DECAY_SHORT_PCT, HY_W, dtype=F32))
    i = jnp.arange(LANES, dtype=jnp.int32)
    cap = CAPACITY_FACTOR * n // N_EXPERTS
    j = jnp.arange(N_EXPERTS * cap, dtype=jnp.int32)
    tabs = dict(expand=(i[:, None] == j[None, :] // cap).astype(BF16),
                slotpat=(j % cap).astype(F32)[None, :],
                cn=cn, sn=sn, cf=cf, sf=sf,
                bc=jnp.where(same, jnp.cos(ang), 0.0).astype(BF16),
                bs=jnp.where(same, jnp.sin(ang), 0.0).astype(BF16),
                feats=_hyena_feats(n), deltas=jnp.tile(deltas, HY_ORDER)[None, :],
                tri=(i[:, None] < i[None, :]).astype(BF16))
    if rope:
        tabs["cos"], tabs["sin"] = _rope_tables(n)
    return tabs


def kernel(x, c, ctx, c_ctx, w_mod, b_mod, norm1_g, norm2_g, w_in, q_gain, k_gain, hy_short_w, hy_short_b,
           hy_f_w1, hy_f_b1, hy_f_w2, hy_f_b2, hy_f_w3, hy_f_freq, hy_bias, w_branch, w_out, w_router,
           w_gate, w_up, w_down, final_g):
    bsz, n_lat, d = x.shape
    n_ctx = ctx.shape[1]
    depth = w_mod.shape[0]
    assert d == D_MODEL and n_lat % LANES == 0 and n_ctx % LANES == 0

    wq, wk = w_in[:, :, OFF_Q:OFF_K], w_in[:, :, OFF_K:OFF_V]
    hid = jnp.arange(ATT_Q_W, dtype=jnp.int32) // HEAD_DIM
    P = dict(
        norm1_g=norm1_g[:, None, :], norm2_g=norm2_g[:, None, :],
        w_main=w_in[:, :, :OFF_GATE].astype(BF16),
        w_sw=jnp.concatenate([_pair_swap(wq), _pair_swap(wk)], axis=-1).astype(BF16),
        w_gates=w_in[:, :, OFF_GATE:].astype(BF16),
        qg=jnp.stack([jnp.tile(q_gain, (1, N_Q_HEADS)), jnp.tile(_pair_swap(q_gain), (1, N_Q_HEADS))], axis=1),
        kg=jnp.stack([jnp.tile(k_gain, (1, N_KV_HEADS)), jnp.tile(_pair_swap(k_gain), (1, N_KV_HEADS))], axis=1),
        gmat=(hid[:, None] == hid[None, :]).astype(BF16),
        hy_sw=hy_short_w, hy_sb=hy_short_b[:, None, :],
        hy_w1=jnp.pad(hy_f_w1, ((0, 0), (0, LANES - HY_EMB), (0, 0))), hy_b1=hy_f_b1[:, None, :],
        hy_w2=hy_f_w2, hy_b2=hy_f_b2[:, None, :], hy_w3=hy_f_w3, hy_freq=hy_f_freq[:, None, :],
        hy_bias=hy_bias,
        w_branch=w_branch.astype(BF16), w_out=w_out.astype(BF16),
        w_gate=w_gate.astype(BF16), w_up=w_up.astype(BF16), w_down=w_down.astype(BF16),
    )
    wr = jnp.pad(w_router, ((0, 0), (0, 0), (0, LANES - N_EXPERTS)))
    P["wr_hi"] = wr.astype(BF16)
    P["wr_lo"] = (wr - P["wr_hi"].astype(F32)).astype(BF16)

    tab_x = _tables(n_lat, True)
    tab_c = _tables(n_ctx, False)

    rows = -(-(bsz + 1) // 8) * 8
    c_all = jnp.concatenate([c, c_ctx[None, :], jnp.zeros((rows - bsz - 1, d), F32)], axis=0)
    mod = _modulation(c_all, w_mod, b_mod).reshape(depth, rows, 6, d)
    mod = jnp.pad(mod, ((0, 0), (0, 0), (0, 2), (0, 0)))

    tm_x, tm_c = min(n_lat, 512), min(n_ctx, 512)
    xs, xpend = x.reshape(bsz * n_lat, d), None
    cs, cpend = ctx.reshape(bsz * n_ctx, d), None
    for l in range(depth):
        last = l == depth - 1
        mod_x, mod_c = mod[l, :bsz], mod[l, bsz:bsz + 1]
        pm_x = None if l == 0 else jnp.concatenate([mod_x[:, :5], mod[l - 1, :bsz, 5:6], mod_x[:, 6:]], axis=1)
        pm_c = None if l == 0 else jnp.concatenate([mod_c[:, :5], mod[l - 1, bsz:bsz + 1, 5:6], mod_c[:, 6:]],
                                                   axis=1)
        c1, cy, kc, vc = _side(cs, cpend, n_ctx, mod_c if l == 0 else pm_c, l, P, tab_c, None, False, tm_c, last)
        x1, xy, _, _ = _side(xs, xpend, n_lat, mod_x if l == 0 else pm_x, l, P, tab_x, (kc, vc), True, tm_x, False)
        xs, xpend = x1, xy
        if not last:
            cs, cpend = c1, cy
    out = _combine(xs, xpend, mod[depth - 1, :bsz], final_g[None, :], n_lat, tm_x, True)
    return out.reshape(bsz, n_lat, d)
```

```python
import functools
import math

import jax
import jax.numpy as jnp
from jax import lax
from jax.experimental import pallas as pl
from jax.experimental.pallas import tpu as pltpu

F32 = jnp.float32
BF16 = jnp.bfloat16

D_MODEL = 1024
GRID_W = 64
HEAD_DIM = 64
N_Q_HEADS = 8
N_KV_HEADS = 2
Q_GROUP = N_Q_HEADS // N_KV_HEADS
ATT_Q_W = N_Q_HEADS * HEAD_DIM
ATT_KV_W = N_KV_HEADS * HEAD_DIM
ROPE_THETA = 10000.0
ROPE_AXIS_DIM = HEAD_DIM // 2
HY_W = D_MODEL // 4
HY_ORDER = 2
HY_SHORT = 3
HY_BANDS = 16
HY_EMB = 2 * HY_BANDS + 1
HY_FFN = 64
HY_DECAY_TARGET = 1e-2
HY_DECAY_SHORT_PCT = 0.3
HY_DECAY_LONG_PCT = 1.5
FN_GROUPS = 4
FN_GROUP_W = D_MODEL // 16
FN_W = FN_GROUPS * FN_GROUP_W
MIX_W = ATT_Q_W + HY_W + FN_W
N_BRANCH = 3
OFF_Q = 0
OFF_K = OFF_Q + ATT_Q_W
OFF_V = OFF_K + ATT_KV_W
OFF_HY = OFF_V + ATT_KV_W
OFF_FN = OFF_HY + (HY_ORDER + 1) * HY_W
OFF_GATE = OFF_FN + FN_W
N_EXPERTS = 16
CAPACITY_FACTOR = 2
EPS = 1e-6

LANES = 128
VMEM_LIMIT = 56 * 1024 * 1024
FFN_ROW_TILE = 1024
FFN_HIDDEN_TILE = 1024
HY_CHUNK = 512
MERGE_CHUNK = 256
ATTN_HEADS_PER_CHAIN = 4

A_Q, A_K, A_V, A_HY, A_FN, A_END = 0, 512, 640, 768, 1536, 1792


def _cparams(sem):
    return pltpu.CompilerParams(dimension_semantics=sem, vmem_limit_bytes=VMEM_LIMIT)


def _dot(a, b):
    return jnp.dot(a, b, preferred_element_type=F32)


def _split_bf16(x):
    hi = x.astype(BF16)
    lo = (x - hi.astype(F32)).astype(BF16)
    return hi, lo


def _rms_mod(x, g, shift, scale):
    y = x * lax.rsqrt(jnp.mean(x * x, axis=-1, keepdims=True) + EPS)
    return (y * g) * (1.0 + scale) + shift


def _mod_kernel(c_ref, w_ref, b_ref, o_ref):
    c = c_ref[...]
    sc = c * (1.0 / (1.0 + jnp.exp(-c)))
    o_ref[...] = jnp.dot(sc, w_ref[...], preferred_element_type=F32,
                         precision=lax.Precision.HIGHEST) + b_ref[...]


def _modulation(c_all, w_mod, b_mod):
    depth, d, n6 = w_mod.shape
    rows = c_all.shape[0]
    tn = 1536
    return pl.pallas_call(
        _mod_kernel,
        grid=(depth, n6 // tn),
        in_specs=[pl.BlockSpec((rows, d), lambda l, j: (0, 0)),
                  pl.BlockSpec((None, d, tn), lambda l, j: (l, 0, j)),
                  pl.BlockSpec((None, 1, tn), lambda l, j: (l, 0, j))],
        out_specs=pl.BlockSpec((None, rows, tn), lambda l, j: (l, 0, j)),
        out_shape=jax.ShapeDtypeStruct((depth, rows, n6), F32),
        compiler_params=_cparams(("arbitrary", "arbitrary")),
        name="modulation",
    )(c_all, w_mod, b_mod.reshape(depth, 1, n6))


def _head_norm(u, gmat_ref, width):
    hi, lo = _split_bf16(u * u)
    gm = gmat_ref[0:width, 0:width]
    ms = (_dot(hi, gm) + _dot(lo, gm)) * (1.0 / HEAD_DIM)
    return lax.rsqrt(ms + EPS)


def _inproj_kernel(*refs, rope, combine):
    it = iter(refs)
    x_ref = next(it)
    if combine:
        xb_ref = next(it)
    mod_ref = next(it)
    g_ref = next(it)
    w_ref = next(it)
    if rope:
        wsw_ref = next(it)
        cos_ref = next(it)
        sin_ref = next(it)
    qg_ref = next(it)
    kg_ref = next(it)
    gmat_ref = next(it)
    if combine:
        xo_ref = next(it)
    q_ref = next(it)
    k_ref = next(it)
    v_ref = next(it)
    hy_ref = next(it)
    fn_ref = next(it)

    x = x_ref[...]
    if combine:
        x = x + mod_ref[5:6, :] * xb_ref[...]
        xo_ref[...] = x
    hb = _rms_mod(x, g_ref[...], mod_ref[0:1, :], mod_ref[1:2, :]).astype(BF16)

    def qk(lo, width, gain_ref, out_ref, scale):
        u = _dot(hb, w_ref[:, lo:lo + width])
        r = _head_norm(u, gmat_ref, width)
        un = u * r * gain_ref[0:1, :]
        if rope:
            off = lo - A_Q if lo == A_Q else ATT_Q_W
            usw = _dot(hb, wsw_ref[:, off:off + width])
            reps = width // LANES
            cos_t = jnp.concatenate([cos_ref[...]] * reps, axis=1) if reps > 1 else cos_ref[...]
            sin_t = jnp.concatenate([sin_ref[...]] * reps, axis=1) if reps > 1 else sin_ref[...]
            un = un * cos_t + (usw * r * gain_ref[1:2, :]) * sin_t
        out_ref[...] = (un * scale).astype(out_ref.dtype)

    qk(A_Q, ATT_Q_W, qg_ref, q_ref, HEAD_DIM ** -0.5 * math.log2(math.e))
    qk(A_K, ATT_KV_W, kg_ref, k_ref, 1.0)
    v_ref[...] = _dot(hb, w_ref[:, A_V:A_HY]).astype(v_ref.dtype)
    hy_ref[...] = _dot(hb, w_ref[:, A_HY:A_FN]).astype(hy_ref.dtype)
    fn_ref[...] = _dot(hb, w_ref[:, A_FN:A_END]).astype(fn_ref.dtype)


def _inproj(x, xb, mod, norm_g, w_in, layer, w_sw, cos_t, sin_t, qg, kg, gmat, n, tm):
    m, d = x.shape
    rope = w_sw is not None
    combine = xb is not None
    per_b = mod.shape[0] > 1
    tpb = n // tm
    row = lambda i: (i, 0)
    const = lambda i: (0, 0)
    mod_map = (lambda i: (i // tpb, 0, 0)) if per_b else (lambda i: (0, 0, 0))
    args, specs = [x], [pl.BlockSpec((tm, d), row)]
    if combine:
        args.append(xb)
        specs.append(pl.BlockSpec((tm, d), row))
    args += [mod, norm_g, w_in]
    specs += [pl.BlockSpec((None, 8, d), mod_map), pl.BlockSpec((1, d), const),
              pl.BlockSpec((None, d, A_END), lambda i: (layer, 0, 0))]
    if rope:
        args += [w_sw, cos_t, sin_t]
        specs += [pl.BlockSpec(w_sw.shape, const),
                  pl.BlockSpec((tm, LANES), lambda i: (i % tpb, 0)),
                  pl.BlockSpec((tm, LANES), lambda i: (i % tpb, 0))]
    args += [qg, kg, gmat]
    specs += [pl.BlockSpec(qg.shape, const), pl.BlockSpec(kg.shape, const), pl.BlockSpec(gmat.shape, const)]
    widths = [ATT_Q_W, ATT_KV_W, ATT_KV_W, (HY_ORDER + 1) * HY_W, FN_W]
    out_shape = [jax.ShapeDtypeStruct((m, w), BF16) for w in widths]
    out_specs = [pl.BlockSpec((tm, w), row) for w in widths]
    if combine:
        out_shape.insert(0, jax.ShapeDtypeStruct((m, d), F32))
        out_specs.insert(0, pl.BlockSpec((tm, d), row))
    return pl.pallas_call(
        functools.partial(_inproj_kernel, rope=rope, combine=combine),
        grid=(m // tm,), in_specs=specs, out_specs=out_specs, out_shape=out_shape,
        compiler_params=_cparams(("arbitrary",)), name="inproj",
    )(*args)


def _attn_kernel(q_ref, kt_ref, v_ref, o_ref):
    tq = q_ref.shape[0]
    hpc = ATTN_HEADS_PER_CHAIN
    chains = [(h0 // Q_GROUP, range(h0, h0 + hpc)) for h0 in range(0, N_Q_HEADS, hpc)]
    ss = []
    for g, heads in chains:
        qg = jnp.concatenate([q_ref[:, h * HEAD_DIM:(h + 1) * HEAD_DIM] for h in heads], axis=0)
        ss.append(_dot(qg, kt_ref[g * HEAD_DIM:(g + 1) * HEAD_DIM, :]))
    outs = []
    for (g, heads), s in zip(chains, ss):
        p = jnp.exp2(s - jnp.max(s, axis=-1, keepdims=True)).astype(BF16)
        o = _dot(p, v_ref[g])
        for j in range(hpc):
            oh = o[j * tq:(j + 1) * tq, :]
            outs.append(oh[:, 0:HEAD_DIM] / oh[:, HEAD_DIM:HEAD_DIM + 1])
    o_ref[...] = jnp.concatenate(outs, axis=1).astype(o_ref.dtype)


def _attention(q, kt, vaug, n, tq):
    m = q.shape[0]
    b, _, s = kt.shape
    tpb = n // tq
    return pl.pallas_call(
        _attn_kernel,
        grid=(b, tpb),
        in_specs=[pl.BlockSpec((tq, ATT_Q_W), lambda bi, i: (bi * tpb + i, 0)),
                  pl.BlockSpec((None, ATT_KV_W, s), lambda bi, i: (bi, 0, 0)),
                  pl.BlockSpec((None, N_KV_HEADS, s, LANES), lambda bi, i: (bi, 0, 0, 0))],
        out_specs=pl.BlockSpec((tq, ATT_Q_W), lambda bi, i: (bi * tpb + i, 0)),
        out_shape=jax.ShapeDtypeStruct((m, ATT_Q_W), BF16),
        compiler_params=_cparams(("arbitrary", "arbitrary")), name="attention",
    )(q, kt, vaug)


def _hyena_filter_kernel(feat_ref, w1_ref, b1_ref, w2_ref, b2_ref, w3_ref, freq_ref, delta_ref,
                         cn_ref, sn_ref, hr_ref, hs_ref, hn_ref):
    n = feat_ref.shape[0]
    hp = lax.Precision.HIGHEST
    freq = freq_ref[...]
    h = jnp.sin(freq * (jnp.dot(feat_ref[...], w1_ref[...], preferred_element_type=F32, precision=hp)
                        + b1_ref[...]))
    h = jnp.sin(freq * (jnp.dot(h, w2_ref[...], preferred_element_type=F32, precision=hp) + b2_ref[...]))
    h = jnp.dot(h, w3_ref[...], preferred_element_type=F32, precision=hp)
    w2o = HY_ORDER * HY_W
    pos = lax.broadcasted_iota(jnp.int32, (n, w2o), 0)
    t = pos.astype(F32) / (n - 1)
    decay = jnp.exp(-t * delta_ref[...])
    hf = h[:, 0:w2o] * decay
    hb = jnp.where(pos == 0, 0.0, h[:, w2o:2 * w2o] * decay)
    inv = 1.0 / (jnp.sum(jnp.abs(hf), axis=0, keepdims=True) + jnp.sum(jnp.abs(hb), axis=0, keepdims=True))
    hsum = (hf + hb) * inv
    hdif = (hf - hb) * inv
    sign = jnp.where(pos % 2 == 0, 1.0, -1.0)
    wk = jnp.where(pos == 0, 1.0, 2.0) * (1.0 / (2 * n))
    s_hi, s_lo = _split_bf16(hsum)
    d_hi, d_lo = _split_bf16(hdif)
    hr_ref[...] = (_dot(cn_ref[...], s_hi) + _dot(cn_ref[...], s_lo)) * wk
    hs_ref[...] = (_dot(sn_ref[...], d_hi) + _dot(sn_ref[...], d_lo)) * wk
    hn_ref[...] = jnp.sum(hsum * sign, axis=0, keepdims=True) * (1.0 / (2 * n))


def _hyena_filter(feats, w1, b1, w2, b2, w3, freq, deltas, cn, sn):
    n = feats.shape[0]
    w2o = HY_ORDER * HY_W
    args = (feats, w1, b1, w2, b2, w3, freq, deltas, cn, sn)
    return pl.pallas_call(
        _hyena_filter_kernel,
        grid=(1,),
        in_specs=[pl.BlockSpec(a.shape, lambda i: (0, 0)) for a in args],
        out_specs=[pl.BlockSpec((n, w2o), lambda i: (0, 0)), pl.BlockSpec((n, w2o), lambda i: (0, 0)),
                   pl.BlockSpec((1, w2o), lambda i: (0, 0))],
        out_shape=[jax.ShapeDtypeStruct((n, w2o), F32), jax.ShapeDtypeStruct((n, w2o), F32),
                   jax.ShapeDtypeStruct((1, w2o), F32)],
        compiler_params=_cparams(("arbitrary",)), name="hyena_filter",
    )(*args)


def _hyena_kernel(u_ref, sw_ref, sb_ref, hr_ref, hs_ref, hn_ref, d_ref, cn_ref, sn_ref, o_ref,
                  zf_ref, zb_ref, g1_ref, g2_ref, yr_ref, ys_ref):
    n = u_ref.shape[0]
    ck = min(n, HY_CHUNK)
    chunks = [slice(r, r + ck) for r in range(0, n, ck)]
    pos = lax.broadcasted_iota(jnp.int32, (n, HY_W), 0)
    sign = jnp.where(pos % 2 == 0, 1.0, -1.0)

    def short(j):
        u = u_ref[:, j * HY_W:(j + 1) * HY_W].astype(F32)
        up = jnp.where(pos == 0, 0.0, pltpu.roll(u, 1, 0))
        un = jnp.where(pos == n - 1, 0.0, pltpu.roll(u, n - 1, 0))
        c = slice(j * HY_W, (j + 1) * HY_W)
        return sb_ref[0:1, c] + up * sw_ref[0:1, c] + u * sw_ref[1:2, c] + un * sw_ref[2:3, c]

    v = short(0)
    zf_ref[...] = v
    zb_ref[...] = v.astype(BF16)
    g1_ref[...] = short(1)
    g2_ref[...] = short(2)
    gates = (g1_ref, g2_ref)
    csign = jnp.where(lax.broadcasted_iota(jnp.int32, (ck, HY_W), 0) % 2 == 0, 1.0, -1.0)

    for o in range(HY_ORDER):
        c = slice(o * HY_W, (o + 1) * HY_W)
        nyq = jnp.sum(zf_ref[...] * sign, axis=0, keepdims=True) * hn_ref[0:1, c]
        for r in chunks:
            xr = _dot(cn_ref[r, :], zb_ref[...])
            xs = _dot(sn_ref[r, :], zb_ref[...])
            hr = hr_ref[r, c]
            hs = hs_ref[r, c]
            yr_ref[r, :] = (xr * hr - xs * hs).astype(BF16)
            ys_ref[r, :] = (xr * hs + xs * hr).astype(BF16)
        for r in chunks:
            y = _dot(cn_ref[r, :], yr_ref[...]) + _dot(sn_ref[r, :], ys_ref[...])
            y = (y + csign * nyq + zf_ref[r, :] * d_ref[o:o + 1, :]) * gates[o][r, :]
            if o + 1 < HY_ORDER:
                zf_ref[r, :] = y
                zb_ref[r, :] = y.astype(BF16)
            else:
                o_ref[r, :] = y.astype(o_ref.dtype)


def _hyena(hy, sw, sb, hr, hs, hn, dbias, cn, sn, n):
    m = hy.shape[0]
    const = lambda b: (0, 0)
    once = lambda a: pl.BlockSpec(a.shape, const, pipeline_mode=pl.Buffered(1))
    return pl.pallas_call(
        _hyena_kernel,
        grid=(m // n,),
        in_specs=[pl.BlockSpec((n, hy.shape[1]), lambda b: (b, 0)),
                  pl.BlockSpec(sw.shape, const), pl.BlockSpec(sb.shape, const),
                  once(hr), once(hs), pl.BlockSpec(hn.shape, const),
                  pl.BlockSpec(dbias.shape, const), once(cn), once(sn)],
        out_specs=pl.BlockSpec((n, HY_W), lambda b: (b, 0)),
        out_shape=jax.ShapeDtypeStruct((m, HY_W), BF16),
        scratch_shapes=[pltpu.VMEM((n, HY_W), F32), pltpu.VMEM((n, HY_W), BF16),
                        pltpu.VMEM((n, HY_W), F32), pltpu.VMEM((n, HY_W), F32),
                        pltpu.VMEM((n, HY_W), BF16), pltpu.VMEM((n, HY_W), BF16)],
        compiler_params=_cparams(("arbitrary",)), name="hyena",
    )(hy, sw, sb, hr, hs, hn, dbias, cn, sn)


def _fnet_kernel(u_ref, bc_ref, bs_ref, cf_ref, sf_ref, o_ref):
    n = u_ref.shape[0]
    u = u_ref[...]
    uc = _dot(u, bc_ref[...]).astype(BF16)
    us = _dot(u, bs_ref[...]).astype(BF16)
    y = _dot(cf_ref[...], uc) - _dot(sf_ref[...], us)
    o_ref[...] = (y * (1.0 / math.sqrt(n * FN_GROUP_W))).astype(o_ref.dtype)


def _fnet(fn, bc, bs, cf, sf, n):
    m = fn.shape[0]
    const = lambda b: (0, 0)
    big = lambda a: pl.BlockSpec(a.shape, const, pipeline_mode=pl.Buffered(1))
    return pl.pallas_call(
        _fnet_kernel,
        grid=(m // n,),
        in_specs=[pl.BlockSpec((n, FN_W), lambda b: (b, 0)), pl.BlockSpec(bc.shape, const),
                  pl.BlockSpec(bs.shape, const), big(cf), big(sf)],
        out_specs=pl.BlockSpec((n, FN_W), lambda b: (b, 0)),
        out_shape=jax.ShapeDtypeStruct((m, FN_W), BF16),
        compiler_params=_cparams(("arbitrary",)), name="fnet",
    )(fn, bc, bs, cf, sf)


def _merge_kernel(x_ref, mod_ref, g1_ref, g2_ref, att_ref, hy_ref, fn_ref, wg_ref, wb_ref, wo_ref,
                  wrh_ref, wrl_ref, x1_ref, h2_ref, lg_ref):
    tm, d = x_ref.shape
    branches = ((att_ref, 0, ATT_Q_W), (hy_ref, ATT_Q_W, ATT_Q_W + HY_W), (fn_ref, ATT_Q_W + HY_W, MIX_W))
    rows = [slice(r, r + MERGE_CHUNK) for r in range(0, tm, MERGE_CHUNK)]
    xs = [x_ref[r, :] for r in rows]
    hbs = [_rms_mod(x, g1_ref[...], mod_ref[0:1, :], mod_ref[1:2, :]).astype(BF16) for x in xs]
    mixes = []
    for r, hb in zip(rows, hbs):
        mix = None
        for i, (b_ref, lo, hi) in enumerate(branches):
            gate = _dot(hb, wg_ref[:, OFF_GATE + i * d:OFF_GATE + (i + 1) * d])
            gate = 1.0 / (1.0 + jnp.exp(-gate))
            term = gate * _dot(b_ref[r, :], wb_ref[lo:hi, :])
            mix = term if mix is None else mix + term
        mixes.append(mix.astype(BF16))
    for r, x, mix in zip(rows, xs, mixes):
        y = _dot(mix, wo_ref[...])
        x1 = x + mod_ref[2:3, :] * y
        x1_ref[r, :] = x1
        h2 = _rms_mod(x1, g2_ref[...], mod_ref[3:4, :], mod_ref[4:5, :])
        hi, lo = _split_bf16(h2)
        h2_ref[r, :] = hi
        lg_ref[r, :] = _dot(hi, wrh_ref[...]) + _dot(lo, wrh_ref[...]) + _dot(hi, wrl_ref[...])


def _merge(x, mod, g1, g2, att, hyo, fno, wg, layer, wb, wo, wrh, wrl, n, tm):
    m, d = x.shape
    per_b = mod.shape[0] > 1
    tpb = n // tm
    row = lambda i: (i, 0)
    const = lambda i: (0, 0)
    mod_map = (lambda i: (i // tpb, 0, 0)) if per_b else (lambda i: (0, 0, 0))
    return pl.pallas_call(
        _merge_kernel,
        grid=(m // tm,),
        in_specs=[pl.BlockSpec((tm, d), row), pl.BlockSpec((None, 8, d), mod_map),
                  pl.BlockSpec((1, d), const), pl.BlockSpec((1, d), const),
                  pl.BlockSpec((tm, ATT_Q_W), row), pl.BlockSpec((tm, HY_W), row), pl.BlockSpec((tm, FN_W), row),
                  pl.BlockSpec((None,) + wg.shape[1:], lambda i: (layer, 0, 0), pipeline_mode=pl.Buffered(1)),
                  pl.BlockSpec(wb.shape, const), pl.BlockSpec(wo.shape, const),
                  pl.BlockSpec(wrh.shape, const), pl.BlockSpec(wrl.shape, const)],
        out_specs=[pl.BlockSpec((tm, d), row), pl.BlockSpec((tm, d), row), pl.BlockSpec((tm, LANES), row)],
        out_shape=[jax.ShapeDtypeStruct((m, d), F32), jax.ShapeDtypeStruct((m, d), BF16),
                   jax.ShapeDtypeStruct((m, LANES), F32)],
        compiler_params=_cparams(("arbitrary",)), name="merge",
    )(x, mod, g1, g2, att, hyo, fno, wg, wb, wo, wrh, wrl)


def _route_kernel(lg_ref, tri_ref, rank_ref, aff_ref, rankt_ref, *, cap):
    n = lg_ref.shape[0]
    lt = lg_ref[...].T[0:N_EXPERTS, :]
    e = jnp.exp(lt - jnp.max(lt, axis=0, keepdims=True))
    aff = e / jnp.sum(e, axis=0, keepdims=True)
    bits = pltpu.bitcast(aff, jnp.int32)

    def step(i, thr):
        cand = thr | jnp.left_shift(jnp.int32(1), 30 - i)
        cnt = jnp.sum(jnp.where(bits >= cand, 1.0, 0.0), axis=1, keepdims=True)
        return jnp.where(cnt >= cap, cand, thr)

    thr = lax.fori_loop(0, 31, step, jnp.zeros((N_EXPERTS, 1), jnp.int32))
    gt = bits > thr
    eq = bits == thr
    need = cap - jnp.sum(jnp.where(gt, 1.0, 0.0), axis=1, keepdims=True)

    def excl_cumsum(mask):
        parts = []
        off = jnp.zeros((N_EXPERTS, 1), F32)
        for c in range(n // LANES):
            blk = mask[:, c * LANES:(c + 1) * LANES]
            parts.append(_dot(blk.astype(BF16), tri_ref[...]) + off)
            off = off + jnp.sum(blk, axis=1, keepdims=True)
        return jnp.concatenate(parts, axis=1)

    tie_rank = excl_cumsum(jnp.where(eq, 1.0, 0.0))
    sel = jnp.where(gt, 1.0, jnp.where(eq, jnp.where(tie_rank < need, 1.0, 0.0), 0.0))
    rank = jnp.where(sel > 0.5, excl_cumsum(sel), -1.0)
    rank_ref[...] = rank.astype(jnp.int32)
    aff_ref[...] = aff
    pad = jnp.full((LANES - N_EXPERTS, n), -1.0, F32)
    rankt_ref[...] = jnp.concatenate([rank, pad], axis=0).T.astype(BF16)


def _route(logits, tri, n, cap):
    m = logits.shape[0]
    b = m // n
    assert cap <= 256
    return pl.pallas_call(
        functools.partial(_route_kernel, cap=cap),
        grid=(b,),
        in_specs=[pl.BlockSpec((n, LANES), lambda i: (i, 0)), pl.BlockSpec(tri.shape, lambda i: (0, 0))],
        out_specs=[pl.BlockSpec((None, N_EXPERTS, n), lambda i: (i, 0, 0)),
                   pl.BlockSpec((None, N_EXPERTS, n), lambda i: (i, 0, 0)),
                   pl.BlockSpec((n, LANES), lambda i: (i, 0))],
        out_shape=[jax.ShapeDtypeStruct((b, N_EXPERTS, n), jnp.int32),
                   jax.ShapeDtypeStruct((b, N_EXPERTS, n), F32),
                   jax.ShapeDtypeStruct((m, LANES), BF16)],
        compiler_params=_cparams(("arbitrary",)), name="route",
    )(logits, tri)


def _gather_kernel(h_ref, rank_ref, aff_ref, xg_ref, w_ref, sel_ref, *, cap):
    n, d = h_ref.shape
    slot = lax.broadcasted_iota(jnp.int32, (cap, n), 0)
    for e in range(N_EXPERTS):
        hit = slot == rank_ref[e:e + 1, :]
        sel_ref[e * cap:(e + 1) * cap, :] = jnp.where(hit, 1.0, 0.0).astype(BF16)
        wcol = jnp.sum(jnp.where(hit, aff_ref[e:e + 1, :], 0.0), axis=1, keepdims=True)
        w_ref[e] = jnp.broadcast_to(wcol, (cap, LANES))
    tn = 256
    for c in range(d // tn):
        xg = _dot(sel_ref[...], h_ref[:, c * tn:(c + 1) * tn]).astype(BF16)
        for e in range(N_EXPERTS):
            xg_ref[e, :, c * tn:(c + 1) * tn] = xg[e * cap:(e + 1) * cap, :]


def _gather(h2, rank, aff, n, cap):
    m, d = h2.shape
    b = m // n
    return pl.pallas_call(
        functools.partial(_gather_kernel, cap=cap),
        grid=(b,),
        in_specs=[pl.BlockSpec((n, d), lambda i: (i, 0)),
                  pl.BlockSpec((None, N_EXPERTS, n), lambda i: (i, 0, 0)),
                  pl.BlockSpec((None, N_EXPERTS, n), lambda i: (i, 0, 0))],
        out_specs=[pl.BlockSpec((N_EXPERTS, None, cap, d), lambda i: (0, i, 0, 0)),
                   pl.BlockSpec((N_EXPERTS, None, cap, LANES), lambda i: (0, i, 0, 0))],
        out_shape=[jax.ShapeDtypeStruct((N_EXPERTS, b, cap, d), BF16),
                   jax.ShapeDtypeStruct((N_EXPERTS, b, cap, LANES), F32)],
        scratch_shapes=[pltpu.VMEM((N_EXPERTS * cap, n), BF16)],
        compiler_params=_cparams(("arbitrary",)), name="moe_gather",
    )(h2, rank, aff)


def _ffn_kernel(x_ref, w_ref, wg_ref, wu_ref, wd_ref, o_ref, acc_ref):
    f = pl.program_id(2)
    x = x_ref[...]
    d = acc_ref.shape[1]
    tf = 512
    y = None
    for c in range(wg_ref.shape[1] // tf):
        cols = slice(c * tf, (c + 1) * tf)
        a = _dot(x, wg_ref[:, cols].astype(BF16))
        u = _dot(x, wu_ref[:, cols].astype(BF16))
        hm = (a * (1.0 / (1.0 + jnp.exp(-a))) * u).astype(BF16)
        t = _dot(hm, wd_ref[cols, :].astype(BF16))
        y = t if y is None else y + t

    @pl.when(f == 0)
    def _():
        acc_ref[...] = y

    @pl.when(f > 0)
    def _():
        acc_ref[...] += y

    @pl.when(f == pl.num_programs(2) - 1)
    def _():
        wt = jnp.concatenate([w_ref[...]] * (d // LANES), axis=1)
        o_ref[...] = (acc_ref[...] * wt).astype(o_ref.dtype)


def _ffn(xg, w, wg, wu, wd, tm, tf):
    ne, rows, d = xg.shape
    ff = wg.shape[2]
    return pl.pallas_call(
        _ffn_kernel,
        grid=(ne, rows // tm, ff // tf),
        in_specs=[pl.BlockSpec((None, tm, d), lambda e, i, f: (e, i, 0)),
                  pl.BlockSpec((None, tm, LANES), lambda e, i, f: (e, i, 0)),
                  pl.BlockSpec((None, d, tf), lambda e, i, f: (e, 0, f)),
                  pl.BlockSpec((None, d, tf), lambda e, i, f: (e, 0, f)),
                  pl.BlockSpec((None, tf, d), lambda e, i, f: (e, f, 0))],
        out_specs=pl.BlockSpec((None, tm, d), lambda e, i, f: (e, i, 0)),
        out_shape=jax.ShapeDtypeStruct((ne, rows, d), BF16),
        scratch_shapes=[pltpu.VMEM((tm, d), F32)],
        compiler_params=_cparams(("arbitrary", "arbitrary", "arbitrary")), name="moe_ffn",
    )(xg, w, wg, wu, wd)


def _scatter_kernel(rankt_ref, y_ref, expand_ref, slot_ref, o_ref, selt_ref, *, cap):
    n = rankt_ref.shape[0]
    d = o_ref.shape[1]
    width = N_EXPERTS * cap
    if cap % LANES == 0:
        rank = rankt_ref[...].astype(F32)
        slot = lax.broadcasted_iota(jnp.int32, (n, cap), 1).astype(F32)
        for e in range(N_EXPERTS):
            r = jnp.broadcast_to(rank[:, e:e + 1], (n, cap))
            selt_ref[:, e * cap:(e + 1) * cap] = jnp.where(r == slot, 1.0, 0.0).astype(BF16)
    else:
        r = _dot(rankt_ref[...], expand_ref[...])
        selt_ref[...] = jnp.where(r == slot_ref[...], 1.0, 0.0).astype(BF16)
    y = y_ref[...].reshape(width, d)
    tn = 256
    for c in range(d // tn):
        o_ref[:, c * tn:(c + 1) * tn] = _dot(selt_ref[...], y[:, c * tn:(c + 1) * tn])


def _scatter(rankt, yw, expand, slotpat, n, cap):
    ne, b, _, d = yw.shape
    return pl.pallas_call(
        functools.partial(_scatter_kernel, cap=cap),
        grid=(b,),
        in_specs=[pl.BlockSpec((n, LANES), lambda i: (i, 0)),
                  pl.BlockSpec((ne, None, cap, d), lambda i: (0, i, 0, 0), pipeline_mode=pl.Buffered(1)),
                  pl.BlockSpec(expand.shape, lambda i: (0, 0)),
                  pl.BlockSpec(slotpat.shape, lambda i: (0, 0))],
        out_specs=pl.BlockSpec((n, d), lambda i: (i, 0)),
        out_shape=jax.ShapeDtypeStruct((b * n, d), F32),
        scratch_shapes=[pltpu.VMEM((n, ne * cap), BF16)],
        compiler_params=_cparams(("arbitrary",)), name="moe_scatter",
    )(rankt, yw, expand, slotpat)


def _moe(h2, rank, aff, rankt, wg, wu, wd, tabs, n, cap):
    b = h2.shape[0] // n
    d = h2.shape[1]
    xg, w = _gather(h2, rank, aff, n, cap)
    rows = b * cap
    yw = _ffn(xg.reshape(N_EXPERTS, rows, d), w.reshape(N_EXPERTS, rows, LANES), wg, wu, wd,
              min(rows, FFN_ROW_TILE), FFN_HIDDEN_TILE)
    return _scatter(rankt, yw.reshape(N_EXPERTS, b, cap, d), tabs["expand"], tabs["slotpat"], n, cap)


def _combine_kernel(x_ref, y_ref, mod_ref, g_ref, o_ref, *, norm):
    x = x_ref[...] + mod_ref[5:6, :] * y_ref[...]
    if norm:
        x = x * lax.rsqrt(jnp.mean(x * x, axis=-1, keepdims=True) + EPS) * g_ref[...]
    o_ref[...] = x


def _combine(x, y, mod, g, n, tm, norm):
    m, d = x.shape
    per_b = mod.shape[0] > 1
    tpb = n // tm
    row = lambda i: (i, 0)
    mod_map = (lambda i: (i // tpb, 0, 0)) if per_b else (lambda i: (0, 0, 0))
    return pl.pallas_call(
        functools.partial(_combine_kernel, norm=norm),
        grid=(m // tm,),
        in_specs=[pl.BlockSpec((tm, d), row), pl.BlockSpec((tm, d), row),
                  pl.BlockSpec((None, 8, d), mod_map), pl.BlockSpec((1, d), lambda i: (0, 0))],
        out_specs=pl.BlockSpec((tm, d), row),
        out_shape=jax.ShapeDtypeStruct((m, d), F32),
        compiler_params=_cparams(("arbitrary",)), name="combine",
    )(x, y, mod, g)


def _dft_tables(n, period):
    j = jnp.arange(n, dtype=jnp.int32)
    jk = (j[:, None] * j[None, :]) % period
    ang = jk.astype(F32) * (2.0 * math.pi / period)
    return jnp.cos(ang).astype(BF16), jnp.sin(ang).astype(BF16)


def _rope_tables(n):
    rows = n // GRID_W
    row = jnp.repeat(jnp.arange(rows, dtype=F32), GRID_W)
    col = jnp.tile(jnp.arange(GRID_W, dtype=F32), rows)
    inv = ROPE_THETA ** (-jnp.arange(0, ROPE_AXIS_DIM, 2, dtype=F32) / ROPE_AXIS_DIM)
    ang = jnp.concatenate([row[:, None] * inv, col[:, None] * inv], axis=-1)
    cos = jnp.repeat(jnp.cos(ang), 2, axis=1)
    sin = jnp.repeat(jnp.sin(ang), 2, axis=1) * jnp.tile(jnp.array([-1.0, 1.0], F32), HEAD_DIM // 2)
    return jnp.tile(cos, (1, LANES // HEAD_DIM)), jnp.tile(sin, (1, LANES // HEAD_DIM))


def _hyena_feats(n):
    pos = jnp.arange(n, dtype=F32)
    t = pos / (n - 1)
    bands = jnp.linspace(1e-4, HY_BANDS - 1, HY_BANDS, dtype=F32)
    ang = (2.0 * math.pi / n) * pos[:, None] * bands[None, :]
    feats = jnp.concatenate([t[:, None], jnp.cos(ang), -jnp.sin(ang)], axis=-1)
    return jnp.pad(feats, ((0, 0), (0, LANES - HY_EMB)))


def _pair_swap(a):
    s = a.shape
    return a.reshape(*s[:-1], s[-1] // 2, 2)[..., ::-1].reshape(s)


def _side(x, xb, n, mod, l, P, tabs, kv_ext, rope, tm, last_ctx):
    b = x.shape[0] // n
    outs = _inproj(x, xb, mod, P["norm1_g"][l], P["w_in"], l, P["w_sw"][l] if rope else None,
                   tabs["cos"] if rope else None, tabs["sin"] if rope else None,
                   P["qg"][l], P["kg"][l], P["gmat"], n, tm)
    if xb is not None:
        x, outs = outs[0], outs[1:]
    q, k, v, hy, fn = outs
    k3, v3 = k.reshape(b, n, ATT_KV_W), v.reshape(b, n, ATT_KV_W)
    if last_ctx:
        return None, None, k3, v3
    if kv_ext is not None:
        kc, vc = jnp.concatenate([k3, kv_ext[0]], axis=1), jnp.concatenate([v3, kv_ext[1]], axis=1)
    else:
        kc, vc = k3, v3
    s = kc.shape[1]
    fill = jnp.concatenate([jnp.ones((b, s, 1), BF16), jnp.zeros((b, s, LANES - HEAD_DIM - 1), BF16)], axis=-1)
    vaug = jnp.stack([jnp.concatenate([vc[..., g * HEAD_DIM:(g + 1) * HEAD_DIM], fill], axis=-1)
                      for g in range(N_KV_HEADS)], axis=1)
    att = _attention(q, jnp.swapaxes(kc, 1, 2), vaug, n, min(n, 256))
    hr, hs, hn = _hyena_filter(tabs["feats"], P["hy_w1"][l], P["hy_b1"][l], P["hy_w2"][l], P["hy_b2"][l],
                               P["hy_w3"][l], P["hy_freq"][l], tabs["deltas"], tabs["cn"], tabs["sn"])
    hyo = _hyena(hy, P["hy_sw"][l], P["hy_sb"][l], hr, hs, hn, P["hy_bias"][l], tabs["cn"], tabs["sn"], n)
    fno = _fnet(fn, tabs["bc"], tabs["bs"], tabs["cf"], tabs["sf"], n)
    x1, h2, logits = _merge(x, mod, P["norm1_g"][l], P["norm2_g"][l], att, hyo, fno, P["w_in"], l,
                            P["w_branch"][l], P["w_out"][l], P["wr_hi"][l], P["wr_lo"][l], n, tm)
    cap = CAPACITY_FACTOR * n // N_EXPERTS
    rank, aff, rankt = _route(logits, tabs["tri"], n, cap)
    y = _moe(h2, rank, aff, rankt, P["w_gate"][l], P["w_up"][l], P["w_down"][l], tabs, n, cap)
    return x1, y, k3, v3


def _tables(n, rope):
    cn, sn = _dft_tables(n, 2 * n)
    cf, sf = _dft_tables(n, n)
    a = jnp.arange(FN_W, dtype=jnp.int32)
    same = (a[:, None] // FN_GROUP_W) == (a[None, :] // FN_GROUP_W)
    ang = ((a[:, None] * a[None, :]) % FN_GROUP_W).astype(F32) * (2.0 * math.pi / FN_GROUP_W)
    deltas = jnp.abs(jnp.linspace(math.log(HY_DECAY_TARGET) / HY_DECAY_LONG_PCT,
                                  math.log(HY_DECAY_TARGET) / HY_DECAY_SHORT_PCT, HY_W, dtype=F32))
    i = jnp.arange(LANES, dtype=jnp.int32)
    cap = CAPACITY_FACTOR * n // N_EXPERTS
    j = jnp.arange(N_EXPERTS * cap, dtype=jnp.int32)
    tabs = dict(expand=(i[:, None] == j[None, :] // cap).astype(BF16),
                slotpat=(j % cap).astype(F32)[None, :],
                cn=cn, sn=sn, cf=cf, sf=sf,
                bc=jnp.where(same, jnp.cos(ang), 0.0).astype(BF16),
                bs=jnp.where(same, jnp.sin(ang), 0.0).astype(BF16),
                feats=_hyena_feats(n), deltas=jnp.tile(deltas, HY_ORDER)[None, :],
                tri=(i[:, None] < i[None, :]).astype(BF16))
    if rope:
        tabs["cos"], tabs["sin"] = _rope_tables(n)
    return tabs


def kernel(x, c, ctx, c_ctx, w_mod, b_mod, norm1_g, norm2_g, w_in, q_gain, k_gain, hy_short_w, hy_short_b,
           hy_f_w1, hy_f_b1, hy_f_w2, hy_f_b2, hy_f_w3, hy_f_freq, hy_bias, w_branch, w_out, w_router,
           w_gate, w_up, w_down, final_g):
    bsz, n_lat, d = x.shape
    n_ctx = ctx.shape[1]
    depth = w_mod.shape[0]
    assert d == D_MODEL and n_lat % LANES == 0 and n_ctx % LANES == 0

    wq, wk = w_in[:, :, OFF_Q:OFF_K], w_in[:, :, OFF_K:OFF_V]
    hid = jnp.arange(ATT_Q_W, dtype=jnp.int32) // HEAD_DIM
    P = dict(
        norm1_g=norm1_g[:, None, :], norm2_g=norm2_g[:, None, :],
        w_in=w_in.astype(BF16),
        w_sw=jnp.concatenate([_pair_swap(wq), _pair_swap(wk)], axis=-1).astype(BF16),
        qg=jnp.stack([jnp.tile(q_gain, (1, N_Q_HEADS)), jnp.tile(_pair_swap(q_gain), (1, N_Q_HEADS))], axis=1),
        kg=jnp.stack([jnp.tile(k_gain, (1, N_KV_HEADS)), jnp.tile(_pair_swap(k_gain), (1, N_KV_HEADS))], axis=1),
        gmat=(hid[:, None] == hid[None, :]).astype(BF16),
        hy_sw=hy_short_w, hy_sb=hy_short_b[:, None, :],
        hy_w1=jnp.pad(hy_f_w1, ((0, 0), (0, LANES - HY_EMB), (0, 0))), hy_b1=hy_f_b1[:, None, :],
        hy_w2=hy_f_w2, hy_b2=hy_f_b2[:, None, :], hy_w3=hy_f_w3, hy_freq=hy_f_freq[:, None, :],
        hy_bias=hy_bias,
        w_branch=w_branch.astype(BF16), w_out=w_out.astype(BF16),
        w_gate=w_gate, w_up=w_up, w_down=w_down,
    )
    wr = jnp.pad(w_router, ((0, 0), (0, 0), (0, LANES - N_EXPERTS)))
    P["wr_hi"] = wr.astype(BF16)
    P["wr_lo"] = (wr - P["wr_hi"].astype(F32)).astype(BF16)

    tab_x = _tables(n_lat, True)
    tab_c = _tables(n_ctx, False)

    rows = -(-(bsz + 1) // 8) * 8
    c_all = jnp.concatenate([c, c_ctx[None, :], jnp.zeros((rows - bsz - 1, d), F32)], axis=0)
    mod = _modulation(c_all, w_mod, b_mod).reshape(depth, rows, 6, d)
    mod = jnp.pad(mod, ((0, 0), (0, 0), (0, 2), (0, 0)))

    tm_x, tm_c = min(n_lat, 512), min(n_ctx, 512)
    xs, xpend = x.reshape(bsz * n_lat, d), None
    cs, cpend = ctx.reshape(bsz * n_ctx, d), None
    for l in range(depth):
        last = l == depth - 1
        mod_x, mod_c = mod[l, :bsz], mod[l, bsz:bsz + 1]
        pm_x = None if l == 0 else jnp.concatenate([mod_x[:, :5], mod[l - 1, :bsz, 5:6], mod_x[:, 6:]], axis=1)
        pm_c = None if l == 0 else jnp.concatenate([mod_c[:, :5], mod[l - 1, bsz:bsz + 1, 5:6], mod_c[:, 6:]],
                                                   axis=1)
        c1, cy, kc, vc = _side(cs, cpend, n_ctx, mod_c if l == 0 else pm_c, l, P, tab_c, None, False, tm_c, last)
        x1, xy, _, _ = _side(xs, xpend, n_lat, mod_x if l == 0 else pm_x, l, P, tab_x, (kc, vc), True, tm_x, False)
        xs, xpend = x1, xy
        if not last:
            cs, cpend = c1, cy
    out = _combine(xs, xpend, mod[depth - 1, :bsz], final_g[None, :], n_lat, tm_x, True)
    return out.reshape(bsz, n_lat, d)
```

```python
import functools
import math

import jax
import jax.numpy as jnp
from jax import lax
from jax.experimental import pallas as pl
from jax.experimental.pallas import tpu as pltpu

F32 = jnp.float32
BF16 = jnp.bfloat16

D_MODEL = 1024
GRID_W = 64
HEAD_DIM = 64
N_Q_HEADS = 8
N_KV_HEADS = 2
Q_GROUP = N_Q_HEADS // N_KV_HEADS
ATT_Q_W = N_Q_HEADS * HEAD_DIM
ATT_KV_W = N_KV_HEADS * HEAD_DIM
ROPE_THETA = 10000.0
ROPE_AXIS_DIM = HEAD_DIM // 2
HY_W = D_MODEL // 4
HY_ORDER = 2
HY_SHORT = 3
HY_BANDS = 16
HY_EMB = 2 * HY_BANDS + 1
HY_FFN = 64
HY_DECAY_TARGET = 1e-2
HY_DECAY_SHORT_PCT = 0.3
HY_DECAY_LONG_PCT = 1.5
FN_GROUPS = 4
FN_GROUP_W = D_MODEL // 16
FN_W = FN_GROUPS * FN_GROUP_W
MIX_W = ATT_Q_W + HY_W + FN_W
N_BRANCH = 3
OFF_Q = 0
OFF_K = OFF_Q + ATT_Q_W
OFF_V = OFF_K + ATT_KV_W
OFF_HY = OFF_V + ATT_KV_W
OFF_FN = OFF_HY + (HY_ORDER + 1) * HY_W
OFF_GATE = OFF_FN + FN_W
N_EXPERTS = 16
CAPACITY_FACTOR = 2
EPS = 1e-6

LANES = 128
VMEM_LIMIT = 56 * 1024 * 1024
FFN_ROW_TILE = 1024
FFN_HIDDEN_TILE = 1024
HY_CHUNK = 512
INPROJ_CHUNK = 256
MERGE_CHUNK = 256
ATTN_HEADS_PER_CHAIN = 4

A_Q, A_K, A_V, A_HY, A_FN, A_END = 0, 512, 640, 768, 1536, 1792


def _cparams(sem):
    return pltpu.CompilerParams(dimension_semantics=sem, vmem_limit_bytes=VMEM_LIMIT)


def _dot(a, b):
    return jnp.dot(a, b, preferred_element_type=F32)


def _split_bf16(x):
    hi = x.astype(BF16)
    lo = (x - hi.astype(F32)).astype(BF16)
    return hi, lo


def _layer_spec(arr, layer, **kw):
    zeros = (0,) * (arr.ndim - 1)
    return pl.BlockSpec((None,) + arr.shape[1:], lambda *_: (layer,) + zeros, **kw)


def _rms_mod(x, g, shift, scale):
    y = x * lax.rsqrt(jnp.mean(x * x, axis=-1, keepdims=True) + EPS)
    return (y * g) * (1.0 + scale) + shift


def _mod_kernel(c_ref, w_ref, b_ref, o_ref):
    c = c_ref[...]
    sc = c * (1.0 / (1.0 + jnp.exp(-c)))
    o_ref[...] = jnp.dot(sc, w_ref[...], preferred_element_type=F32,
                         precision=lax.Precision.HIGHEST) + b_ref[...]


def _modulation(c_all, w_mod, b_mod):
    depth, d, n6 = w_mod.shape
    rows = c_all.shape[0]
    tn = 1536
    return pl.pallas_call(
        _mod_kernel,
        grid=(depth, n6 // tn),
        in_specs=[pl.BlockSpec((rows, d), lambda l, j: (0, 0)),
                  pl.BlockSpec((None, d, tn), lambda l, j: (l, 0, j)),
                  pl.BlockSpec((None, 1, tn), lambda l, j: (l, 0, j))],
        out_specs=pl.BlockSpec((None, rows, tn), lambda l, j: (l, 0, j)),
        out_shape=jax.ShapeDtypeStruct((depth, rows, n6), F32),
        compiler_params=_cparams(("arbitrary", "arbitrary")),
        name="modulation",
    )(c_all, w_mod, b_mod.reshape(depth, 1, n6))


def _head_norm(u, gmat_ref, width):
    hi, lo = _split_bf16(u * u)
    gm = gmat_ref[0:width, 0:width]
    ms = (_dot(hi, gm) + _dot(lo, gm)) * (1.0 / HEAD_DIM)
    return lax.rsqrt(ms + EPS)


def _inproj_kernel(*refs, rope, combine):
    it = iter(refs)
    x_ref = next(it)
    if combine:
        xb_ref = next(it)
    mod_ref = next(it)
    g_ref = next(it)
    w_ref = next(it)
    if rope:
        wsw_ref = next(it)
        cos_ref = next(it)
        sin_ref = next(it)
    qg_ref = next(it)
    kg_ref = next(it)
    gmat_ref = next(it)
    if combine:
        xo_ref = next(it)
    q_ref = next(it)
    k_ref = next(it)
    v_ref = next(it)
    hy_ref = next(it)
    fn_ref = next(it)

    tm = x_ref.shape[0]
    ck = min(tm, INPROJ_CHUNK)
    chunks = [slice(r0, r0 + ck) for r0 in range(0, tm, ck)]
    hbs = []
    for rows in chunks:
        x = x_ref[rows, :]
        if combine:
            x = x + mod_ref[5:6, :] * xb_ref[rows, :]
            xo_ref[rows, :] = x
        hbs.append(_rms_mod(x, g_ref[...], mod_ref[0:1, :], mod_ref[1:2, :]).astype(BF16))

    def qk(rows, hb, lo, width, gain_ref, out_ref, scale):
        u = _dot(hb, w_ref[:, lo:lo + width])
        r = _head_norm(u, gmat_ref, width)
        un = u * r * gain_ref[0:1, :]
        if rope:
            off = lo - A_Q if lo == A_Q else ATT_Q_W
            usw = _dot(hb, wsw_ref[:, off:off + width])
            reps = width // LANES
            cos_t = jnp.concatenate([cos_ref[rows, :]] * reps, axis=1) if reps > 1 else cos_ref[rows, :]
            sin_t = jnp.concatenate([sin_ref[rows, :]] * reps, axis=1) if reps > 1 else sin_ref[rows, :]
            un = un * cos_t + (usw * r * gain_ref[1:2, :]) * sin_t
        out_ref[rows, :] = (un * scale).astype(out_ref.dtype)

    for rows, hb in zip(chunks, hbs):
        qk(rows, hb, A_Q, ATT_Q_W, qg_ref, q_ref, HEAD_DIM ** -0.5 * math.log2(math.e))
        qk(rows, hb, A_K, ATT_KV_W, kg_ref, k_ref, 1.0)
        v_ref[rows, :] = _dot(hb, w_ref[:, A_V:A_HY]).astype(v_ref.dtype)
        hy_ref[rows, :] = _dot(hb, w_ref[:, A_HY:A_FN]).astype(hy_ref.dtype)
        fn_ref[rows, :] = _dot(hb, w_ref[:, A_FN:A_END]).astype(fn_ref.dtype)


def _inproj(x, xb, mod, norm_g, w_in, layer, w_sw, cos_t, sin_t, qg, kg, gmat, n, tm):
    m, d = x.shape
    rope = w_sw is not None
    combine = xb is not None
    per_b = mod.shape[0] > 1
    tpb = n // tm
    row = lambda i: (i, 0)
    const = lambda i: (0, 0)
    mod_map = (lambda i: (i // tpb, 0, 0)) if per_b else (lambda i: (0, 0, 0))
    args, specs = [x], [pl.BlockSpec((tm, d), row)]
    if combine:
        args.append(xb)
        specs.append(pl.BlockSpec((tm, d), row))
    args += [mod, norm_g, w_in]
    specs += [pl.BlockSpec((None, 8, d), mod_map), _layer_spec(norm_g, layer),
              pl.BlockSpec((None, d, A_END), lambda i: (layer, 0, 0))]
    if rope:
        args += [w_sw, cos_t, sin_t]
        specs += [_layer_spec(w_sw, layer),
                  pl.BlockSpec((tm, LANES), lambda i: (i % tpb, 0)),
                  pl.BlockSpec((tm, LANES), lambda i: (i % tpb, 0))]
    args += [qg, kg, gmat]
    specs += [_layer_spec(qg, layer), _layer_spec(kg, layer), pl.BlockSpec(gmat.shape, const)]
    widths = [ATT_Q_W, ATT_KV_W, ATT_KV_W, (HY_ORDER + 1) * HY_W, FN_W]
    out_shape = [jax.ShapeDtypeStruct((m, w), BF16) for w in widths]
    out_specs = [pl.BlockSpec((tm, w), row) for w in widths]
    if combine:
        out_shape.insert(0, jax.ShapeDtypeStruct((m, d), F32))
        out_specs.insert(0, pl.BlockSpec((tm, d), row))
    return pl.pallas_call(
        functools.partial(_inproj_kernel, rope=rope, combine=combine),
        grid=(m // tm,), in_specs=specs, out_specs=out_specs, out_shape=out_shape,
        compiler_params=_cparams(("arbitrary",)), name="inproj",
    )(*args)


def _attn_kernel(q_ref, kt_ref, v_ref, o_ref):
    tq = q_ref.shape[0]
    hpc = ATTN_HEADS_PER_CHAIN
    chains = [(h0 // Q_GROUP, range(h0, h0 + hpc)) for h0 in range(0, N_Q_HEADS, hpc)]
    ss = []
    for g, heads in chains:
        qg = jnp.concatenate([q_ref[:, h * HEAD_DIM:(h + 1) * HEAD_DIM] for h in heads], axis=0)
        ss.append(_dot(qg, kt_ref[g * HEAD_DIM:(g + 1) * HEAD_DIM, :]))
    outs = []
    for (g, heads), s in zip(chains, ss):
        p = jnp.exp2(s - jnp.max(s, axis=-1, keepdims=True)).astype(BF16)
        o = _dot(p, v_ref[g])
        for j in range(hpc):
            oh = o[j * tq:(j + 1) * tq, :]
            outs.append(oh[:, 0:HEAD_DIM] / oh[:, HEAD_DIM:HEAD_DIM + 1])
    o_ref[...] = jnp.concatenate(outs, axis=1).astype(o_ref.dtype)


def _attention(q, kt, vaug, n, tq):
    m = q.shape[0]
    b, _, s = kt.shape
    tpb = n // tq
    return pl.pallas_call(
        _attn_kernel,
        grid=(b, tpb),
        in_specs=[pl.BlockSpec((tq, ATT_Q_W), lambda bi, i: (bi * tpb + i, 0)),
                  pl.BlockSpec((None, ATT_KV_W, s), lambda bi, i: (bi, 0, 0)),
                  pl.BlockSpec((None, N_KV_HEADS, s, LANES), lambda bi, i: (bi, 0, 0, 0))],
        out_specs=pl.BlockSpec((tq, ATT_Q_W), lambda bi, i: (bi * tpb + i, 0)),
        out_shape=jax.ShapeDtypeStruct((m, ATT_Q_W), BF16),
        compiler_params=_cparams(("arbitrary", "arbitrary")), name="attention",
    )(q, kt, vaug)


def _hyena_filter_kernel(feat_ref, w1_ref, b1_ref, w2_ref, b2_ref, w3_ref, freq_ref, delta_ref,
                         cn_ref, sn_ref, hr_ref, hs_ref, hn_ref):
    n = feat_ref.shape[0]
    hp = lax.Precision.HIGHEST
    freq = freq_ref[...]
    h = jnp.sin(freq * (jnp.dot(feat_ref[...], w1_ref[...], preferred_element_type=F32, precision=hp)
                        + b1_ref[...]))
    h = jnp.sin(freq * (jnp.dot(h, w2_ref[...], preferred_element_type=F32, precision=hp) + b2_ref[...]))
    h = jnp.dot(h, w3_ref[...], preferred_element_type=F32, precision=hp)
    w2o = HY_ORDER * HY_W
    pos = lax.broadcasted_iota(jnp.int32, (n, w2o), 0)
    t = pos.astype(F32) / (n - 1)
    decay = jnp.exp(-t * delta_ref[...])
    hf = h[:, 0:w2o] * decay
    hb = jnp.where(pos == 0, 0.0, h[:, w2o:2 * w2o] * decay)
    inv = 1.0 / (jnp.sum(jnp.abs(hf), axis=0, keepdims=True) + jnp.sum(jnp.abs(hb), axis=0, keepdims=True))
    hsum = (hf + hb) * inv
    hdif = (hf - hb) * inv
    sign = jnp.where(pos % 2 == 0, 1.0, -1.0)
    wk = jnp.where(pos == 0, 1.0, 2.0) * (1.0 / (2 * n))
    s_hi, s_lo = _split_bf16(hsum)
    d_hi, d_lo = _split_bf16(hdif)
    hr_ref[...] = (_dot(cn_ref[...], s_hi) + _dot(cn_ref[...], s_lo)) * wk
    hs_ref[...] = (_dot(sn_ref[...], d_hi) + _dot(sn_ref[...], d_lo)) * wk
    hn_ref[...] = jnp.sum(hsum * sign, axis=0, keepdims=True) * (1.0 / (2 * n))


def _hyena_filter(feats, w1, b1, w2, b2, w3, freq, layer, deltas, cn, sn):
    n = feats.shape[0]
    w2o = HY_ORDER * HY_W
    args = (feats, w1, b1, w2, b2, w3, freq, deltas, cn, sn)
    whole = lambda a: pl.BlockSpec(a.shape, lambda i: (0, 0))
    return pl.pallas_call(
        _hyena_filter_kernel,
        grid=(1,),
        in_specs=[whole(feats)] + [_layer_spec(a, layer) for a in args[1:7]] + [whole(a) for a in args[7:]],
        out_specs=[pl.BlockSpec((n, w2o), lambda i: (0, 0)), pl.BlockSpec((n, w2o), lambda i: (0, 0)),
                   pl.BlockSpec((1, w2o), lambda i: (0, 0))],
        out_shape=[jax.ShapeDtypeStruct((n, w2o), F32), jax.ShapeDtypeStruct((n, w2o), F32),
                   jax.ShapeDtypeStruct((1, w2o), F32)],
        compiler_params=_cparams(("arbitrary",)), name="hyena_filter",
    )(*args)


def _hyena_kernel(u_ref, sw_ref, sb_ref, hr_ref, hs_ref, hn_ref, d_ref, cn_ref, sn_ref, o_ref,
                  zf_ref, zb_ref, g1_ref, g2_ref, yr_ref, ys_ref):
    n = u_ref.shape[0]
    ck = min(n, HY_CHUNK)
    chunks = [slice(r, r + ck) for r in range(0, n, ck)]
    pos = lax.broadcasted_iota(jnp.int32, (n, HY_W), 0)
    sign = jnp.where(pos % 2 == 0, 1.0, -1.0)

    def short(j):
        u = u_ref[:, j * HY_W:(j + 1) * HY_W].astype(F32)
        up = jnp.where(pos == 0, 0.0, pltpu.roll(u, 1, 0))
        un = jnp.where(pos == n - 1, 0.0, pltpu.roll(u, n - 1, 0))
        c = slice(j * HY_W, (j + 1) * HY_W)
        return sb_ref[0:1, c] + up * sw_ref[0:1, c] + u * sw_ref[1:2, c] + un * sw_ref[2:3, c]

    v = short(0)
    zf_ref[...] = v
    zb_ref[...] = v.astype(BF16)
    g1_ref[...] = short(1)
    g2_ref[...] = short(2)
    gates = (g1_ref, g2_ref)
    csign = jnp.where(lax.broadcasted_iota(jnp.int32, (ck, HY_W), 0) % 2 == 0, 1.0, -1.0)

    for o in range(HY_ORDER):
        c = slice(o * HY_W, (o + 1) * HY_W)
        nyq = jnp.sum(zf_ref[...] * sign, axis=0, keepdims=True) * hn_ref[0:1, c]
        for r in chunks:
            xr = _dot(cn_ref[r, :], zb_ref[...])
            xs = _dot(sn_ref[r, :], zb_ref[...])
            hr = hr_ref[r, c]
            hs = hs_ref[r, c]
            yr_ref[r, :] = (xr * hr - xs * hs).astype(BF16)
            ys_ref[r, :] = (xr * hs + xs * hr).astype(BF16)
        for r in chunks:
            y = _dot(cn_ref[r, :], yr_ref[...]) + _dot(sn_ref[r, :], ys_ref[...])
            y = (y + csign * nyq + zf_ref[r, :] * d_ref[o:o + 1, :]) * gates[o][r, :]
            if o + 1 < HY_ORDER:
                zf_ref[r, :] = y
                zb_ref[r, :] = y.astype(BF16)
            else:
                o_ref[r, :] = y.astype(o_ref.dtype)


def _hyena(hy, sw, sb, hr, hs, hn, dbias, layer, cn, sn, n):
    m = hy.shape[0]
    const = lambda b: (0, 0)
    once = lambda a: pl.BlockSpec(a.shape, const, pipeline_mode=pl.Buffered(1))
    return pl.pallas_call(
        _hyena_kernel,
        grid=(m // n,),
        in_specs=[pl.BlockSpec((n, hy.shape[1]), lambda b: (b, 0)),
                  _layer_spec(sw, layer), _layer_spec(sb, layer),
                  once(hr), once(hs), pl.BlockSpec(hn.shape, const),
                  _layer_spec(dbias, layer), once(cn), once(sn)],
        out_specs=pl.BlockSpec((n, HY_W), lambda b: (b, 0)),
        out_shape=jax.ShapeDtypeStruct((m, HY_W), BF16),
        scratch_shapes=[pltpu.VMEM((n, HY_W), F32), pltpu.VMEM((n, HY_W), BF16),
                        pltpu.VMEM((n, HY_W), F32), pltpu.VMEM((n, HY_W), F32),
                        pltpu.VMEM((n, HY_W), BF16), pltpu.VMEM((n, HY_W), BF16)],
        compiler_params=_cparams(("arbitrary",)), name="hyena",
    )(hy, sw, sb, hr, hs, hn, dbias, cn, sn)


def _fnet_kernel(u_ref, bc_ref, bs_ref, cf_ref, sf_ref, o_ref):
    n = u_ref.shape[0]
    u = u_ref[...]
    uc = _dot(u, bc_ref[...]).astype(BF16)
    us = _dot(u, bs_ref[...]).astype(BF16)
    y = _dot(cf_ref[...], uc) - _dot(sf_ref[...], us)
    o_ref[...] = (y * (1.0 / math.sqrt(n * FN_GROUP_W))).astype(o_ref.dtype)


def _fnet(fn, bc, bs, cf, sf, n):
    m = fn.shape[0]
    const = lambda b: (0, 0)
    big = lambda a: pl.BlockSpec(a.shape, const, pipeline_mode=pl.Buffered(1))
    return pl.pallas_call(
        _fnet_kernel,
        grid=(m // n,),
        in_specs=[pl.BlockSpec((n, FN_W), lambda b: (b, 0)), pl.BlockSpec(bc.shape, const),
                  pl.BlockSpec(bs.shape, const), big(cf), big(sf)],
        out_specs=pl.BlockSpec((n, FN_W), lambda b: (b, 0)),
        out_shape=jax.ShapeDtypeStruct((m, FN_W), BF16),
        compiler_params=_cparams(("arbitrary",)), name="fnet",
    )(fn, bc, bs, cf, sf)


def _merge_kernel(x_ref, mod_ref, g1_ref, g2_ref, att_ref, hy_ref, fn_ref, wg_ref, wb_ref, wo_ref,
                  wrh_ref, wrl_ref, x1_ref, h2_ref, lg_ref):
    tm, d = x_ref.shape
    branches = ((att_ref, 0, ATT_Q_W), (hy_ref, ATT_Q_W, ATT_Q_W + HY_W), (fn_ref, ATT_Q_W + HY_W, MIX_W))
    rows = [slice(r, r + MERGE_CHUNK) for r in range(0, tm, MERGE_CHUNK)]
    xs = [x_ref[r, :] for r in rows]
    hbs = [_rms_mod(x, g1_ref[...], mod_ref[0:1, :], mod_ref[1:2, :]).astype(BF16) for x in xs]
    mixes = []
    for r, hb in zip(rows, hbs):
        mix = None
        for i, (b_ref, lo, hi) in enumerate(branches):
            gate = _dot(hb, wg_ref[:, OFF_GATE + i * d:OFF_GATE + (i + 1) * d])
            gate = 1.0 / (1.0 + jnp.exp(-gate))
            term = gate * _dot(b_ref[r, :], wb_ref[lo:hi, :])
            mix = term if mix is None else mix + term
        mixes.append(mix.astype(BF16))
    for r, x, mix in zip(rows, xs, mixes):
        y = _dot(mix, wo_ref[...])
        x1 = x + mod_ref[2:3, :] * y
        x1_ref[r, :] = x1
        h2 = _rms_mod(x1, g2_ref[...], mod_ref[3:4, :], mod_ref[4:5, :])
        hi, lo = _split_bf16(h2)
        h2_ref[r, :] = hi
        lg_ref[r, :] = _dot(hi, wrh_ref[...]) + _dot(lo, wrh_ref[...]) + _dot(hi, wrl_ref[...])


def _merge(x, mod, g1, g2, att, hyo, fno, wg, layer, wb, wo, wrh, wrl, n, tm):
    m, d = x.shape
    per_b = mod.shape[0] > 1
    tpb = n // tm
    row = lambda i: (i, 0)
    const = lambda i: (0, 0)
    mod_map = (lambda i: (i // tpb, 0, 0)) if per_b else (lambda i: (0, 0, 0))
    return pl.pallas_call(
        _merge_kernel,
        grid=(m // tm,),
        in_specs=[pl.BlockSpec((tm, d), row), pl.BlockSpec((None, 8, d), mod_map),
                  _layer_spec(g1, layer), _layer_spec(g2, layer),
                  pl.BlockSpec((tm, ATT_Q_W), row), pl.BlockSpec((tm, HY_W), row), pl.BlockSpec((tm, FN_W), row),
                  _layer_spec(wg, layer, pipeline_mode=pl.Buffered(1)),
                  _layer_spec(wb, layer), _layer_spec(wo, layer),
                  _layer_spec(wrh, layer), _layer_spec(wrl, layer)],
        out_specs=[pl.BlockSpec((tm, d), row), pl.BlockSpec((tm, d), row), pl.BlockSpec((tm, LANES), row)],
        out_shape=[jax.ShapeDtypeStruct((m, d), F32), jax.ShapeDtypeStruct((m, d), BF16),
                   jax.ShapeDtypeStruct((m, LANES), F32)],
        compiler_params=_cparams(("arbitrary",)), name="merge",
    )(x, mod, g1, g2, att, hyo, fno, wg, wb, wo, wrh, wrl)


def _route_kernel(lg_ref, tri_ref, rank_ref, aff_ref, rankt_ref, *, cap):
    n = lg_ref.shape[0]
    lt = lg_ref[...].T[0:N_EXPERTS, :]
    e = jnp.exp(lt - jnp.max(lt, axis=0, keepdims=True))
    aff = e / jnp.sum(e, axis=0, keepdims=True)
    bits = pltpu.bitcast(aff, jnp.int32)

    def step(i, thr):
        cand = thr | jnp.left_shift(jnp.int32(1), 30 - i)
        cnt = jnp.sum(jnp.where(bits >= cand, 1.0, 0.0), axis=1, keepdims=True)
        return jnp.where(cnt >= cap, cand, thr)

    thr = lax.fori_loop(0, 31, step, jnp.zeros((N_EXPERTS, 1), jnp.int32))
    gt = bits > thr
    eq = bits == thr
    need = cap - jnp.sum(jnp.where(gt, 1.0, 0.0), axis=1, keepdims=True)

    def excl_cumsum(mask):
        parts = []
        off = jnp.zeros((N_EXPERTS, 1), F32)
        for c in range(n // LANES):
            blk = mask[:, c * LANES:(c + 1) * LANES]
            parts.append(_dot(blk.astype(BF16), tri_ref[...]) + off)
            off = off + jnp.sum(blk, axis=1, keepdims=True)
        return jnp.concatenate(parts, axis=1)

    tie_rank = excl_cumsum(jnp.where(eq, 1.0, 0.0))
    sel = jnp.where(gt, 1.0, jnp.where(eq, jnp.where(tie_rank < need, 1.0, 0.0), 0.0))
    rank = jnp.where(sel > 0.5, excl_cumsum(sel), -1.0)
    rank_ref[...] = rank.astype(jnp.int32)
    aff_ref[...] = aff
    pad = jnp.full((LANES - N_EXPERTS, n), -1.0, F32)
    rankt_ref[...] = jnp.concatenate([rank, pad], axis=0).T.astype(BF16)


def _route(logits, tri, n, cap):
    m = logits.shape[0]
    b = m // n
    assert cap <= 256
    return pl.pallas_call(
        functools.partial(_route_kernel, cap=cap),
        grid=(b,),
        in_specs=[pl.BlockSpec((n, LANES), lambda i: (i, 0)), pl.BlockSpec(tri.shape, lambda i: (0, 0))],
        out_specs=[pl.BlockSpec((None, N_EXPERTS, n), lambda i: (i, 0, 0)),
                   pl.BlockSpec((None, N_EXPERTS, n), lambda i: (i, 0, 0)),
                   pl.BlockSpec((n, LANES), lambda i: (i, 0))],
        out_shape=[jax.ShapeDtypeStruct((b, N_EXPERTS, n), jnp.int32),
                   jax.ShapeDtypeStruct((b, N_EXPERTS, n), F32),
                   jax.ShapeDtypeStruct((m, LANES), BF16)],
        compiler_params=_cparams(("arbitrary",)), name="route",
    )(logits, tri)


def _gather_kernel(h_ref, rank_ref, aff_ref, xg_ref, w_ref, sel_ref, *, cap):
    n, d = h_ref.shape
    slot = lax.broadcasted_iota(jnp.int32, (cap, n), 0)
    for e in range(N_EXPERTS):
        hit = slot == rank_ref[e:e + 1, :]
        sel_ref[e * cap:(e + 1) * cap, :] = jnp.where(hit, 1.0, 0.0).astype(BF16)
        wcol = jnp.sum(jnp.where(hit, aff_ref[e:e + 1, :], 0.0), axis=1, keepdims=True)
        w_ref[e] = jnp.broadcast_to(wcol, (cap, LANES))
    tn = 256
    for c in range(d // tn):
        xg = _dot(sel_ref[...], h_ref[:, c * tn:(c + 1) * tn]).astype(BF16)
        for e in range(N_EXPERTS):
            xg_ref[e, :, c * tn:(c + 1) * tn] = xg[e * cap:(e + 1) * cap, :]


def _gather(h2, rank, aff, n, cap):
    m, d = h2.shape
    b = m // n
    return pl.pallas_call(
        functools.partial(_gather_kernel, cap=cap),
        grid=(b,),
        in_specs=[pl.BlockSpec((n, d), lambda i: (i, 0)),
                  pl.BlockSpec((None, N_EXPERTS, n), lambda i: (i, 0, 0)),
                  pl.BlockSpec((None, N_EXPERTS, n), lambda i: (i, 0, 0))],
        out_specs=[pl.BlockSpec((N_EXPERTS, None, cap, d), lambda i: (0, i, 0, 0)),
                   pl.BlockSpec((N_EXPERTS, None, cap, LANES), lambda i: (0, i, 0, 0))],
        out_shape=[jax.ShapeDtypeStruct((N_EXPERTS, b, cap, d), BF16),
                   jax.ShapeDtypeStruct((N_EXPERTS, b, cap, LANES), F32)],
        scratch_shapes=[pltpu.VMEM((N_EXPERTS * cap, n), BF16)],
        compiler_params=_cparams(("arbitrary",)), name="moe_gather",
    )(h2, rank, aff)


def _ffn_kernel(x_ref, w_ref, wg_ref, wu_ref, wd_ref, o_ref, acc_ref):
    f = pl.program_id(2)
    x = x_ref[...]
    d = acc_ref.shape[1]
    tf = 512
    y = None
    for c in range(wg_ref.shape[1] // tf):
        cols = slice(c * tf, (c + 1) * tf)
        a = _dot(x, wg_ref[:, cols].astype(BF16))
        u = _dot(x, wu_ref[:, cols].astype(BF16))
        hm = (a * (1.0 / (1.0 + jnp.exp(-a))) * u).astype(BF16)
        t = _dot(hm, wd_ref[cols, :].astype(BF16))
        y = t if y is None else y + t

    @pl.when(f == 0)
    def _():
        acc_ref[...] = y

    @pl.when(f > 0)
    def _():
        acc_ref[...] += y

    @pl.when(f == pl.num_programs(2) - 1)
    def _():
        wt = jnp.concatenate([w_ref[...]] * (d // LANES), axis=1)
        o_ref[...] = (acc_ref[...] * wt).astype(o_ref.dtype)


def _ffn(xg, w, wg, wu, wd, layer, tm, tf):
    ne, rows, d = xg.shape
    ff = wg.shape[3]
    return pl.pallas_call(
        _ffn_kernel,
        grid=(ne, rows // tm, ff // tf),
        in_specs=[pl.BlockSpec((None, tm, d), lambda e, i, f: (e, i, 0)),
                  pl.BlockSpec((None, tm, LANES), lambda e, i, f: (e, i, 0)),
                  pl.BlockSpec((None, None, d, tf), lambda e, i, f: (layer, e, 0, f)),
                  pl.BlockSpec((None, None, d, tf), lambda e, i, f: (layer, e, 0, f)),
                  pl.BlockSpec((None, None, tf, d), lambda e, i, f: (layer, e, f, 0))],
        out_specs=pl.BlockSpec((None, tm, d), lambda e, i, f: (e, i, 0)),
        out_shape=jax.ShapeDtypeStruct((ne, rows, d), BF16),
        scratch_shapes=[pltpu.VMEM((tm, d), F32)],
        compiler_params=_cparams(("arbitrary", "arbitrary", "arbitrary")), name="moe_ffn",
    )(xg, w, wg, wu, wd)


def _scatter_kernel(rankt_ref, y_ref, expand_ref, slot_ref, o_ref, selt_ref, *, cap):
    n = rankt_ref.shape[0]
    d = o_ref.shape[1]
    width = N_EXPERTS * cap
    if cap % LANES == 0:
        rank = rankt_ref[...].astype(F32)
        slot = lax.broadcasted_iota(jnp.int32, (n, cap), 1).astype(F32)
        for e in range(N_EXPERTS):
            r = jnp.broadcast_to(rank[:, e:e + 1], (n, cap))
            selt_ref[:, e * cap:(e + 1) * cap] = jnp.where(r == slot, 1.0, 0.0).astype(BF16)
    else:
        r = _dot(rankt_ref[...], expand_ref[...])
        selt_ref[...] = jnp.where(r == slot_ref[...], 1.0, 0.0).astype(BF16)
    y = y_ref[...].reshape(width, d)
    tn = 256
    for c in range(d // tn):
        o_ref[:, c * tn:(c + 1) * tn] = _dot(selt_ref[...], y[:, c * tn:(c + 1) * tn])


def _scatter(rankt, yw, expand, slotpat, n, cap):
    ne, b, _, d = yw.shape
    return pl.pallas_call(
        functools.partial(_scatter_kernel, cap=cap),
        grid=(b,),
        in_specs=[pl.BlockSpec((n, LANES), lambda i: (i, 0)),
                  pl.BlockSpec((ne, None, cap, d), lambda i: (0, i, 0, 0), pipeline_mode=pl.Buffered(1)),
                  pl.BlockSpec(expand.shape, lambda i: (0, 0)),
                  pl.BlockSpec(slotpat.shape, lambda i: (0, 0))],
        out_specs=pl.BlockSpec((n, d), lambda i: (i, 0)),
        out_shape=jax.ShapeDtypeStruct((b * n, d), F32),
        scratch_shapes=[pltpu.VMEM((n, ne * cap), BF16)],
        compiler_params=_cparams(("arbitrary",)), name="moe_scatter",
    )(rankt, yw, expand, slotpat)


def _moe(h2, rank, aff, rankt, wg, wu, wd, layer, tabs, n, cap):
    b = h2.shape[0] // n
    d = h2.shape[1]
    xg, w = _gather(h2, rank, aff, n, cap)
    rows = b * cap
    yw = _ffn(xg.reshape(N_EXPERTS, rows, d), w.reshape(N_EXPERTS, rows, LANES), wg, wu, wd, layer,
              min(rows, FFN_ROW_TILE), FFN_HIDDEN_TILE)
    return _scatter(rankt, yw.reshape(N_EXPERTS, b, cap, d), tabs["expand"], tabs["slotpat"], n, cap)


def _combine_kernel(x_ref, y_ref, mod_ref, g_ref, o_ref, *, norm):
    x = x_ref[...] + mod_ref[5:6, :] * y_ref[...]
    if norm:
        x = x * lax.rsqrt(jnp.mean(x * x, axis=-1, keepdims=True) + EPS) * g_ref[...]
    o_ref[...] = x


def _combine(x, y, mod, g, n, tm, norm):
    m, d = x.shape
    per_b = mod.shape[0] > 1
    tpb = n // tm
    row = lambda i: (i, 0)
    mod_map = (lambda i: (i // tpb, 0, 0)) if per_b else (lambda i: (0, 0, 0))
    return pl.pallas_call(
        functools.partial(_combine_kernel, norm=norm),
        grid=(m // tm,),
        in_specs=[pl.BlockSpec((tm, d), row), pl.BlockSpec((tm, d), row),
                  pl.BlockSpec((None, 8, d), mod_map), pl.BlockSpec((1, d), lambda i: (0, 0))],
        out_specs=pl.BlockSpec((tm, d), row),
        out_shape=jax.ShapeDtypeStruct((m, d), F32),
        compiler_params=_cparams(("arbitrary",)), name="combine",
    )(x, y, mod, g)


def _dft_tables(n, period):
    j = jnp.arange(n, dtype=jnp.int32)
    jk = (j[:, None] * j[None, :]) % period
    ang = jk.astype(F32) * (2.0 * math.pi / period)
    return jnp.cos(ang).astype(BF16), jnp.sin(ang).astype(BF16)


def _rope_tables(n):
    rows = n // GRID_W
    row = jnp.repeat(jnp.arange(rows, dtype=F32), GRID_W)
    col = jnp.tile(jnp.arange(GRID_W, dtype=F32), rows)
    inv = ROPE_THETA ** (-jnp.arange(0, ROPE_AXIS_DIM, 2, dtype=F32) / ROPE_AXIS_DIM)
    ang = jnp.concatenate([row[:, None] * inv, col[:, None] * inv], axis=-1)
    cos = jnp.repeat(jnp.cos(ang), 2, axis=1)
    sin = jnp.repeat(jnp.sin(ang), 2, axis=1) * jnp.tile(jnp.array([-1.0, 1.0], F32), HEAD_DIM // 2)
    return jnp.tile(cos, (1, LANES // HEAD_DIM)), jnp.tile(sin, (1, LANES // HEAD_DIM))


def _hyena_feats(n):
    pos = jnp.arange(n, dtype=F32)
    t = pos / (n - 1)
    bands = jnp.linspace(1e-4, HY_BANDS - 1, HY_BANDS, dtype=F32)
    ang = (2.0 * math.pi / n) * pos[:, None] * bands[None, :]
    feats = jnp.concatenate([t[:, None], jnp.cos(ang), -jnp.sin(ang)], axis=-1)
    return jnp.pad(feats, ((0, 0), (0, LANES - HY_EMB)))


def _pair_swap(a):
    s = a.shape
    return a.reshape(*s[:-1], s[-1] // 2, 2)[..., ::-1].reshape(s)


def _side(x, xb, n, mod, l, P, tabs, kv_ext, rope, tm, last_ctx):
    b = x.shape[0] // n
    outs = _inproj(x, xb, mod, P["norm1_g"], P["w_in"], l, P["w_sw"] if rope else None,
                   tabs["cos"] if rope else None, tabs["sin"] if rope else None,
                   P["qg"], P["kg"], P["gmat"], n, tm)
    if xb is not None:
        x, outs = outs[0], outs[1:]
    q, k, v, hy, fn = outs
    k3, v3 = k.reshape(b, n, ATT_KV_W), v.reshape(b, n, ATT_KV_W)
    if last_ctx:
        return None, None, k3, v3
    if kv_ext is not None:
        kc, vc = jnp.concatenate([k3, kv_ext[0]], axis=1), jnp.concatenate([v3, kv_ext[1]], axis=1)
    else:
        kc, vc = k3, v3
    s = kc.shape[1]
    fill = jnp.concatenate([jnp.ones((b, s, 1), BF16), jnp.zeros((b, s, LANES - HEAD_DIM - 1), BF16)], axis=-1)
    vaug = jnp.stack([jnp.concatenate([vc[..., g * HEAD_DIM:(g + 1) * HEAD_DIM], fill], axis=-1)
                      for g in range(N_KV_HEADS)], axis=1)
    att = _attention(q, jnp.swapaxes(kc, 1, 2), vaug, n, min(n, 256))
    hr, hs, hn = _hyena_filter(tabs["feats"], P["hy_w1"], P["hy_b1"], P["hy_w2"], P["hy_b2"],
                               P["hy_w3"], P["hy_freq"], l, tabs["deltas"], tabs["cn"], tabs["sn"])
    hyo = _hyena(hy, P["hy_sw"], P["hy_sb"], hr, hs, hn, P["hy_bias"], l, tabs["cn"], tabs["sn"], n)
    fno = _fnet(fn, tabs["bc"], tabs["bs"], tabs["cf"], tabs["sf"], n)
    x1, h2, logits = _merge(x, mod, P["norm1_g"], P["norm2_g"], att, hyo, fno, P["w_in"], l,
                            P["w_branch"], P["w_out"], P["wr_hi"], P["wr_lo"], n, tm)
    cap = CAPACITY_FACTOR * n // N_EXPERTS
    rank, aff, rankt = _route(logits, tabs["tri"], n, cap)
    y = _moe(h2, rank, aff, rankt, P["w_gate"], P["w_up"], P["w_down"], l, tabs, n, cap)
    return x1, y, k3, v3


def _tables(n, rope):
    cn, sn = _dft_tables(n, 2 * n)
    cf, sf = _dft_tables(n, n)
    a = jnp.arange(FN_W, dtype=jnp.int32)
    same = (a[:, None] // FN_GROUP_W) == (a[None, :] // FN_GROUP_W)
    ang = ((a[:, None] * a[None, :]) % FN_GROUP_W).astype(F32) * (2.0 * math.pi / FN_GROUP_W)
    deltas = jnp.abs(jnp.linspace(math.log(HY_DECAY_TARGET) / HY_DECAY_LONG_PCT,
                                  math.log(HY_DECAY_TARGET) / HY_DECAY_SHORT_PCT, HY_W, dtype=F32))
    i = jnp.arange(LANES, dtype=jnp.int32)
    cap = CAPACITY_FACTOR * n // N_EXPERTS
    j = jnp.arange(N_EXPERTS * cap, dtype=jnp.int32)
    tabs = dict(expand=(i[:, None] == j[None, :] // cap).astype(BF16),
                slotpat=(j % cap).astype(F32)[None, :],
                cn=cn, sn=sn, cf=cf, sf=sf,
                bc=jnp.where(same, jnp.cos(ang), 0.0).astype(BF16),
                bs=jnp.where(same, jnp.sin(ang), 0.0).astype(BF16),
                feats=_hyena_feats(n), deltas=jnp.tile(deltas, HY_ORDER)[None, :],
                tri=(i[:, None] < i[None, :]).astype(BF16))
    if rope:
        tabs["cos"], tabs["sin"] = _rope_tables(n)
    return tabs


def kernel(x, c, ctx, c_ctx, w_mod, b_mod, norm1_g, norm2_g, w_in, q_gain, k_gain, hy_short_w, hy_short_b,
           hy_f_w1, hy_f_b1, hy_f_w2, hy_f_b2, hy_f_w3, hy_f_freq, hy_bias, w_branch, w_out, w_router,
           w_gate, w_up, w_down, final_g):
    bsz, n_lat, d = x.shape
    n_ctx = ctx.shape[1]
    depth = w_mod.shape[0]
    assert d == D_MODEL and n_lat % LANES == 0 and n_ctx % LANES == 0

    wq, wk = w_in[:, :, OFF_Q:OFF_K], w_in[:, :, OFF_K:OFF_V]
    hid = jnp.arange(ATT_Q_W, dtype=jnp.int32) // HEAD_DIM
    P = dict(
        norm1_g=norm1_g[:, None, :], norm2_g=norm2_g[:, None, :],
        w_in=w_in.astype(BF16),
        w_sw=jnp.concatenate([_pair_swap(wq), _pair_swap(wk)], axis=-1).astype(BF16),
        qg=jnp.stack([jnp.tile(q_gain, (1, N_Q_HEADS)), jnp.tile(_pair_swap(q_gain), (1, N_Q_HEADS))], axis=1),
        kg=jnp.stack([jnp.tile(k_gain, (1, N_KV_HEADS)), jnp.tile(_pair_swap(k_gain), (1, N_KV_HEADS))], axis=1),
        gmat=(hid[:, None] == hid[None, :]).astype(BF16),
        hy_sw=hy_short_w, hy_sb=hy_short_b[:, None, :],
        hy_w1=jnp.pad(hy_f_w1, ((0, 0), (0, LANES - HY_EMB), (0, 0))), hy_b1=hy_f_b1[:, None, :],
        hy_w2=hy_f_w2, hy_b2=hy_f_b2[:, None, :], hy_w3=hy_f_w3, hy_freq=hy_f_freq[:, None, :],
        hy_bias=hy_bias,
        w_branch=w_branch.astype(BF16), w_out=w_out.astype(BF16),
        w_gate=w_gate, w_up=w_up, w_down=w_down,
    )
    wr = jnp.pad(w_router, ((0, 0), (0, 0), (0, LANES - N_EXPERTS)))
    P["wr_hi"] = wr.astype(BF16)
    P["wr_lo"] = (wr - P["wr_hi"].astype(F32)).astype(BF16)

    tab_x = _tables(n_lat, True)
    tab_c = _tables(n_ctx, False)

    rows = -(-(bsz + 1) // 8) * 8
    c_all = jnp.concatenate([c, c_ctx[None, :], jnp.zeros((rows - bsz - 1, d), F32)], axis=0)
    mod = _modulation(c_all, w_mod, b_mod).reshape(depth, rows, 6, d)
    mod = jnp.pad(mod, ((0, 0), (0, 0), (0, 2), (0, 0)))

    tm_x, tm_c = min(n_lat, 512), min(n_ctx, 512)
    xs, xpend = x.reshape(bsz * n_lat, d), None
    cs, cpend = ctx.reshape(bsz * n_ctx, d), None
    for l in range(depth):
        last = l == depth - 1
        mod_x, mod_c = mod[l, :bsz], mod[l, bsz:bsz + 1]
        pm_x = None if l == 0 else jnp.concatenate([mod_x[:, :5], mod[l - 1, :bsz, 5:6], mod_x[:, 6:]], axis=1)
        pm_c = None if l == 0 else jnp.concatenate([mod_c[:, :5], mod[l - 1, bsz:bsz + 1, 5:6], mod_c[:, 6:]],
                                                   axis=1)
        c1, cy, kc, vc = _side(cs, cpend, n_ctx, mod_c if l == 0 else pm_c, l, P, tab_c, None, False, tm_c, last)
        x1, xy, _, _ = _side(xs, xpend, n_lat, mod_x if l == 0 else pm_x, l, P, tab_x, (kc, vc), True, tm_x, False)
        xs, xpend = x1, xy
        if not last:
            cs, cpend = c1, cy
    out = _combine(xs, xpend, mod[depth - 1, :bsz], final_g[None, :], n_lat, tm_x, True)
    return out.reshape(bsz, n_lat, d)
```

```python
import functools
import math

import jax
import jax.numpy as jnp
from jax import lax
from jax.experimental import pallas as pl
from jax.experimental.pallas import tpu as pltpu

F32 = jnp.float32
BF16 = jnp.bfloat16

D_MODEL = 1024
GRID_W = 64
HEAD_DIM = 64
N_Q_HEADS = 8
N_KV_HEADS = 2
Q_GROUP = N_Q_HEADS // N_KV_HEADS
ATT_Q_W = N_Q_HEADS * HEAD_DIM
ATT_KV_W = N_KV_HEADS * HEAD_DIM
ROPE_THETA = 10000.0
ROPE_AXIS_DIM = HEAD_DIM // 2
HY_W = D_MODEL // 4
HY_ORDER = 2
HY_SHORT = 3
HY_BANDS = 16
HY_EMB = 2 * HY_BANDS + 1
HY_FFN = 64
HY_DECAY_TARGET = 1e-2
HY_DECAY_SHORT_PCT = 0.3
HY_DECAY_LONG_PCT = 1.5
FN_GROUPS = 4
FN_GROUP_W = D_MODEL // 16
FN_W = FN_GROUPS * FN_GROUP_W
MIX_W = ATT_Q_W + HY_W + FN_W
N_BRANCH = 3
OFF_Q = 0
OFF_K = OFF_Q + ATT_Q_W
OFF_V = OFF_K + ATT_KV_W
OFF_HY = OFF_V + ATT_KV_W
OFF_FN = OFF_HY + (HY_ORDER + 1) * HY_W
OFF_GATE = OFF_FN + FN_W
N_EXPERTS = 16
CAPACITY_FACTOR = 2
EPS = 1e-6

LANES = 128
VMEM_LIMIT = 56 * 1024 * 1024
FFN_ROW_TILE = 1024
FFN_HIDDEN_TILE = 1024
HY_CHUNK = 512
ROUTE_SAMPLES = 4
INPROJ_CHUNK = 256
MERGE_CHUNK = 256
ATTN_HEADS_PER_CHAIN = 4

A_Q, A_K, A_V, A_HY, A_FN, A_END = 0, 512, 640, 768, 1536, 1792


def _cparams(sem):
    return pltpu.CompilerParams(dimension_semantics=sem, vmem_limit_bytes=VMEM_LIMIT)


def _dot(a, b):
    return jnp.dot(a, b, preferred_element_type=F32)


def _split_bf16(x):
    hi = x.astype(BF16)
    lo = (x - hi.astype(F32)).astype(BF16)
    return hi, lo


def _layer_spec(arr, layer, **kw):
    zeros = (0,) * (arr.ndim - 1)
    return pl.BlockSpec((None,) + arr.shape[1:], lambda *_: (layer,) + zeros, **kw)


def _rms_mod(x, g, shift, scale):
    y = x * lax.rsqrt(jnp.mean(x * x, axis=-1, keepdims=True) + EPS)
    return (y * g) * (1.0 + scale) + shift


def _mod_kernel(c_ref, w_ref, b_ref, o_ref):
    c = c_ref[...]
    sc = c * (1.0 / (1.0 + jnp.exp(-c)))
    o_ref[...] = jnp.dot(sc, w_ref[...], preferred_element_type=F32,
                         precision=lax.Precision.HIGHEST) + b_ref[...]


def _modulation(c_all, w_mod, b_mod):
    depth, d, n6 = w_mod.shape
    rows = c_all.shape[0]
    tn = 1536
    return pl.pallas_call(
        _mod_kernel,
        grid=(depth, n6 // tn),
        in_specs=[pl.BlockSpec((rows, d), lambda l, j: (0, 0)),
                  pl.BlockSpec((None, d, tn), lambda l, j: (l, 0, j)),
                  pl.BlockSpec((None, 1, tn), lambda l, j: (l, 0, j))],
        out_specs=pl.BlockSpec((None, rows, tn), lambda l, j: (l, 0, j)),
        out_shape=jax.ShapeDtypeStruct((depth, rows, n6), F32),
        compiler_params=_cparams(("arbitrary", "arbitrary")),
        name="modulation",
    )(c_all, w_mod, b_mod.reshape(depth, 1, n6))


def _head_norm(u, gmat_ref, width):
    gm = gmat_ref[0:width, 0:width]
    ms = _dot((u * u).astype(BF16), gm) * (1.0 / HEAD_DIM)
    return lax.rsqrt(ms + EPS)


def _inproj_kernel(*refs, rope, combine):
    it = iter(refs)
    x_ref = next(it)
    if combine:
        xb_ref = next(it)
    mod_ref = next(it)
    g_ref = next(it)
    w_ref = next(it)
    if rope:
        cos_ref = next(it)
        sin_ref = next(it)
    qg_ref = next(it)
    kg_ref = next(it)
    gmat_ref = next(it)
    if combine:
        xo_ref = next(it)
    q_ref = next(it)
    k_ref = next(it)
    v_ref = next(it)
    hy_ref = next(it)
    fn_ref = next(it)

    tm = x_ref.shape[0]
    ck = min(tm, INPROJ_CHUNK)
    chunks = [slice(r0, r0 + ck) for r0 in range(0, tm, ck)]
    hbs = []
    for rows in chunks:
        x = x_ref[rows, :]
        if combine:
            x = x + mod_ref[5:6, :] * xb_ref[rows, :]
            xo_ref[rows, :] = x
        hbs.append(_rms_mod(x, g_ref[...], mod_ref[0:1, :], mod_ref[1:2, :]).astype(BF16))

    def qk(rows, hb, lo, width, gain_ref, out_ref, scale):
        u = _dot(hb, w_ref[:, lo:lo + width])
        r = _head_norm(u, gmat_ref, width)
        un = u * r * gain_ref[0:1, :]
        if rope:
            even = lax.broadcasted_iota(jnp.int32, (un.shape[0], LANES), 1) % 2 == 0
            parts = []
            for j in range(width // LANES):
                s = un[:, j * LANES:(j + 1) * LANES]
                sw = jnp.where(even, pltpu.roll(s, LANES - 1, 1), pltpu.roll(s, 1, 1))
                parts.append(s * cos_ref[rows, :] + sw * sin_ref[rows, :])
            un = jnp.concatenate(parts, axis=1) if len(parts) > 1 else parts[0]
        out_ref[rows, :] = (un * scale).astype(out_ref.dtype)

    for rows, hb in zip(chunks, hbs):
        qk(rows, hb, A_Q, ATT_Q_W, qg_ref, q_ref, HEAD_DIM ** -0.5 * math.log2(math.e))
        qk(rows, hb, A_K, ATT_KV_W, kg_ref, k_ref, 1.0)
        v_ref[rows, :] = _dot(hb, w_ref[:, A_V:A_HY]).astype(v_ref.dtype)
        hy_ref[rows, :] = _dot(hb, w_ref[:, A_HY:A_FN]).astype(hy_ref.dtype)
        fn_ref[rows, :] = _dot(hb, w_ref[:, A_FN:A_END]).astype(fn_ref.dtype)


def _inproj(x, xb, mod, norm_g, w_in, layer, cos_t, sin_t, qg, kg, gmat, n, tm):
    m, d = x.shape
    rope = cos_t is not None
    combine = xb is not None
    per_b = mod.shape[0] > 1
    tpb = n // tm
    row = lambda i: (i, 0)
    const = lambda i: (0, 0)
    mod_map = (lambda i: (i // tpb, 0, 0)) if per_b else (lambda i: (0, 0, 0))
    args, specs = [x], [pl.BlockSpec((tm, d), row)]
    if combine:
        args.append(xb)
        specs.append(pl.BlockSpec((tm, d), row))
    args += [mod, norm_g, w_in]
    specs += [pl.BlockSpec((None, 8, d), mod_map), _layer_spec(norm_g, layer),
              pl.BlockSpec((None, d, A_END), lambda i: (layer, 0, 0))]
    if rope:
        args += [cos_t, sin_t]
        specs += [pl.BlockSpec((tm, LANES), lambda i: (i % tpb, 0)),
                  pl.BlockSpec((tm, LANES), lambda i: (i % tpb, 0))]
    args += [qg, kg, gmat]
    specs += [_layer_spec(qg, layer), _layer_spec(kg, layer), pl.BlockSpec(gmat.shape, const)]
    widths = [ATT_Q_W, ATT_KV_W, ATT_KV_W, (HY_ORDER + 1) * HY_W, FN_W]
    out_shape = [jax.ShapeDtypeStruct((m, w), BF16) for w in widths]
    out_specs = [pl.BlockSpec((tm, w), row) for w in widths]
    if combine:
        out_shape.insert(0, jax.ShapeDtypeStruct((m, d), F32))
        out_specs.insert(0, pl.BlockSpec((tm, d), row))
    return pl.pallas_call(
        functools.partial(_inproj_kernel, rope=rope, combine=combine),
        grid=(m // tm,), in_specs=specs, out_specs=out_specs, out_shape=out_shape,
        compiler_params=_cparams(("arbitrary",)), name="inproj",
    )(*args)


def _attn_kernel(q_ref, kt_ref, v_ref, o_ref):
    tq = q_ref.shape[0]
    hpc = ATTN_HEADS_PER_CHAIN
    chains = [(h0 // Q_GROUP, range(h0, h0 + hpc)) for h0 in range(0, N_Q_HEADS, hpc)]
    ss = []
    for g, heads in chains:
        qg = jnp.concatenate([q_ref[:, h * HEAD_DIM:(h + 1) * HEAD_DIM] for h in heads], axis=0)
        ss.append(_dot(qg, kt_ref[g * HEAD_DIM:(g + 1) * HEAD_DIM, :]))
    outs = []
    for (g, heads), s in zip(chains, ss):
        p = jnp.exp2(s - jnp.max(s, axis=-1, keepdims=True)).astype(BF16)
        o = _dot(p, v_ref[g])
        for j in range(hpc):
            oh = o[j * tq:(j + 1) * tq, :]
            outs.append(oh[:, 0:HEAD_DIM] / oh[:, HEAD_DIM:HEAD_DIM + 1])
    o_ref[...] = jnp.concatenate(outs, axis=1).astype(o_ref.dtype)


def _attention(q, kt, vaug, n, tq):
    m = q.shape[0]
    b, _, s = kt.shape
    tpb = n // tq
    return pl.pallas_call(
        _attn_kernel,
        grid=(b, tpb),
        in_specs=[pl.BlockSpec((tq, ATT_Q_W), lambda bi, i: (bi * tpb + i, 0)),
                  pl.BlockSpec((None, ATT_KV_W, s), lambda bi, i: (bi, 0, 0)),
                  pl.BlockSpec((None, N_KV_HEADS, s, LANES), lambda bi, i: (bi, 0, 0, 0))],
        out_specs=pl.BlockSpec((tq, ATT_Q_W), lambda bi, i: (bi * tpb + i, 0)),
        out_shape=jax.ShapeDtypeStruct((m, ATT_Q_W), BF16),
        compiler_params=_cparams(("arbitrary", "arbitrary")), name="attention",
    )(q, kt, vaug)


def _hyena_filter_kernel(feat_ref, w1_ref, b1_ref, w2_ref, b2_ref, w3_ref, freq_ref, delta_ref,
                         cn_ref, sn_ref, hr_ref, hs_ref, hn_ref):
    n = feat_ref.shape[0]
    hp = lax.Precision.HIGHEST
    freq = freq_ref[...]
    h = jnp.sin(freq * (jnp.dot(feat_ref[...], w1_ref[...], preferred_element_type=F32, precision=hp)
                        + b1_ref[...]))
    h = jnp.sin(freq * (jnp.dot(h, w2_ref[...], preferred_element_type=F32, precision=hp) + b2_ref[...]))
    h = jnp.dot(h, w3_ref[...], preferred_element_type=F32, precision=hp)
    w2o = HY_ORDER * HY_W
    pos = lax.broadcasted_iota(jnp.int32, (n, w2o), 0)
    t = pos.astype(F32) / (n - 1)
    decay = jnp.exp(-t * delta_ref[...])
    hf = h[:, 0:w2o] * decay
    hb = jnp.where(pos == 0, 0.0, h[:, w2o:2 * w2o] * decay)
    inv = 1.0 / (jnp.sum(jnp.abs(hf), axis=0, keepdims=True) + jnp.sum(jnp.abs(hb), axis=0, keepdims=True))
    hsum = (hf + hb) * inv
    hdif = (hf - hb) * inv
    sign = jnp.where(pos % 2 == 0, 1.0, -1.0)
    wk = jnp.where(pos == 0, 1.0, 2.0) * (1.0 / (2 * n))
    s_hi, s_lo = _split_bf16(hsum)
    d_hi, d_lo = _split_bf16(hdif)
    hr_ref[...] = (_dot(cn_ref[...], s_hi) + _dot(cn_ref[...], s_lo)) * wk
    hs_ref[...] = (_dot(sn_ref[...], d_hi) + _dot(sn_ref[...], d_lo)) * wk
    hn_ref[...] = jnp.sum(hsum * sign, axis=0, keepdims=True) * (1.0 / (2 * n))


def _hyena_filter(feats, w1, b1, w2, b2, w3, freq, layer, deltas, cn, sn):
    n = feats.shape[0]
    w2o = HY_ORDER * HY_W
    args = (feats, w1, b1, w2, b2, w3, freq, deltas, cn, sn)
    whole = lambda a: pl.BlockSpec(a.shape, lambda i: (0, 0))
    return pl.pallas_call(
        _hyena_filter_kernel,
        grid=(1,),
        in_specs=[whole(feats)] + [_layer_spec(a, layer) for a in args[1:7]] + [whole(a) for a in args[7:]],
        out_specs=[pl.BlockSpec((n, w2o), lambda i: (0, 0)), pl.BlockSpec((n, w2o), lambda i: (0, 0)),
                   pl.BlockSpec((1, w2o), lambda i: (0, 0))],
        out_shape=[jax.ShapeDtypeStruct((n, w2o), F32), jax.ShapeDtypeStruct((n, w2o), F32),
                   jax.ShapeDtypeStruct((1, w2o), F32)],
        compiler_params=_cparams(("arbitrary",)), name="hyena_filter",
    )(*args)


def _hyena_kernel(u_ref, sw_ref, sb_ref, hr_ref, hs_ref, hn_ref, d_ref, cn_ref, sn_ref, o_ref,
                  zf_ref, zb_ref, g1_ref, g2_ref, yr_ref, ys_ref):
    n = u_ref.shape[0]
    ck = min(n, HY_CHUNK)
    chunks = [slice(r, r + ck) for r in range(0, n, ck)]
    pos = lax.broadcasted_iota(jnp.int32, (n, HY_W), 0)
    sign = jnp.where(pos % 2 == 0, 1.0, -1.0)

    def short(j):
        u = u_ref[:, j * HY_W:(j + 1) * HY_W].astype(F32)
        up = jnp.where(pos == 0, 0.0, pltpu.roll(u, 1, 0))
        un = jnp.where(pos == n - 1, 0.0, pltpu.roll(u, n - 1, 0))
        c = slice(j * HY_W, (j + 1) * HY_W)
        return sb_ref[0:1, c] + up * sw_ref[0:1, c] + u * sw_ref[1:2, c] + un * sw_ref[2:3, c]

    v = short(0)
    zf_ref[...] = v
    zb_ref[...] = v.astype(BF16)
    g1_ref[...] = short(1)
    g2_ref[...] = short(2)
    gates = (g1_ref, g2_ref)
    csign = jnp.where(lax.broadcasted_iota(jnp.int32, (ck, HY_W), 0) % 2 == 0, 1.0, -1.0)

    for o in range(HY_ORDER):
        c = slice(o * HY_W, (o + 1) * HY_W)
        nyq = jnp.sum(zf_ref[...] * sign, axis=0, keepdims=True) * hn_ref[0:1, c]
        for r in chunks:
            xr = _dot(cn_ref[r, :], zb_ref[...])
            xs = _dot(sn_ref[r, :], zb_ref[...])
            hr = hr_ref[r, c]
            hs = hs_ref[r, c]
            yr_ref[r, :] = (xr * hr - xs * hs).astype(BF16)
            ys_ref[r, :] = (xr * hs + xs * hr).astype(BF16)
        for r in chunks:
            y = _dot(cn_ref[r, :], yr_ref[...]) + _dot(sn_ref[r, :], ys_ref[...])
            y = (y + csign * nyq + zf_ref[r, :] * d_ref[o:o + 1, :]) * gates[o][r, :]
            if o + 1 < HY_ORDER:
                zf_ref[r, :] = y
                zb_ref[r, :] = y.astype(BF16)
            else:
                o_ref[r, :] = y.astype(o_ref.dtype)


def _hyena(hy, sw, sb, hr, hs, hn, dbias, layer, cn, sn, n):
    m = hy.shape[0]
    const = lambda b: (0, 0)
    once = lambda a: pl.BlockSpec(a.shape, const, pipeline_mode=pl.Buffered(1))
    return pl.pallas_call(
        _hyena_kernel,
        grid=(m // n,),
        in_specs=[pl.BlockSpec((n, hy.shape[1]), lambda b: (b, 0)),
                  _layer_spec(sw, layer), _layer_spec(sb, layer),
                  once(hr), once(hs), pl.BlockSpec(hn.shape, const),
                  _layer_spec(dbias, layer), once(cn), once(sn)],
        out_specs=pl.BlockSpec((n, HY_W), lambda b: (b, 0)),
        out_shape=jax.ShapeDtypeStruct((m, HY_W), BF16),
        scratch_shapes=[pltpu.VMEM((n, HY_W), F32), pltpu.VMEM((n, HY_W), BF16),
                        pltpu.VMEM((n, HY_W), F32), pltpu.VMEM((n, HY_W), F32),
                        pltpu.VMEM((n, HY_W), BF16), pltpu.VMEM((n, HY_W), BF16)],
        compiler_params=_cparams(("arbitrary",)), name="hyena",
    )(hy, sw, sb, hr, hs, hn, dbias, cn, sn)


def _fnet_kernel(u_ref, bc_ref, bs_ref, cf_ref, sf_ref, o_ref):
    n = u_ref.shape[0]
    u = u_ref[...]
    uc = _dot(u, bc_ref[...]).astype(BF16)
    us = _dot(u, bs_ref[...]).astype(BF16)
    y = _dot(cf_ref[...], uc) - _dot(sf_ref[...], us)
    o_ref[...] = (y * (1.0 / math.sqrt(n * FN_GROUP_W))).astype(o_ref.dtype)


def _fnet(fn, bc, bs, cf, sf, n):
    m = fn.shape[0]
    const = lambda b: (0, 0)
    big = lambda a: pl.BlockSpec(a.shape, const, pipeline_mode=pl.Buffered(1))
    return pl.pallas_call(
        _fnet_kernel,
        grid=(m // n,),
        in_specs=[pl.BlockSpec((n, FN_W), lambda b: (b, 0)), pl.BlockSpec(bc.shape, const),
                  pl.BlockSpec(bs.shape, const), big(cf), big(sf)],
        out_specs=pl.BlockSpec((n, FN_W), lambda b: (b, 0)),
        out_shape=jax.ShapeDtypeStruct((m, FN_W), BF16),
        compiler_params=_cparams(("arbitrary",)), name="fnet",
    )(fn, bc, bs, cf, sf)


def _merge_kernel(x_ref, mod_ref, g1_ref, g2_ref, att_ref, hy_ref, fn_ref, wg_ref, wb_ref, wo_ref,
                  wrh_ref, wrl_ref, x1_ref, h2_ref, lg_ref):
    tm, d = x_ref.shape
    branches = ((att_ref, 0, ATT_Q_W), (hy_ref, ATT_Q_W, ATT_Q_W + HY_W), (fn_ref, ATT_Q_W + HY_W, MIX_W))
    rows = [slice(r, r + MERGE_CHUNK) for r in range(0, tm, MERGE_CHUNK)]
    xs = [x_ref[r, :] for r in rows]
    hbs = [_rms_mod(x, g1_ref[...], mod_ref[0:1, :], mod_ref[1:2, :]).astype(BF16) for x in xs]
    mixes = []
    for r, hb in zip(rows, hbs):
        mix = None
        for i, (b_ref, lo, hi) in enumerate(branches):
            gate = _dot(hb, wg_ref[:, OFF_GATE + i * d:OFF_GATE + (i + 1) * d])
            gate = 1.0 / (1.0 + jnp.exp(-gate))
            term = gate * _dot(b_ref[r, :], wb_ref[lo:hi, :])
            mix = term if mix is None else mix + term
        mixes.append(mix.astype(BF16))
    for r, x, mix in zip(rows, xs, mixes):
        y = _dot(mix, wo_ref[...])
        x1 = x + mod_ref[2:3, :] * y
        x1_ref[r, :] = x1
        h2 = _rms_mod(x1, g2_ref[...], mod_ref[3:4, :], mod_ref[4:5, :])
        hi, lo = _split_bf16(h2)
        h2_ref[r, :] = hi
        lg_ref[r, :] = _dot(hi, wrh_ref[...]) + _dot(lo, wrh_ref[...]) + _dot(hi, wrl_ref[...])


def _merge(x, mod, g1, g2, att, hyo, fno, wg, layer, wb, wo, wrh, wrl, n, tm):
    m, d = x.shape
    per_b = mod.shape[0] > 1
    tpb = n // tm
    row = lambda i: (i, 0)
    const = lambda i: (0, 0)
    mod_map = (lambda i: (i // tpb, 0, 0)) if per_b else (lambda i: (0, 0, 0))
    return pl.pallas_call(
        _merge_kernel,
        grid=(m // tm,),
        in_specs=[pl.BlockSpec((tm, d), row), pl.BlockSpec((None, 8, d), mod_map),
                  _layer_spec(g1, layer), _layer_spec(g2, layer),
                  pl.BlockSpec((tm, ATT_Q_W), row), pl.BlockSpec((tm, HY_W), row), pl.BlockSpec((tm, FN_W), row),
                  _layer_spec(wg, layer, pipeline_mode=pl.Buffered(1)),
                  _layer_spec(wb, layer), _layer_spec(wo, layer),
                  _layer_spec(wrh, layer), _layer_spec(wrl, layer)],
        out_specs=[pl.BlockSpec((tm, d), row), pl.BlockSpec((tm, d), row), pl.BlockSpec((tm, LANES), row)],
        out_shape=[jax.ShapeDtypeStruct((m, d), F32), jax.ShapeDtypeStruct((m, d), BF16),
                   jax.ShapeDtypeStruct((m, LANES), F32)],
        compiler_params=_cparams(("arbitrary",)), name="merge",
    )(x, mod, g1, g2, att, hyo, fno, wg, wb, wo, wrh, wrl)


def _route_kernel(lg_ref, tri_ref, rank_ref, aff_ref, rankt_ref, *, cap):
    ns, _, n = rank_ref.shape
    rows = ns * N_EXPERTS
    affs = []
    for s in range(ns):
        lt = lg_ref[s * n:(s + 1) * n, :].T[0:N_EXPERTS, :]
        e = jnp.exp(lt - jnp.max(lt, axis=0, keepdims=True))
        affs.append(e / jnp.sum(e, axis=0, keepdims=True))
    aff = jnp.concatenate(affs, axis=0) if ns > 1 else affs[0]
    bits = pltpu.bitcast(aff, jnp.int32)

    def step(i, thr):
        cand = thr | jnp.left_shift(jnp.int32(1), 30 - i)
        cnt = jnp.sum(jnp.where(bits >= cand, 1.0, 0.0), axis=1, keepdims=True)
        return jnp.where(cnt >= cap, cand, thr)

    thr = lax.fori_loop(0, 31, step, jnp.zeros((rows, 1), jnp.int32))
    gt = bits > thr
    eq = bits == thr
    need = cap - jnp.sum(jnp.where(gt, 1.0, 0.0), axis=1, keepdims=True)

    def excl_cumsum(mask):
        parts = []
        off = jnp.zeros((rows, 1), F32)
        for c in range(n // LANES):
            blk = mask[:, c * LANES:(c + 1) * LANES]
            parts.append(_dot(blk.astype(BF16), tri_ref[...]) + off)
            off = off + jnp.sum(blk, axis=1, keepdims=True)
        return jnp.concatenate(parts, axis=1)

    tie_rank = excl_cumsum(jnp.where(eq, 1.0, 0.0))
    sel = jnp.where(gt, 1.0, jnp.where(eq, jnp.where(tie_rank < need, 1.0, 0.0), 0.0))
    rank = jnp.where(sel > 0.5, excl_cumsum(sel), -1.0)
    pad = jnp.full((LANES - N_EXPERTS, n), -1.0, F32)
    for s in range(ns):
        r = rank[s * N_EXPERTS:(s + 1) * N_EXPERTS, :]
        rank_ref[s] = r.astype(jnp.int32)
        aff_ref[s] = affs[s]
        rankt_ref[s * n:(s + 1) * n, :] = jnp.concatenate([r, pad], axis=0).T.astype(BF16)


def _route(logits, tri, n, cap):
    m = logits.shape[0]
    b = m // n
    assert cap <= 256
    ns = math.gcd(b, ROUTE_SAMPLES)
    return pl.pallas_call(
        functools.partial(_route_kernel, cap=cap),
        grid=(b // ns,),
        in_specs=[pl.BlockSpec((ns * n, LANES), lambda i: (i, 0)), pl.BlockSpec(tri.shape, lambda i: (0, 0))],
        out_specs=[pl.BlockSpec((ns, N_EXPERTS, n), lambda i: (i, 0, 0)),
                   pl.BlockSpec((ns, N_EXPERTS, n), lambda i: (i, 0, 0)),
                   pl.BlockSpec((ns * n, LANES), lambda i: (i, 0))],
        out_shape=[jax.ShapeDtypeStruct((b, N_EXPERTS, n), jnp.int32),
                   jax.ShapeDtypeStruct((b, N_EXPERTS, n), F32),
                   jax.ShapeDtypeStruct((m, LANES), BF16)],
        compiler_params=_cparams(("arbitrary",)), name="route",
    )(logits, tri)


def _gather_kernel(h_ref, rank_ref, aff_ref, xg_ref, w_ref, sel_ref, *, cap):
    n, d = h_ref.shape
    slot = lax.broadcasted_iota(jnp.int32, (cap, n), 0)
    for e in range(N_EXPERTS):
        hit = slot == rank_ref[e:e + 1, :]
        sel_ref[e * cap:(e + 1) * cap, :] = jnp.where(hit, 1.0, 0.0).astype(BF16)
        wcol = jnp.sum(jnp.where(hit, aff_ref[e:e + 1, :], 0.0), axis=1, keepdims=True)
        w_ref[e] = jnp.broadcast_to(wcol, (cap, LANES))
    tn = 256
    for c in range(d // tn):
        xg = _dot(sel_ref[...], h_ref[:, c * tn:(c + 1) * tn]).astype(BF16)
        for e in range(N_EXPERTS):
            xg_ref[e, :, c * tn:(c + 1) * tn] = xg[e * cap:(e + 1) * cap, :]


def _gather(h2, rank, aff, n, cap):
    m, d = h2.shape
    b = m // n
    return pl.pallas_call(
        functools.partial(_gather_kernel, cap=cap),
        grid=(b,),
        in_specs=[pl.BlockSpec((n, d), lambda i: (i, 0)),
                  pl.BlockSpec((None, N_EXPERTS, n), lambda i: (i, 0, 0)),
                  pl.BlockSpec((None, N_EXPERTS, n), lambda i: (i, 0, 0))],
        out_specs=[pl.BlockSpec((N_EXPERTS, None, cap, d), lambda i: (0, i, 0, 0)),
                   pl.BlockSpec((N_EXPERTS, None, cap, LANES), lambda i: (0, i, 0, 0))],
        out_shape=[jax.ShapeDtypeStruct((N_EXPERTS, b, cap, d), BF16),
                   jax.ShapeDtypeStruct((N_EXPERTS, b, cap, LANES), F32)],
        scratch_shapes=[pltpu.VMEM((N_EXPERTS * cap, n), BF16)],
        compiler_params=_cparams(("arbitrary",)), name="moe_gather",
    )(h2, rank, aff)


def _ffn_kernel(*refs, extra):
    if extra:
        x_ref, w_ref, xc_ref, wc_ref, wg_ref, wu_ref, wd_ref, o_ref, oc_ref, acc_ref, accc_ref = refs
    else:
        x_ref, w_ref, wg_ref, wu_ref, wd_ref, o_ref, acc_ref = refs
    f = pl.program_id(2)

    def run(x_ref, w_ref, o_ref, acc_ref):
        x = x_ref[...]
        d = acc_ref.shape[1]
        tf = 512
        y = None
        for c in range(wg_ref.shape[1] // tf):
            cols = slice(c * tf, (c + 1) * tf)
            a = _dot(x, wg_ref[:, cols].astype(BF16))
            u = _dot(x, wu_ref[:, cols].astype(BF16))
            hm = (a * (1.0 / (1.0 + jnp.exp(-a))) * u).astype(BF16)
            t = _dot(hm, wd_ref[cols, :].astype(BF16))
            y = t if y is None else y + t

        @pl.when(f == 0)
        def _():
            acc_ref[...] = y

        @pl.when(f > 0)
        def _():
            acc_ref[...] += y

        @pl.when(f == pl.num_programs(2) - 1)
        def _():
            wt = jnp.concatenate([w_ref[...]] * (d // LANES), axis=1)
            o_ref[...] = (acc_ref[...] * wt).astype(o_ref.dtype)

    run(x_ref, w_ref, o_ref, acc_ref)
    if extra:
        @pl.when(pl.program_id(1) == 0)
        def _():
            run(xc_ref, wc_ref, oc_ref, accc_ref)


def _ffn(xg, w, xgc, wc, wg, wu, wd, layer, tm, tf):
    ne, rows, d = xg.shape
    ff = wg.shape[3]
    extra = xgc is not None
    tile = lambda e, i, f: (e, i, 0)
    whole = lambda e, i, f: (e, 0, 0)
    args = [xg, w]
    in_specs = [pl.BlockSpec((None, tm, d), tile), pl.BlockSpec((None, tm, LANES), tile)]
    out_specs = [pl.BlockSpec((None, tm, d), tile)]
    out_shape = [jax.ShapeDtypeStruct((ne, rows, d), BF16)]
    scratch = [pltpu.VMEM((tm, d), F32)]
    if extra:
        rc = xgc.shape[1]
        args += [xgc, wc]
        in_specs += [pl.BlockSpec((None, rc, d), whole), pl.BlockSpec((None, rc, LANES), whole)]
        out_specs.append(pl.BlockSpec((None, rc, d), whole))
        out_shape.append(jax.ShapeDtypeStruct((ne, rc, d), BF16))
        scratch.append(pltpu.VMEM((rc, d), F32))
    args += [wg, wu, wd]
    in_specs += [pl.BlockSpec((None, None, d, tf), lambda e, i, f: (layer, e, 0, f)),
                 pl.BlockSpec((None, None, d, tf), lambda e, i, f: (layer, e, 0, f)),
                 pl.BlockSpec((None, None, tf, d), lambda e, i, f: (layer, e, f, 0))]
    outs = pl.pallas_call(
        functools.partial(_ffn_kernel, extra=extra),
        grid=(ne, rows // tm, ff // tf),
        in_specs=in_specs, out_specs=out_specs, out_shape=out_shape, scratch_shapes=scratch,
        compiler_params=_cparams(("arbitrary", "arbitrary", "arbitrary")), name="moe_ffn",
    )(*args)
    return (outs[0], outs[1]) if extra else (outs[0], None)


def _scatter_kernel(rankt_ref, y_ref, expand_ref, slot_ref, o_ref, selt_ref, *, cap):
    n = rankt_ref.shape[0]
    d = o_ref.shape[1]
    width = N_EXPERTS * cap
    if cap % LANES == 0:
        rank = rankt_ref[...].astype(F32)
        slot = lax.broadcasted_iota(jnp.int32, (n, cap), 1).astype(F32)
        for e in range(N_EXPERTS):
            r = jnp.broadcast_to(rank[:, e:e + 1], (n, cap))
            selt_ref[:, e * cap:(e + 1) * cap] = jnp.where(r == slot, 1.0, 0.0).astype(BF16)
    else:
        r = _dot(rankt_ref[...], expand_ref[...])
        selt_ref[...] = jnp.where(r == slot_ref[...], 1.0, 0.0).astype(BF16)
    y = y_ref[...].reshape(width, d)
    tn = 256
    for c in range(d // tn):
        o_ref[:, c * tn:(c + 1) * tn] = _dot(selt_ref[...], y[:, c * tn:(c + 1) * tn])


def _scatter(rankt, yw, expand, slotpat, n, cap):
    ne, b, _, d = yw.shape
    return pl.pallas_call(
        functools.partial(_scatter_kernel, cap=cap),
        grid=(b,),
        in_specs=[pl.BlockSpec((n, LANES), lambda i: (i, 0)),
                  pl.BlockSpec((ne, None, cap, d), lambda i: (0, i, 0, 0), pipeline_mode=pl.Buffered(1)),
                  pl.BlockSpec(expand.shape, lambda i: (0, 0)),
                  pl.BlockSpec(slotpat.shape, lambda i: (0, 0))],
        out_specs=pl.BlockSpec((n, d), lambda i: (i, 0)),
        out_shape=jax.ShapeDtypeStruct((b * n, d), F32),
        scratch_shapes=[pltpu.VMEM((n, ne * cap), BF16)],
        compiler_params=_cparams(("arbitrary",)), name="moe_scatter",
    )(rankt, yw, expand, slotpat)


def _moe(routed, routed_extra, wg, wu, wd, layer):
    def flat(r):
        rows = r["b"] * r["cap"]
        return r["xg"].reshape(N_EXPERTS, rows, D_MODEL), r["w"].reshape(N_EXPERTS, rows, LANES)

    def scatter(r, yw):
        yw = yw.reshape(N_EXPERTS, r["b"], r["cap"], D_MODEL)
        return _scatter(r["rankt"], yw, r["tabs"]["expand"], r["tabs"]["slotpat"], r["n"], r["cap"])

    xg, w = flat(routed)
    xgc, wc = flat(routed_extra) if routed_extra is not None else (None, None)
    yw, ywc = _ffn(xg, w, xgc, wc, wg, wu, wd, layer, min(xg.shape[1], FFN_ROW_TILE), FFN_HIDDEN_TILE)
    return scatter(routed, yw), (scatter(routed_extra, ywc) if routed_extra is not None else None)


def _combine_kernel(x_ref, y_ref, mod_ref, g_ref, o_ref, *, norm):
    x = x_ref[...] + mod_ref[5:6, :] * y_ref[...]
    if norm:
        x = x * lax.rsqrt(jnp.mean(x * x, axis=-1, keepdims=True) + EPS) * g_ref[...]
    o_ref[...] = x


def _combine(x, y, mod, g, n, tm, norm):
    m, d = x.shape
    per_b = mod.shape[0] > 1
    tpb = n // tm
    row = lambda i: (i, 0)
    mod_map = (lambda i: (i // tpb, 0, 0)) if per_b else (lambda i: (0, 0, 0))
    return pl.pallas_call(
        functools.partial(_combine_kernel, norm=norm),
        grid=(m // tm,),
        in_specs=[pl.BlockSpec((tm, d), row), pl.BlockSpec((tm, d), row),
                  pl.BlockSpec((None, 8, d), mod_map), pl.BlockSpec((1, d), lambda i: (0, 0))],
        out_specs=pl.BlockSpec((tm, d), row),
        out_shape=jax.ShapeDtypeStruct((m, d), F32),
        compiler_params=_cparams(("arbitrary",)), name="combine",
    )(x, y, mod, g)


def _dft_tables(n, period):
    j = jnp.arange(n, dtype=jnp.int32)
    jk = (j[:, None] * j[None, :]) % period
    ang = jk.astype(F32) * (2.0 * math.pi / period)
    return jnp.cos(ang).astype(BF16), jnp.sin(ang).astype(BF16)


def _rope_tables(n):
    rows = n // GRID_W
    row = jnp.repeat(jnp.arange(rows, dtype=F32), GRID_W)
    col = jnp.tile(jnp.arange(GRID_W, dtype=F32), rows)
    inv = ROPE_THETA ** (-jnp.arange(0, ROPE_AXIS_DIM, 2, dtype=F32) / ROPE_AXIS_DIM)
    ang = jnp.concatenate([row[:, None] * inv, col[:, None] * inv], axis=-1)
    cos = jnp.repeat(jnp.cos(ang), 2, axis=1)
    sin = jnp.repeat(jnp.sin(ang), 2, axis=1) * jnp.tile(jnp.array([-1.0, 1.0], F32), HEAD_DIM // 2)
    return jnp.tile(cos, (1, LANES // HEAD_DIM)), jnp.tile(sin, (1, LANES // HEAD_DIM))


def _hyena_feats(n):
    pos = jnp.arange(n, dtype=F32)
    t = pos / (n - 1)
    bands = jnp.linspace(1e-4, HY_BANDS - 1, HY_BANDS, dtype=F32)
    ang = (2.0 * math.pi / n) * pos[:, None] * bands[None, :]
    feats = jnp.concatenate([t[:, None], jnp.cos(ang), -jnp.sin(ang)], axis=-1)
    return jnp.pad(feats, ((0, 0), (0, LANES - HY_EMB)))


def _side(x, xb, n, mod, l, P, tabs, kv_ext, rope, tm, last_ctx):
    b = x.shape[0] // n
    outs = _inproj(x, xb, mod, P["norm1_g"], P["w_in"], l, tabs["cos"] if rope else None, tabs["sin"] if rope else None,
                   P["qg"], P["kg"], P["gmat"], n, tm)
    if xb is not None:
        x, outs = outs[0], outs[1:]
    q, k, v, hy, fn = outs
    k3, v3 = k.reshape(b, n, ATT_KV_W), v.reshape(b, n, ATT_KV_W)
    if last_ctx:
        return None, None, k3, v3
    if kv_ext is not None:
        kc, vc = jnp.concatenate([k3, kv_ext[0]], axis=1), jnp.concatenate([v3, kv_ext[1]], axis=1)
    else:
        kc, vc = k3, v3
    s = kc.shape[1]
    fill = jnp.concatenate([jnp.ones((b, s, 1), BF16), jnp.zeros((b, s, LANES - HEAD_DIM - 1), BF16)], axis=-1)
    vaug = jnp.stack([jnp.concatenate([vc[..., g * HEAD_DIM:(g + 1) * HEAD_DIM], fill], axis=-1)
                      for g in range(N_KV_HEADS)], axis=1)
    att = _attention(q, jnp.swapaxes(kc, 1, 2), vaug, n, min(n, 256))
    hr, hs, hn = _hyena_filter(tabs["feats"], P["hy_w1"], P["hy_b1"], P["hy_w2"], P["hy_b2"],
                               P["hy_w3"], P["hy_freq"], l, tabs["deltas"], tabs["cn"], tabs["sn"])
    hyo = _hyena(hy, P["hy_sw"], P["hy_sb"], hr, hs, hn, P["hy_bias"], l, tabs["cn"], tabs["sn"], n)
    fno = _fnet(fn, tabs["bc"], tabs["bs"], tabs["cf"], tabs["sf"], n)
    x1, h2, logits = _merge(x, mod, P["norm1_g"], P["norm2_g"], att, hyo, fno, P["w_in"], l,
                            P["w_branch"], P["w_out"], P["wr_hi"], P["wr_lo"], n, tm)
    cap = CAPACITY_FACTOR * n // N_EXPERTS
    rank, aff, rankt = _route(logits, tabs["tri"], n, cap)
    xg, w = _gather(h2, rank, aff, n, cap)
    return x1, dict(xg=xg, w=w, rankt=rankt, tabs=tabs, n=n, cap=cap, b=b), k3, v3


def _tables(n, rope):
    cn, sn = _dft_tables(n, 2 * n)
    cf, sf = _dft_tables(n, n)
    a = jnp.arange(FN_W, dtype=jnp.int32)
    same = (a[:, None] // FN_GROUP_W) == (a[None, :] // FN_GROUP_W)
    ang = ((a[:, None] * a[None, :]) % FN_GROUP_W).astype(F32) * (2.0 * math.pi / FN_GROUP_W)
    deltas = jnp.abs(jnp.linspace(math.log(HY_DECAY_TARGET) / HY_DECAY_LONG_PCT,
                                  math.log(HY_DECAY_TARGET) / HY_DECAY_SHORT_PCT, HY_W, dtype=F32))
    i = jnp.arange(LANES, dtype=jnp.int32)
    cap = CAPACITY_FACTOR * n // N_EXPERTS
    j = jnp.arange(N_EXPERTS * cap, dtype=jnp.int32)
    tabs = dict(expand=(i[:, None] == j[None, :] // cap).astype(BF16),
                slotpat=(j % cap).astype(F32)[None, :],
                cn=cn, sn=sn, cf=cf, sf=sf,
                bc=jnp.where(same, jnp.cos(ang), 0.0).astype(BF16),
                bs=jnp.where(same, jnp.sin(ang), 0.0).astype(BF16),
                feats=_hyena_feats(n), deltas=jnp.tile(deltas, HY_ORDER)[None, :],
                tri=(i[:, None] < i[None, :]).astype(BF16))
    if rope:
        tabs["cos"], tabs["sin"] = _rope_tables(n)
    return tabs


def kernel(x, c, ctx, c_ctx, w_mod, b_mod, norm1_g, norm2_g, w_in, q_gain, k_gain, hy_short_w, hy_short_b,
           hy_f_w1, hy_f_b1, hy_f_w2, hy_f_b2, hy_f_w3, hy_f_freq, hy_bias, w_branch, w_out, w_router,
           w_gate, w_up, w_down, final_g):
    bsz, n_lat, d = x.shape
    n_ctx = ctx.shape[1]
    depth = w_mod.shape[0]
    assert d == D_MODEL and n_lat % LANES == 0 and n_ctx % LANES == 0

    hid = jnp.arange(ATT_Q_W, dtype=jnp.int32) // HEAD_DIM
    P = dict(
        norm1_g=norm1_g[:, None, :], norm2_g=norm2_g[:, None, :],
        w_in=w_in.astype(BF16),
        qg=jnp.tile(q_gain, (1, N_Q_HEADS))[:, None, :], kg=jnp.tile(k_gain, (1, N_KV_HEADS))[:, None, :],
        gmat=(hid[:, None] == hid[None, :]).astype(BF16),
        hy_sw=hy_short_w, hy_sb=hy_short_b[:, None, :],
        hy_w1=jnp.pad(hy_f_w1, ((0, 0), (0, LANES - HY_EMB), (0, 0))), hy_b1=hy_f_b1[:, None, :],
        hy_w2=hy_f_w2, hy_b2=hy_f_b2[:, None, :], hy_w3=hy_f_w3, hy_freq=hy_f_freq[:, None, :],
        hy_bias=hy_bias,
        w_branch=w_branch.astype(BF16), w_out=w_out.astype(BF16),
        w_gate=w_gate, w_up=w_up, w_down=w_down,
    )
    wr = jnp.pad(w_router, ((0, 0), (0, 0), (0, LANES - N_EXPERTS)))
    P["wr_hi"] = wr.astype(BF16)
    P["wr_lo"] = (wr - P["wr_hi"].astype(F32)).astype(BF16)

    tab_x = _tables(n_lat, True)
    tab_c = _tables(n_ctx, False)

    rows = -(-(bsz + 1) // 8) * 8
    c_all = jnp.concatenate([c, c_ctx[None, :], jnp.zeros((rows - bsz - 1, d), F32)], axis=0)
    mod = _modulation(c_all, w_mod, b_mod).reshape(depth, rows, 6, d)
    mod = jnp.pad(mod, ((0, 0), (0, 0), (0, 2), (0, 0)))

    tm_x, tm_c = min(n_lat, 512), min(n_ctx, 512)
    xs, xpend = x.reshape(bsz * n_lat, d), None
    cs, cpend = ctx.reshape(bsz * n_ctx, d), None
    for l in range(depth):
        last = l == depth - 1
        mod_x, mod_c = mod[l, :bsz], mod[l, bsz:bsz + 1]
        pm_x = None if l == 0 else jnp.concatenate([mod_x[:, :5], mod[l - 1, :bsz, 5:6], mod_x[:, 6:]], axis=1)
        pm_c = None if l == 0 else jnp.concatenate([mod_c[:, :5], mod[l - 1, bsz:bsz + 1, 5:6], mod_c[:, 6:]],
                                                   axis=1)
        c1, rc, kc, vc = _side(cs, cpend, n_ctx, mod_c if l == 0 else pm_c, l, P, tab_c, None, False, tm_c, last)
        x1, rx, _, _ = _side(xs, xpend, n_lat, mod_x if l == 0 else pm_x, l, P, tab_x, (kc, vc), True, tm_x, False)
        xy, cy = _moe(rx, rc, P["w_gate"], P["w_up"], P["w_down"], l)
        xs, xpend = x1, xy
        if not last:
            cs, cpend = c1, cy
    out = _combine(xs, xpend, mod[depth - 1, :bsz], final_g[None, :], n_lat, tm_x, True)
    return out.reshape(bsz, n_lat, d)
```

```python
import functools
import math

import jax
import jax.numpy as jnp
from jax import lax
from jax.experimental import pallas as pl
from jax.experimental.pallas import tpu as pltpu

F32 = jnp.float32
BF16 = jnp.bfloat16

D_MODEL = 1024
GRID_W = 64
HEAD_DIM = 64
N_Q_HEADS = 8
N_KV_HEADS = 2
Q_GROUP = N_Q_HEADS // N_KV_HEADS
ATT_Q_W = N_Q_HEADS * HEAD_DIM
ATT_KV_W = N_KV_HEADS * HEAD_DIM
ROPE_THETA = 10000.0
ROPE_AXIS_DIM = HEAD_DIM // 2
HY_W = D_MODEL // 4
HY_ORDER = 2
HY_SHORT = 3
HY_BANDS = 16
HY_EMB = 2 * HY_BANDS + 1
HY_FFN = 64
HY_DECAY_TARGET = 1e-2
HY_DECAY_SHORT_PCT = 0.3
HY_DECAY_LONG_PCT = 1.5
FN_GROUPS = 4
FN_GROUP_W = D_MODEL // 16
FN_W = FN_GROUPS * FN_GROUP_W
MIX_W = ATT_Q_W + HY_W + FN_W
N_BRANCH = 3
OFF_Q = 0
OFF_K = OFF_Q + ATT_Q_W
OFF_V = OFF_K + ATT_KV_W
OFF_HY = OFF_V + ATT_KV_W
OFF_FN = OFF_HY + (HY_ORDER + 1) * HY_W
OFF_GATE = OFF_FN + FN_W
N_EXPERTS = 16
CAPACITY_FACTOR = 2
EPS = 1e-6

LANES = 128
VMEM_LIMIT = 56 * 1024 * 1024
ROW_TILE = 512
ATTN_Q_TILE = 256
FFN_ROW_TILE = 1024
FFN_HIDDEN_TILE = 1024
HY_CHUNK = 512
ROUTE_SAMPLES = 4
INPROJ_CHUNK = 256
MERGE_CHUNK = 256
ATTN_HEADS_PER_CHAIN = 4

A_Q, A_K, A_V, A_HY, A_FN, A_END = 0, 512, 640, 768, 1536, 1792


def _cparams(sem):
    return pltpu.CompilerParams(dimension_semantics=sem, vmem_limit_bytes=VMEM_LIMIT)


def _dot(a, b):
    return jnp.dot(a, b, preferred_element_type=F32)


def _split_bf16(x):
    hi = x.astype(BF16)
    lo = (x - hi.astype(F32)).astype(BF16)
    return hi, lo


def _layer_spec(arr, layer, **kw):
    zeros = (0,) * (arr.ndim - 1)
    return pl.BlockSpec((None,) + arr.shape[1:], lambda *_: (layer,) + zeros, **kw)


def _rms_mod(x, g, shift, scale):
    y = x * lax.rsqrt(jnp.mean(x * x, axis=-1, keepdims=True) + EPS)
    return (y * g) * (1.0 + scale) + shift


def _mod_kernel(c_ref, w_ref, b_ref, o_ref):
    c = c_ref[...]
    sc = c * (1.0 / (1.0 + jnp.exp(-c)))
    o_ref[...] = jnp.dot(sc, w_ref[...], preferred_element_type=F32,
                         precision=lax.Precision.HIGHEST) + b_ref[...]


def _modulation(c_all, w_mod, b_mod):
    depth, d, n6 = w_mod.shape
    rows = c_all.shape[0]
    tn = 1536
    return pl.pallas_call(
        _mod_kernel,
        grid=(depth, n6 // tn),
        in_specs=[pl.BlockSpec((rows, d), lambda l, j: (0, 0)),
                  pl.BlockSpec((None, d, tn), lambda l, j: (l, 0, j)),
                  pl.BlockSpec((None, 1, tn), lambda l, j: (l, 0, j))],
        out_specs=pl.BlockSpec((None, rows, tn), lambda l, j: (l, 0, j)),
        out_shape=jax.ShapeDtypeStruct((depth, rows, n6), F32),
        compiler_params=_cparams(("arbitrary", "arbitrary")),
        name="modulation",
    )(c_all, w_mod, b_mod.reshape(depth, 1, n6))


def _head_norm(u, gmat_ref, width):
    gm = gmat_ref[0:width, 0:width]
    ms = _dot((u * u).astype(BF16), gm) * (1.0 / HEAD_DIM)
    return lax.rsqrt(ms + EPS)


def _inproj_kernel(*refs, rope, combine):
    it = iter(refs)
    x_ref = next(it)
    if combine:
        xb_ref = next(it)
    mod_ref = next(it)
    g_ref = next(it)
    w_ref = next(it)
    if rope:
        cos_ref = next(it)
        sin_ref = next(it)
    qg_ref = next(it)
    kg_ref = next(it)
    gmat_ref = next(it)
    if combine:
        xo_ref = next(it)
    q_ref = next(it)
    k_ref = next(it)
    v_ref = next(it)
    hy_ref = next(it)
    fn_ref = next(it)

    tm = x_ref.shape[0]
    ck = min(tm, INPROJ_CHUNK)
    chunks = [slice(r0, r0 + ck) for r0 in range(0, tm, ck)]
    hbs = []
    for rows in chunks:
        x = x_ref[rows, :]
        if combine:
            x = x + mod_ref[5:6, :] * xb_ref[rows, :]
            xo_ref[rows, :] = x
        hbs.append(_rms_mod(x, g_ref[...], mod_ref[0:1, :], mod_ref[1:2, :]).astype(BF16))

    def qk(rows, hb, lo, width, gain_ref, out_ref, scale):
        u = _dot(hb, w_ref[:, lo:lo + width])
        r = _head_norm(u, gmat_ref, width)
        un = u * r * gain_ref[0:1, :]
        if rope:
            even = lax.broadcasted_iota(jnp.int32, (un.shape[0], LANES), 1) % 2 == 0
            parts = []
            for j in range(width // LANES):
                s = un[:, j * LANES:(j + 1) * LANES]
                sw = jnp.where(even, pltpu.roll(s, LANES - 1, 1), pltpu.roll(s, 1, 1))
                parts.append(s * cos_ref[rows, :] + sw * sin_ref[rows, :])
            un = jnp.concatenate(parts, axis=1) if len(parts) > 1 else parts[0]
        out_ref[rows, :] = (un * scale).astype(out_ref.dtype)

    for rows, hb in zip(chunks, hbs):
        qk(rows, hb, A_Q, ATT_Q_W, qg_ref, q_ref, HEAD_DIM ** -0.5 * math.log2(math.e))
        qk(rows, hb, A_K, ATT_KV_W, kg_ref, k_ref, 1.0)
        v_ref[rows, :] = _dot(hb, w_ref[:, A_V:A_HY]).astype(v_ref.dtype)
        hy_ref[rows, :] = _dot(hb, w_ref[:, A_HY:A_FN]).astype(hy_ref.dtype)
        fn_ref[rows, :] = _dot(hb, w_ref[:, A_FN:A_END]).astype(fn_ref.dtype)


def _inproj(x, xb, mod, norm_g, w_in, layer, cos_t, sin_t, qg, kg, gmat, n, tm):
    m, d = x.shape
    rope = cos_t is not None
    combine = xb is not None
    per_b = mod.shape[0] > 1
    tpb = n // tm
    row = lambda i: (i, 0)
    const = lambda i: (0, 0)
    mod_map = (lambda i: (i // tpb, 0, 0)) if per_b else (lambda i: (0, 0, 0))
    args, specs = [x], [pl.BlockSpec((tm, d), row)]
    if combine:
        args.append(xb)
        specs.append(pl.BlockSpec((tm, d), row))
    args += [mod, norm_g, w_in]
    specs += [pl.BlockSpec((None, 8, d), mod_map), _layer_spec(norm_g, layer),
              pl.BlockSpec((None, d, A_END), lambda i: (layer, 0, 0))]
    if rope:
        args += [cos_t, sin_t]
        specs += [pl.BlockSpec((tm, LANES), lambda i: (i % tpb, 0)),
                  pl.BlockSpec((tm, LANES), lambda i: (i % tpb, 0))]
    args += [qg, kg, gmat]
    specs += [_layer_spec(qg, layer), _layer_spec(kg, layer), pl.BlockSpec(gmat.shape, const)]
    widths = [ATT_Q_W, ATT_KV_W, ATT_KV_W, (HY_ORDER + 1) * HY_W, FN_W]
    out_shape = [jax.ShapeDtypeStruct((m, w), BF16) for w in widths]
    out_specs = [pl.BlockSpec((tm, w), row) for w in widths]
    if combine:
        out_shape.insert(0, jax.ShapeDtypeStruct((m, d), F32))
        out_specs.insert(0, pl.BlockSpec((tm, d), row))
    return pl.pallas_call(
        functools.partial(_inproj_kernel, rope=rope, combine=combine),
        grid=(m // tm,), in_specs=specs, out_specs=out_specs, out_shape=out_shape,
        compiler_params=_cparams(("arbitrary",)), name="inproj",
    )(*args)


def _attn_kernel(q_ref, kt_ref, v_ref, o_ref):
    tq = q_ref.shape[0]
    hpc = ATTN_HEADS_PER_CHAIN
    chains = [(h0 // Q_GROUP, range(h0, h0 + hpc)) for h0 in range(0, N_Q_HEADS, hpc)]
    ss = []
    for g, heads in chains:
        qg = jnp.concatenate([q_ref[:, h * HEAD_DIM:(h + 1) * HEAD_DIM] for h in heads], axis=0)
        ss.append(_dot(qg, kt_ref[g * HEAD_DIM:(g + 1) * HEAD_DIM, :]))
    outs = []
    for (g, heads), s in zip(chains, ss):
        p = jnp.exp2(s - jnp.max(s, axis=-1, keepdims=True)).astype(BF16)
        o = _dot(p, v_ref[g])
        for j in range(hpc):
            oh = o[j * tq:(j + 1) * tq, :]
            outs.append(oh[:, 0:HEAD_DIM] / oh[:, HEAD_DIM:HEAD_DIM + 1])
    o_ref[...] = jnp.concatenate(outs, axis=1).astype(o_ref.dtype)


def _attention(q, kt, vaug, n, tq):
    m = q.shape[0]
    b, _, s = kt.shape
    tpb = n // tq
    return pl.pallas_call(
        _attn_kernel,
        grid=(b, tpb),
        in_specs=[pl.BlockSpec((tq, ATT_Q_W), lambda bi, i: (bi * tpb + i, 0)),
                  pl.BlockSpec((None, ATT_KV_W, s), lambda bi, i: (bi, 0, 0)),
                  pl.BlockSpec((None, N_KV_HEADS, s, LANES), lambda bi, i: (bi, 0, 0, 0))],
        out_specs=pl.BlockSpec((tq, ATT_Q_W), lambda bi, i: (bi * tpb + i, 0)),
        out_shape=jax.ShapeDtypeStruct((m, ATT_Q_W), BF16),
        compiler_params=_cparams(("arbitrary", "arbitrary")), name="attention",
    )(q, kt, vaug)


def _hyena_filter_kernel(feat_ref, w1_ref, b1_ref, w2_ref, b2_ref, w3_ref, freq_ref, delta_ref,
                         ce_ref, se_ref, co_ref, so_ref, hra_ref, hsa_ref, hrb_ref, hsb_ref, hm_ref):
    n = feat_ref.shape[0]
    half = n // 2
    hp = lax.Precision.HIGHEST
    freq = freq_ref[...]
    h = jnp.sin(freq * (jnp.dot(feat_ref[...], w1_ref[...], preferred_element_type=F32, precision=hp)
                        + b1_ref[...]))
    h = jnp.sin(freq * (jnp.dot(h, w2_ref[...], preferred_element_type=F32, precision=hp) + b2_ref[...]))
    h = jnp.dot(h, w3_ref[...], preferred_element_type=F32, precision=hp)
    w2o = HY_ORDER * HY_W
    row = lax.broadcasted_iota(jnp.int32, (n, w2o), 0)
    pos = jnp.where(row < half, 2 * row, 2 * (row - half) + 1)
    t = pos.astype(F32) / (n - 1)
    decay = jnp.exp(-t * delta_ref[...])
    hf = h[:, 0:w2o] * decay
    hb = jnp.where(pos == 0, 0.0, h[:, w2o:2 * w2o] * decay)
    inv = 1.0 / (jnp.sum(jnp.abs(hf), axis=0, keepdims=True) + jnp.sum(jnp.abs(hb), axis=0, keepdims=True))
    hsum = (hf + hb) * inv
    hdif = (hf - hb) * inv
    sum_e, sum_o = _split_bf16(hsum[0:half, :]), _split_bf16(hsum[half:n, :])
    dif_e, dif_o = _split_bf16(hdif[0:half, :]), _split_bf16(hdif[half:n, :])

    def dot2(m_ref, r, parts):
        return _dot(m_ref[r, :], parts[0]) + _dot(m_ref[r, :], parts[1])

    ck = min(half, HY_CHUNK)
    for r0 in range(0, half, ck):
        r = slice(r0, r0 + ck)
        ec, oc = dot2(ce_ref, r, sum_e), dot2(co_ref, r, sum_o)
        es, os_ = dot2(se_ref, r, dif_e), dot2(so_ref, r, dif_o)
        k = lax.broadcasted_iota(jnp.int32, (ck, w2o), 0) + r0
        wk = jnp.where(k == 0, 1.0, 2.0) * (1.0 / (2 * n))
        hra_ref[r, :] = (ec + oc) * wk
        hsa_ref[r, :] = (es + os_) * wk
        hrb_ref[r, :] = (ec - oc) * wk
        hsb_ref[r, :] = (os_ - es) * wk
    alt = jnp.where(lax.broadcasted_iota(jnp.int32, (half, w2o), 0) % 2 == 0, 1.0, -1.0)
    hm_ref[0:1, :] = jnp.sum(hsum[0:half, :] * alt, axis=0, keepdims=True) * (1.0 / n)
    hm_ref[1:2, :] = jnp.sum(hdif[half:n, :] * alt, axis=0, keepdims=True) * (1.0 / n)


def _hyena_filter(feats, w1, b1, w2, b2, w3, freq, layer, deltas, dft):
    n = feats.shape[0]
    half = n // 2
    w2o = HY_ORDER * HY_W
    args = (feats, w1, b1, w2, b2, w3, freq, deltas) + tuple(dft)
    whole = lambda a: pl.BlockSpec(a.shape, lambda i: (0, 0))
    return pl.pallas_call(
        _hyena_filter_kernel,
        grid=(1,),
        in_specs=[whole(feats)] + [_layer_spec(a, layer) for a in args[1:7]] + [whole(a) for a in args[7:]],
        out_specs=[pl.BlockSpec((half, w2o), lambda i: (0, 0))] * 4 + [pl.BlockSpec((2, w2o), lambda i: (0, 0))],
        out_shape=[jax.ShapeDtypeStruct((half, w2o), F32)] * 4 + [jax.ShapeDtypeStruct((2, w2o), F32)],
        compiler_params=_cparams(("arbitrary",)), name="hyena_filter",
    )(*args)


def _hyena_kernel(u_ref, sw_ref, sb_ref, hra_ref, hsa_ref, hrb_ref, hsb_ref, hm_ref, d_ref,
                  ce_ref, se_ref, co_ref, so_ref, cot_ref, sot_ref, o_ref,
                  zfe_ref, zfo_ref, zbe_ref, zbo_ref, g1e_ref, g1o_ref, g2e_ref, g2o_ref,
                  p_ref, q_ref, p2_ref, q2_ref):
    half = u_ref.shape[0]
    w3 = (HY_ORDER + 1) * HY_W
    ck = min(half, HY_CHUNK)
    chunks = [slice(r, r + ck) for r in range(0, half, ck)]
    row = lax.broadcasted_iota(jnp.int32, (half, HY_W), 0)
    alt = jnp.where(row % 2 == 0, 1.0, -1.0)
    calt = jnp.where(lax.broadcasted_iota(jnp.int32, (ck, HY_W), 0) % 2 == 0, 1.0, -1.0)

    def short(j):
        c = slice(j * HY_W, (j + 1) * HY_W)
        ue = u_ref[:, j * HY_W:(j + 1) * HY_W].astype(F32)
        uo = u_ref[:, w3 + j * HY_W:w3 + (j + 1) * HY_W].astype(F32)
        uo_prev = jnp.where(row == 0, 0.0, pltpu.roll(uo, 1, 0))
        ue_next = jnp.where(row == half - 1, 0.0, pltpu.roll(ue, half - 1, 0))
        w0, w1, w2, b = sw_ref[0:1, c], sw_ref[1:2, c], sw_ref[2:3, c], sb_ref[0:1, c]
        return b + uo_prev * w0 + ue * w1 + uo * w2, b + ue * w0 + uo * w1 + ue_next * w2

    ve, vo = short(0)
    zfe_ref[...], zfo_ref[...] = ve, vo
    zbe_ref[...], zbo_ref[...] = ve.astype(BF16), vo.astype(BF16)
    g1e_ref[...], g1o_ref[...] = short(1)
    g2e_ref[...], g2o_ref[...] = short(2)
    gates = ((g1e_ref, g1o_ref), (g2e_ref, g2o_ref))

    for o in range(HY_ORDER):
        c = slice(o * HY_W, (o + 1) * HY_W)
        xr_m = jnp.sum(zfe_ref[...] * alt, axis=0, keepdims=True)
        xs_m = jnp.sum(zfo_ref[...] * alt, axis=0, keepdims=True)
        hr_m, hs_m = hm_ref[0:1, c], hm_ref[1:2, c]
        yr_m, ys_m = xr_m * hr_m - xs_m * hs_m, xr_m * hs_m + xs_m * hr_m
        for r in chunks:
            ec, es = _dot(ce_ref[r, :], zbe_ref[...]), _dot(se_ref[r, :], zbe_ref[...])
            oc, os_ = _dot(co_ref[r, :], zbo_ref[...]), _dot(so_ref[r, :], zbo_ref[...])
            xra, xsa, xrb, xsb = ec + oc, es + os_, ec - oc, os_ - es
            hra, hsa, hrb, hsb = hra_ref[r, c], hsa_ref[r, c], hrb_ref[r, c], hsb_ref[r, c]
            yra, ysa = xra * hra - xsa * hsa, xra * hsa + xsa * hra
            yrb, ysb = xrb * hrb - xsb * hsb, xrb * hsb + xsb * hrb
            p_ref[r, :] = (yra + yrb).astype(BF16)
            q_ref[r, :] = (ysa - ysb).astype(BF16)
            p2_ref[r, :] = (yra - yrb).astype(BF16)
            q2_ref[r, :] = (ysa + ysb).astype(BF16)
        ge_ref, go_ref = gates[o]
        for r in chunks:
            ye = _dot(ce_ref[r, :], p_ref[...]) + _dot(se_ref[r, :], q_ref[...]) + calt * yr_m
            yo = _dot(cot_ref[r, :], p2_ref[...]) + _dot(sot_ref[r, :], q2_ref[...]) + calt * ys_m
            ye = (ye + zfe_ref[r, :] * d_ref[o:o + 1, :]) * ge_ref[r, :]
            yo = (yo + zfo_ref[r, :] * d_ref[o:o + 1, :]) * go_ref[r, :]
            if o + 1 < HY_ORDER:
                zfe_ref[r, :], zfo_ref[r, :] = ye, yo
                zbe_ref[r, :], zbo_ref[r, :] = ye.astype(BF16), yo.astype(BF16)
            else:
                o_ref[r, 0:HY_W] = ye.astype(o_ref.dtype)
                o_ref[r, HY_W:2 * HY_W] = yo.astype(o_ref.dtype)


def _hyena(hy, sw, sb, spec, dbias, layer, dft, n):
    m, w3 = hy.shape
    half = n // 2
    const = lambda b: (0, 0)
    once = lambda a: pl.BlockSpec(a.shape, const, pipeline_mode=pl.Buffered(1))
    f32s = pltpu.VMEM((half, HY_W), F32)
    bf16s = pltpu.VMEM((half, HY_W), BF16)
    out = pl.pallas_call(
        _hyena_kernel,
        grid=(m // n,),
        in_specs=[pl.BlockSpec((half, 2 * w3), lambda b: (b, 0)),
                  _layer_spec(sw, layer), _layer_spec(sb, layer)]
                 + [once(a) for a in spec[:4]] + [pl.BlockSpec(spec[4].shape, const), _layer_spec(dbias, layer)]
                 + [once(a) for a in dft],
        out_specs=pl.BlockSpec((half, 2 * HY_W), lambda b: (b, 0)),
        out_shape=jax.ShapeDtypeStruct((m // 2, 2 * HY_W), BF16),
        scratch_shapes=[f32s, f32s, bf16s, bf16s, f32s, f32s, f32s, f32s, bf16s, bf16s, bf16s, bf16s],
        compiler_params=_cparams(("arbitrary",)), name="hyena",
    )(hy.reshape(m // 2, 2 * w3), sw, sb, *spec, dbias, *dft)
    return out.reshape(m, HY_W)


def _fnet_kernel(u_ref, bc_ref, bs_ref, ce_ref, se_ref, co_ref, so_ref, o_ref):
    half = u_ref.shape[0]
    scale = 1.0 / math.sqrt(2 * half * FN_GROUP_W)
    ue, uo = u_ref[:, 0:FN_W], u_ref[:, FN_W:2 * FN_W]
    uce, use = _dot(ue, bc_ref[...]).astype(BF16), _dot(ue, bs_ref[...]).astype(BF16)
    uco, uso = _dot(uo, bc_ref[...]).astype(BF16), _dot(uo, bs_ref[...]).astype(BF16)
    ck = min(half, HY_CHUNK)
    for r0 in range(0, half, ck):
        r = slice(r0, r0 + ck)
        e = _dot(ce_ref[r, :], uce) - _dot(se_ref[r, :], use)
        o = _dot(co_ref[r, :], uco) - _dot(so_ref[r, :], uso)
        o_ref[r0:r0 + ck, :] = ((e + o) * scale).astype(o_ref.dtype)
        o_ref[half + r0:half + r0 + ck, :] = ((e - o) * scale).astype(o_ref.dtype)


def _fnet(fn, bc, bs, dft, n):
    m = fn.shape[0]
    half = n // 2
    const = lambda b: (0, 0)
    once = lambda a: pl.BlockSpec(a.shape, const, pipeline_mode=pl.Buffered(1))
    return pl.pallas_call(
        _fnet_kernel,
        grid=(m // n,),
        in_specs=[pl.BlockSpec((half, 2 * FN_W), lambda b: (b, 0)), pl.BlockSpec(bc.shape, const),
                  pl.BlockSpec(bs.shape, const)] + [once(a) for a in dft],
        out_specs=pl.BlockSpec((n, FN_W), lambda b: (b, 0)),
        out_shape=jax.ShapeDtypeStruct((m, FN_W), BF16),
        compiler_params=_cparams(("arbitrary",)), name="fnet",
    )(fn.reshape(m // 2, 2 * FN_W), bc, bs, *dft)


def _merge_kernel(x_ref, mod_ref, g1_ref, g2_ref, att_ref, hy_ref, fn_ref, wg_ref, wb_ref, wo_ref,
                  wrh_ref, wrl_ref, x1_ref, h2_ref, lg_ref):
    tm, d = x_ref.shape
    branches = ((att_ref, 0, ATT_Q_W), (hy_ref, ATT_Q_W, ATT_Q_W + HY_W), (fn_ref, ATT_Q_W + HY_W, MIX_W))
    rows = [slice(r, r + MERGE_CHUNK) for r in range(0, tm, MERGE_CHUNK)]
    xs = [x_ref[r, :] for r in rows]
    hbs = [_rms_mod(x, g1_ref[...], mod_ref[0:1, :], mod_ref[1:2, :]).astype(BF16) for x in xs]
    mixes = []
    for r, hb in zip(rows, hbs):
        mix = None
        for i, (b_ref, lo, hi) in enumerate(branches):
            gate = _dot(hb, wg_ref[:, OFF_GATE + i * d:OFF_GATE + (i + 1) * d])
            gate = 1.0 / (1.0 + jnp.exp(-gate))
            term = gate * _dot(b_ref[r, :], wb_ref[lo:hi, :])
            mix = term if mix is None else mix + term
        mixes.append(mix.astype(BF16))
    for r, x, mix in zip(rows, xs, mixes):
        y = _dot(mix, wo_ref[...])
        x1 = x + mod_ref[2:3, :] * y
        x1_ref[r, :] = x1
        h2 = _rms_mod(x1, g2_ref[...], mod_ref[3:4, :], mod_ref[4:5, :])
        hi, lo = _split_bf16(h2)
        h2_ref[r, :] = hi
        lg_ref[r, :] = _dot(hi, wrh_ref[...]) + _dot(lo, wrh_ref[...]) + _dot(hi, wrl_ref[...])


def _merge(x, mod, g1, g2, att, hyo, fno, wg, layer, wb, wo, wrh, wrl, n, tm):
    m, d = x.shape
    per_b = mod.shape[0] > 1
    tpb = n // tm
    row = lambda i: (i, 0)
    const = lambda i: (0, 0)
    mod_map = (lambda i: (i // tpb, 0, 0)) if per_b else (lambda i: (0, 0, 0))
    return pl.pallas_call(
        _merge_kernel,
        grid=(m // tm,),
        in_specs=[pl.BlockSpec((tm, d), row), pl.BlockSpec((None, 8, d), mod_map),
                  _layer_spec(g1, layer), _layer_spec(g2, layer),
                  pl.BlockSpec((tm, ATT_Q_W), row), pl.BlockSpec((tm, HY_W), row), pl.BlockSpec((tm, FN_W), row),
                  _layer_spec(wg, layer, pipeline_mode=pl.Buffered(1)),
                  _layer_spec(wb, layer), _layer_spec(wo, layer),
                  _layer_spec(wrh, layer), _layer_spec(wrl, layer)],
        out_specs=[pl.BlockSpec((tm, d), row), pl.BlockSpec((tm, d), row), pl.BlockSpec((tm, LANES), row)],
        out_shape=[jax.ShapeDtypeStruct((m, d), F32), jax.ShapeDtypeStruct((m, d), BF16),
                   jax.ShapeDtypeStruct((m, LANES), F32)],
        compiler_params=_cparams(("arbitrary",)), name="merge",
    )(x, mod, g1, g2, att, hyo, fno, wg, wb, wo, wrh, wrl)


def _route_kernel(lg_ref, tri_ref, rank_ref, aff_ref, rankt_ref, *, cap):
    ns, _, n = rank_ref.shape
    rows = ns * N_EXPERTS
    affs = []
    for s in range(ns):
        lt = lg_ref[s * n:(s + 1) * n, :].T[0:N_EXPERTS, :]
        e = jnp.exp(lt - jnp.max(lt, axis=0, keepdims=True))
        affs.append(e / jnp.sum(e, axis=0, keepdims=True))
    aff = jnp.concatenate(affs, axis=0) if ns > 1 else affs[0]
    bits = pltpu.bitcast(aff, jnp.int32)

    def step(i, thr):
        cand = thr | jnp.left_shift(jnp.int32(1), 30 - i)
        cnt = jnp.sum(jnp.where(bits >= cand, 1.0, 0.0), axis=1, keepdims=True)
        return jnp.where(cnt >= cap, cand, thr)

    thr = lax.fori_loop(0, 31, step, jnp.zeros((rows, 1), jnp.int32))
    gt = bits > thr
    eq = bits == thr
    need = cap - jnp.sum(jnp.where(gt, 1.0, 0.0), axis=1, keepdims=True)

    def excl_cumsum(mask):
        parts = []
        off = jnp.zeros((rows, 1), F32)
        for c in range(n // LANES):
            blk = mask[:, c * LANES:(c + 1) * LANES]
            parts.append(_dot(blk.astype(BF16), tri_ref[...]) + off)
            off = off + jnp.sum(blk, axis=1, keepdims=True)
        return jnp.concatenate(parts, axis=1)

    tie_rank = excl_cumsum(jnp.where(eq, 1.0, 0.0))
    sel = jnp.where(gt, 1.0, jnp.where(eq, jnp.where(tie_rank < need, 1.0, 0.0), 0.0))
    rank = jnp.where(sel > 0.5, excl_cumsum(sel), -1.0)
    pad = jnp.full((LANES - N_EXPERTS, n), -1.0, F32)
    for s in range(ns):
        r = rank[s * N_EXPERTS:(s + 1) * N_EXPERTS, :]
        rank_ref[s] = r.astype(jnp.int32)
        aff_ref[s] = affs[s]
        rankt_ref[s * n:(s + 1) * n, :] = jnp.concatenate([r, pad], axis=0).T.astype(BF16)


def _route(logits, tri, n, cap):
    m = logits.shape[0]
    b = m // n
    assert cap <= 256
    ns = math.gcd(b, ROUTE_SAMPLES)
    return pl.pallas_call(
        functools.partial(_route_kernel, cap=cap),
        grid=(b // ns,),
        in_specs=[pl.BlockSpec((ns * n, LANES), lambda i: (i, 0)), pl.BlockSpec(tri.shape, lambda i: (0, 0))],
        out_specs=[pl.BlockSpec((ns, N_EXPERTS, n), lambda i: (i, 0, 0)),
                   pl.BlockSpec((ns, N_EXPERTS, n), lambda i: (i, 0, 0)),
                   pl.BlockSpec((ns * n, LANES), lambda i: (i, 0))],
        out_shape=[jax.ShapeDtypeStruct((b, N_EXPERTS, n), jnp.int32),
                   jax.ShapeDtypeStruct((b, N_EXPERTS, n), F32),
                   jax.ShapeDtypeStruct((m, LANES), BF16)],
        compiler_params=_cparams(("arbitrary",)), name="route",
    )(logits, tri)


def _gather_kernel(h_ref, rank_ref, aff_ref, xg_ref, w_ref, sel_ref, *, cap):
    n, d = h_ref.shape
    slot = lax.broadcasted_iota(jnp.int32, (cap, n), 0)
    for e in range(N_EXPERTS):
        hit = slot == rank_ref[e:e + 1, :]
        sel_ref[e * cap:(e + 1) * cap, :] = jnp.where(hit, 1.0, 0.0).astype(BF16)
        wcol = jnp.sum(jnp.where(hit, aff_ref[e:e + 1, :], 0.0), axis=1, keepdims=True)
        w_ref[e] = jnp.broadcast_to(wcol, (cap, LANES))
    tn = 256
    for c in range(d // tn):
        xg = _dot(sel_ref[...], h_ref[:, c * tn:(c + 1) * tn]).astype(BF16)
        for e in range(N_EXPERTS):
            xg_ref[e, :, c * tn:(c + 1) * tn] = xg[e * cap:(e + 1) * cap, :]


def _gather(h2, rank, aff, n, cap):
    m, d = h2.shape
    b = m // n
    return pl.pallas_call(
        functools.partial(_gather_kernel, cap=cap),
        grid=(b,),
        in_specs=[pl.BlockSpec((n, d), lambda i: (i, 0)),
                  pl.BlockSpec((None, N_EXPERTS, n), lambda i: (i, 0, 0)),
                  pl.BlockSpec((None, N_EXPERTS, n), lambda i: (i, 0, 0))],
        out_specs=[pl.BlockSpec((N_EXPERTS, None, cap, d), lambda i: (0, i, 0, 0)),
                   pl.BlockSpec((N_EXPERTS, None, cap, LANES), lambda i: (0, i, 0, 0))],
        out_shape=[jax.ShapeDtypeStruct((N_EXPERTS, b, cap, d), BF16),
                   jax.ShapeDtypeStruct((N_EXPERTS, b, cap, LANES), F32)],
        scratch_shapes=[pltpu.VMEM((N_EXPERTS * cap, n), BF16)],
        compiler_params=_cparams(("arbitrary",)), name="moe_gather",
    )(h2, rank, aff)


def _ffn_kernel(*refs, extra):
    if extra:
        x_ref, w_ref, xc_ref, wc_ref, wg_ref, wu_ref, wd_ref, o_ref, oc_ref, acc_ref, accc_ref = refs
    else:
        x_ref, w_ref, wg_ref, wu_ref, wd_ref, o_ref, acc_ref = refs
    f = pl.program_id(2)

    def run(x_ref, w_ref, o_ref, acc_ref):
        @pl.when((pl.program_id(0) == 0) & (pl.program_id(1) == 0) & (f == 0))
        def _():
            acc_ref[...] = jnp.zeros_like(acc_ref)

        x = x_ref[...]
        d = acc_ref.shape[1]
        tf = 512
        hms, wds = [], []
        for c in range(wg_ref.shape[1] // tf):
            cols = slice(c * tf, (c + 1) * tf)
            a = _dot(x, wg_ref[:, cols].astype(BF16))
            u = _dot(x, wu_ref[:, cols].astype(BF16))
            hms.append((a * (1.0 / (1.0 + jnp.exp(-a))) * u).astype(BF16))
            wds.append(wd_ref[cols, :].astype(BF16))
        tn = 256
        for c in range(d // tn):
            cols = slice(c * tn, (c + 1) * tn)
            t = sum(_dot(hm, wd[:, cols]) for hm, wd in zip(hms, wds))
            s = jnp.where(f > 0, acc_ref[:, cols], 0.0) + t
            acc_ref[:, cols] = s
            o_ref[:, cols] = (s * w_ref[...] if tn == LANES else
                              s * jnp.concatenate([w_ref[...]] * (tn // LANES), axis=1)).astype(o_ref.dtype)

    run(x_ref, w_ref, o_ref, acc_ref)
    if extra:
        @pl.when(pl.program_id(1) == 0)
        def _():
            run(xc_ref, wc_ref, oc_ref, accc_ref)


def _ffn(xg, w, xgc, wc, wg, wu, wd, layer, tm, tf):
    ne, rows, d = xg.shape
    ff = wg.shape[3]
    extra = xgc is not None
    tile = lambda e, i, f: (e, i, 0)
    whole = lambda e, i, f: (e, 0, 0)
    args = [xg, w]
    in_specs = [pl.BlockSpec((None, tm, d), tile), pl.BlockSpec((None, tm, LANES), tile)]
    out_specs = [pl.BlockSpec((None, tm, d), tile)]
    out_shape = [jax.ShapeDtypeStruct((ne, rows, d), BF16)]
    scratch = [pltpu.VMEM((tm, d), F32)]
    if extra:
        rc = xgc.shape[1]
        args += [xgc, wc]
        in_specs += [pl.BlockSpec((None, rc, d), whole), pl.BlockSpec((None, rc, LANES), whole)]
        out_specs.append(pl.BlockSpec((None, rc, d), whole))
        out_shape.append(jax.ShapeDtypeStruct((ne, rc, d), BF16))
        scratch.append(pltpu.VMEM((rc, d), F32))
    args += [wg, wu, wd]
    in_specs += [pl.BlockSpec((None, None, d, tf), lambda e, i, f: (layer, e, 0, f)),
                 pl.BlockSpec((None, None, d, tf), lambda e, i, f: (layer, e, 0, f)),
                 pl.BlockSpec((None, None, tf, d), lambda e, i, f: (layer, e, f, 0))]
    outs = pl.pallas_call(
        functools.partial(_ffn_kernel, extra=extra),
        grid=(ne, rows // tm, ff // tf),
        in_specs=in_specs, out_specs=out_specs, out_shape=out_shape, scratch_shapes=scratch,
        compiler_params=_cparams(("arbitrary", "arbitrary", "arbitrary")), name="moe_ffn",
    )(*args)
    return (outs[0], outs[1]) if extra else (outs[0], None)


def _scatter_kernel(rankt_ref, y_ref, expand_ref, slot_ref, o_ref, selt_ref, *, cap):
    n = rankt_ref.shape[0]
    d = o_ref.shape[1]
    width = N_EXPERTS * cap
    if cap % LANES == 0:
        rank = rankt_ref[...].astype(F32)
        slot = lax.broadcasted_iota(jnp.int32, (n, cap), 1).astype(F32)
        for e in range(N_EXPERTS):
            r = jnp.broadcast_to(rank[:, e:e + 1], (n, cap))
            selt_ref[:, e * cap:(e + 1) * cap] = jnp.where(r == slot, 1.0, 0.0).astype(BF16)
    else:
        r = _dot(rankt_ref[...], expand_ref[...])
        selt_ref[...] = jnp.where(r == slot_ref[...], 1.0, 0.0).astype(BF16)
    y = y_ref[...].reshape(width, d)
    tn = 256
    for c in range(d // tn):
        o_ref[:, c * tn:(c + 1) * tn] = _dot(selt_ref[...], y[:, c * tn:(c + 1) * tn])


def _scatter(rankt, yw, expand, slotpat, n, cap):
    ne, b, _, d = yw.shape
    return pl.pallas_call(
        functools.partial(_scatter_kernel, cap=cap),
        grid=(b,),
        in_specs=[pl.BlockSpec((n, LANES), lambda i: (i, 0)),
                  pl.BlockSpec((ne, None, cap, d), lambda i: (0, i, 0, 0)),
                  pl.BlockSpec(expand.shape, lambda i: (0, 0)),
                  pl.BlockSpec(slotpat.shape, lambda i: (0, 0))],
        out_specs=pl.BlockSpec((n, d), lambda i: (i, 0)),
        out_shape=jax.ShapeDtypeStruct((b * n, d), F32),
        scratch_shapes=[pltpu.VMEM((n, ne * cap), BF16)],
        compiler_params=_cparams(("arbitrary",)), name="moe_scatter",
    )(rankt, yw, expand, slotpat)


def _moe(routed, routed_extra, wg, wu, wd, layer):
    def flat(r):
        rows = r["b"] * r["cap"]
        return r["xg"].reshape(N_EXPERTS, rows, D_MODEL), r["w"].reshape(N_EXPERTS, rows, LANES)

    def scatter(r, yw):
        yw = yw.reshape(N_EXPERTS, r["b"], r["cap"], D_MODEL)
        return _scatter(r["rankt"], yw, r["tabs"]["expand"], r["tabs"]["slotpat"], r["n"], r["cap"])

    xg, w = flat(routed)
    xgc, wc = flat(routed_extra) if routed_extra is not None else (None, None)
    yw, ywc = _ffn(xg, w, xgc, wc, wg, wu, wd, layer, min(xg.shape[1], FFN_ROW_TILE), FFN_HIDDEN_TILE)
    return scatter(routed, yw), (scatter(routed_extra, ywc) if routed_extra is not None else None)


def _combine_kernel(x_ref, y_ref, mod_ref, g_ref, o_ref, *, norm):
    x = x_ref[...] + mod_ref[5:6, :] * y_ref[...]
    if norm:
        x = x * lax.rsqrt(jnp.mean(x * x, axis=-1, keepdims=True) + EPS) * g_ref[...]
    o_ref[...] = x


def _combine(x, y, mod, g, n, tm, norm):
    m, d = x.shape
    per_b = mod.shape[0] > 1
    tpb = n // tm
    row = lambda i: (i, 0)
    mod_map = (lambda i: (i // tpb, 0, 0)) if per_b else (lambda i: (0, 0, 0))
    return pl.pallas_call(
        functools.partial(_combine_kernel, norm=norm),
        grid=(m // tm,),
        in_specs=[pl.BlockSpec((tm, d), row), pl.BlockSpec((tm, d), row),
                  pl.BlockSpec((None, 8, d), mod_map), pl.BlockSpec((1, d), lambda i: (0, 0))],
        out_specs=pl.BlockSpec((tm, d), row),
        out_shape=jax.ShapeDtypeStruct((m, d), F32),
        compiler_params=_cparams(("arbitrary",)), name="combine",
    )(x, y, mod, g)


def _fnet_dft_tables(n):
    j = jnp.arange(n // 2, dtype=jnp.int32)
    even = ((j[:, None] * (2 * j[None, :])) % n).astype(F32) * (2.0 * math.pi / n)
    odd = ((j[:, None] * (2 * j[None, :] + 1)) % n).astype(F32) * (2.0 * math.pi / n)
    return tuple(a.astype(BF16) for a in (jnp.cos(even), jnp.sin(even), jnp.cos(odd), jnp.sin(odd)))


def _rope_tables(n):
    rows = n // GRID_W
    row = jnp.repeat(jnp.arange(rows, dtype=F32), GRID_W)
    col = jnp.tile(jnp.arange(GRID_W, dtype=F32), rows)
    inv = ROPE_THETA ** (-jnp.arange(0, ROPE_AXIS_DIM, 2, dtype=F32) / ROPE_AXIS_DIM)
    ang = jnp.concatenate([row[:, None] * inv, col[:, None] * inv], axis=-1)
    cos = jnp.repeat(jnp.cos(ang), 2, axis=1)
    sin = jnp.repeat(jnp.sin(ang), 2, axis=1) * jnp.tile(jnp.array([-1.0, 1.0], F32), HEAD_DIM // 2)
    return jnp.tile(cos, (1, LANES // HEAD_DIM)), jnp.tile(sin, (1, LANES // HEAD_DIM))


def _hyena_dft_tables(n):
    j = jnp.arange(n // 2, dtype=jnp.int32)
    even = ((j[:, None] * j[None, :]) % n).astype(F32) * (2.0 * math.pi / n)
    odd = ((j[:, None] * (2 * j[None, :] + 1)) % (2 * n)).astype(F32) * (math.pi / n)
    ce, se, co, so = jnp.cos(even), jnp.sin(even), jnp.cos(odd), jnp.sin(odd)
    return tuple(a.astype(BF16) for a in (ce, se, co, so, co.T, so.T))


def _hyena_feats(n):
    pos = jnp.concatenate([jnp.arange(0, n, 2), jnp.arange(1, n, 2)]).astype(F32)
    t = pos / (n - 1)
    bands = jnp.linspace(1e-4, HY_BANDS - 1, HY_BANDS, dtype=F32)
    ang = (2.0 * math.pi / n) * pos[:, None] * bands[None, :]
    feats = jnp.concatenate([t[:, None], jnp.cos(ang), -jnp.sin(ang)], axis=-1)
    return jnp.pad(feats, ((0, 0), (0, LANES - HY_EMB)))


def _side(x, xb, n, mod, l, P, tabs, kv_ext, rope, tm, last_ctx):
    b = x.shape[0] // n
    outs = _inproj(x, xb, mod, P["norm1_g"], P["w_in"], l, tabs["cos"] if rope else None, tabs["sin"] if rope else None,
                   P["qg"], P["kg"], P["gmat"], n, tm)
    if xb is not None:
        x, outs = outs[0], outs[1:]
    q, k, v, hy, fn = outs
    k3, v3 = k.reshape(b, n, ATT_KV_W), v.reshape(b, n, ATT_KV_W)
    if last_ctx:
        return None, None, k3, v3
    if kv_ext is not None:
        kc, vc = jnp.concatenate([k3, kv_ext[0]], axis=1), jnp.concatenate([v3, kv_ext[1]], axis=1)
    else:
        kc, vc = k3, v3
    s = kc.shape[1]
    fill = jnp.concatenate([jnp.ones((b, s, 1), BF16), jnp.zeros((b, s, LANES - HEAD_DIM - 1), BF16)], axis=-1)
    vaug = jnp.stack([jnp.concatenate([vc[..., g * HEAD_DIM:(g + 1) * HEAD_DIM], fill], axis=-1)
                      for g in range(N_KV_HEADS)], axis=1)
    att = _attention(q, jnp.swapaxes(kc, 1, 2), vaug, n, min(n, ATTN_Q_TILE))
    spec = _hyena_filter(tabs["feats"], P["hy_w1"], P["hy_b1"], P["hy_w2"], P["hy_b2"],
                         P["hy_w3"], P["hy_freq"], l, tabs["deltas"], tabs["hy_dft"][:4])
    hyo = _hyena(hy, P["hy_sw"], P["hy_sb"], spec, P["hy_bias"], l, tabs["hy_dft"], n)
    fno = _fnet(fn, tabs["bc"], tabs["bs"], tabs["fn_dft"], n)
    x1, h2, logits = _merge(x, mod, P["norm1_g"], P["norm2_g"], att, hyo, fno, P["w_in"], l,
                            P["w_branch"], P["w_out"], P["wr_hi"], P["wr_lo"], n, tm)
    cap = CAPACITY_FACTOR * n // N_EXPERTS
    rank, aff, rankt = _route(logits, tabs["tri"], n, cap)
    xg, w = _gather(h2, rank, aff, n, cap)
    return x1, dict(xg=xg, w=w, rankt=rankt, tabs=tabs, n=n, cap=cap, b=b), k3, v3


def _tables(n, rope):
    a = jnp.arange(FN_W, dtype=jnp.int32)
    same = (a[:, None] // FN_GROUP_W) == (a[None, :] // FN_GROUP_W)
    ang = ((a[:, None] * a[None, :]) % FN_GROUP_W).astype(F32) * (2.0 * math.pi / FN_GROUP_W)
    deltas = jnp.abs(jnp.linspace(math.log(HY_DECAY_TARGET) / HY_DECAY_LONG_PCT,
                                  math.log(HY_DECAY_TARGET) / HY_DECAY_SHORT_PCT, HY_W, dtype=F32))
    i = jnp.arange(LANES, dtype=jnp.int32)
    cap = CAPACITY_FACTOR * n // N_EXPERTS
    j = jnp.arange(N_EXPERTS * cap, dtype=jnp.int32)
    tabs = dict(expand=(i[:, None] == j[None, :] // cap).astype(BF16),
                slotpat=(j % cap).astype(F32)[None, :],
                hy_dft=_hyena_dft_tables(n), fn_dft=_fnet_dft_tables(n),
                bc=jnp.where(same, jnp.cos(ang), 0.0).astype(BF16),
                bs=jnp.where(same, jnp.sin(ang), 0.0).astype(BF16),
                feats=_hyena_feats(n), deltas=jnp.tile(deltas, HY_ORDER)[None, :],
                tri=(i[:, None] < i[None, :]).astype(BF16))
    if rope:
        tabs["cos"], tabs["sin"] = _rope_tables(n)
    return tabs


def kernel(x, c, ctx, c_ctx, w_mod, b_mod, norm1_g, norm2_g, w_in, q_gain, k_gain, hy_short_w, hy_short_b,
           hy_f_w1, hy_f_b1, hy_f_w2, hy_f_b2, hy_f_w3, hy_f_freq, hy_bias, w_branch, w_out, w_router,
           w_gate, w_up, w_down, final_g):
    bsz, n_lat, d = x.shape
    n_ctx = ctx.shape[1]
    depth = w_mod.shape[0]
    assert d == D_MODEL and n_lat % LANES == 0 and n_ctx % LANES == 0

    hid = jnp.arange(ATT_Q_W, dtype=jnp.int32) // HEAD_DIM
    P = dict(
        norm1_g=norm1_g[:, None, :], norm2_g=norm2_g[:, None, :],
        w_in=w_in.astype(BF16),
        qg=jnp.tile(q_gain, (1, N_Q_HEADS))[:, None, :], kg=jnp.tile(k_gain, (1, N_KV_HEADS))[:, None, :],
        gmat=(hid[:, None] == hid[None, :]).astype(BF16),
        hy_sw=hy_short_w, hy_sb=hy_short_b[:, None, :],
        hy_w1=jnp.pad(hy_f_w1, ((0, 0), (0, LANES - HY_EMB), (0, 0))), hy_b1=hy_f_b1[:, None, :],
        hy_w2=hy_f_w2, hy_b2=hy_f_b2[:, None, :], hy_w3=hy_f_w3, hy_freq=hy_f_freq[:, None, :],
        hy_bias=hy_bias,
        w_branch=w_branch.astype(BF16), w_out=w_out.astype(BF16),
        w_gate=w_gate, w_up=w_up, w_down=w_down,
    )
    wr = jnp.pad(w_router, ((0, 0), (0, 0), (0, LANES - N_EXPERTS)))
    P["wr_hi"] = wr.astype(BF16)
    P["wr_lo"] = (wr - P["wr_hi"].astype(F32)).astype(BF16)

    tab_x = _tables(n_lat, True)
    tab_c = _tables(n_ctx, False)

    rows = -(-(bsz + 1) // 8) * 8
    c_all = jnp.concatenate([c, c_ctx[None, :], jnp.zeros((rows - bsz - 1, d), F32)], axis=0)
    mod = _modulation(c_all, w_mod, b_mod).reshape(depth, rows, 6, d)
    mod = jnp.pad(mod, ((0, 0), (0, 0), (0, 2), (0, 0)))

    tm_x, tm_c = min(n_lat, ROW_TILE), min(n_ctx, ROW_TILE)
    xs, xpend = x.reshape(bsz * n_lat, d), None
    cs, cpend = ctx.reshape(bsz * n_ctx, d), None
    for l in range(depth):
        last = l == depth - 1
        mod_x, mod_c = mod[l, :bsz], mod[l, bsz:bsz + 1]
        pm_x = None if l == 0 else jnp.concatenate([mod_x[:, :5], mod[l - 1, :bsz, 5:6], mod_x[:, 6:]], axis=1)
        pm_c = None if l == 0 else jnp.concatenate([mod_c[:, :5], mod[l - 1, bsz:bsz + 1, 5:6], mod_c[:, 6:]],
                                                   axis=1)
        c1, rc, kc, vc = _side(cs, cpend, n_ctx, mod_c if l == 0 else pm_c, l, P, tab_c, None, False, tm_c, last)
        x1, rx, _, _ = _side(xs, xpend, n_lat, mod_x if l == 0 else pm_x, l, P, tab_x, (kc, vc), True, tm_x, False)
        xy, cy = _moe(rx, rc, P["w_gate"], P["w_up"], P["w_down"], l)
        xs, xpend = x1, xy
        if not last:
            cs, cpend = c1, cy
    out = _combine(xs, xpend, mod[depth - 1, :bsz], final_g[None, :], n_lat, tm_x, True)
    return out.reshape(bsz, n_lat, d)
```

```python
import functools
import math

import jax
import jax.numpy as jnp
from jax import lax
from jax.experimental import pallas as pl
from jax.experimental.pallas import tpu as pltpu

F32 = jnp.float32
BF16 = jnp.bfloat16

D_MODEL = 1024
GRID_W = 64
HEAD_DIM = 64
N_Q_HEADS = 8
N_KV_HEADS = 2
Q_GROUP = N_Q_HEADS // N_KV_HEADS
ATT_Q_W = N_Q_HEADS * HEAD_DIM
ATT_KV_W = N_KV_HEADS * HEAD_DIM
ROPE_THETA = 10000.0
ROPE_AXIS_DIM = HEAD_DIM // 2
HY_W = D_MODEL // 4
HY_ORDER = 2
HY_SHORT = 3
HY_BANDS = 16
HY_EMB = 2 * HY_BANDS + 1
HY_FFN = 64
HY_DECAY_TARGET = 1e-2
HY_DECAY_SHORT_PCT = 0.3
HY_DECAY_LONG_PCT = 1.5
FN_GROUPS = 4
FN_GROUP_W = D_MODEL // 16
FN_W = FN_GROUPS * FN_GROUP_W
MIX_W = ATT_Q_W + HY_W + FN_W
N_BRANCH = 3
OFF_Q = 0
OFF_K = OFF_Q + ATT_Q_W
OFF_V = OFF_K + ATT_KV_W
OFF_HY = OFF_V + ATT_KV_W
OFF_FN = OFF_HY + (HY_ORDER + 1) * HY_W
OFF_GATE = OFF_FN + FN_W
N_EXPERTS = 16
CAPACITY_FACTOR = 2
EPS = 1e-6

LANES = 128
VMEM_LIMIT = 56 * 1024 * 1024
ROW_TILE = 512
ATTN_Q_TILE = 256
FFN_ROW_TILE = 1024
FFN_HIDDEN_TILE = 1024
HY_CHUNK = 512
ROUTE_SAMPLES = 4
INPROJ_CHUNK = 256
MERGE_CHUNK = 256
ATTN_HEADS_PER_CHAIN = 4

A_Q, A_K, A_V, A_HY, A_FN, A_END = 0, 512, 640, 768, 1536, 1792


def _cparams(sem):
    return pltpu.CompilerParams(dimension_semantics=sem, vmem_limit_bytes=VMEM_LIMIT)


def _dot(a, b):
    return jnp.dot(a, b, preferred_element_type=F32)


def _split_bf16(x):
    hi = x.astype(BF16)
    lo = (x - hi.astype(F32)).astype(BF16)
    return hi, lo


def _layer_spec(arr, layer, **kw):
    zeros = (0,) * (arr.ndim - 1)
    return pl.BlockSpec((None,) + arr.shape[1:], lambda *_: (layer,) + zeros, **kw)


def _rms_mod(x, g, shift, scale):
    y = x * lax.rsqrt(jnp.mean(x * x, axis=-1, keepdims=True) + EPS)
    return (y * g) * (1.0 + scale) + shift


def _mod_kernel(c_ref, w_ref, b_ref, o_ref):
    c = c_ref[...]
    sc = c * (1.0 / (1.0 + jnp.exp(-c)))
    o_ref[...] = jnp.dot(sc, w_ref[...], preferred_element_type=F32,
                         precision=lax.Precision.HIGHEST) + b_ref[...]


def _modulation(c_all, w_mod, b_mod):
    depth, d, n6 = w_mod.shape
    rows = c_all.shape[0]
    tn = 1536
    return pl.pallas_call(
        _mod_kernel,
        grid=(depth, n6 // tn),
        in_specs=[pl.BlockSpec((rows, d), lambda l, j: (0, 0)),
                  pl.BlockSpec((None, d, tn), lambda l, j: (l, 0, j)),
                  pl.BlockSpec((None, 1, tn), lambda l, j: (l, 0, j))],
        out_specs=pl.BlockSpec((None, rows, tn), lambda l, j: (l, 0, j)),
        out_shape=jax.ShapeDtypeStruct((depth, rows, n6), F32),
        compiler_params=_cparams(("arbitrary", "arbitrary")),
        name="modulation",
    )(c_all, w_mod, b_mod.reshape(depth, 1, n6))


def _head_norm(u, gmat_ref, width):
    gm = gmat_ref[0:width, 0:width]
    ms = _dot((u * u).astype(BF16), gm) * (1.0 / HEAD_DIM)
    return lax.rsqrt(ms + EPS)


def _inproj_kernel(*refs, rope, combine):
    it = iter(refs)
    x_ref = next(it)
    if combine:
        xb_ref = next(it)
    mod_ref = next(it)
    g_ref = next(it)
    w_ref = next(it)
    if rope:
        cos_ref = next(it)
        sin_ref = next(it)
    qg_ref = next(it)
    kg_ref = next(it)
    gmat_ref = next(it)
    if combine:
        xo_ref = next(it)
    q_ref = next(it)
    k_ref = next(it)
    v_ref = next(it)
    hy_ref = next(it)
    fn_ref = next(it)
    hs_ref = next(it)

    tm = x_ref.shape[0]
    ck = min(tm, INPROJ_CHUNK)
    chunks = [slice(r0, r0 + ck) for r0 in range(0, tm, ck)]
    hbs, hps = [], []
    for rows in chunks:
        x = x_ref[rows, :]
        if combine:
            x = x + mod_ref[5:6, :] * xb_ref[rows, :]
            xo_ref[rows, :] = x
        h = _rms_mod(x, g_ref[...], mod_ref[0:1, :], mod_ref[1:2, :])
        hbs.append(h.astype(BF16))
        parts = []
        for j in range(h.shape[1] // LANES):
            hs_ref[j, rows, :] = h[:, j * LANES:(j + 1) * LANES]
            he = hs_ref[j, pl.ds(rows.start, ck // 2, stride=2), :]
            ho = hs_ref[j, pl.ds(rows.start + 1, ck // 2, stride=2), :]
            parts.append(jnp.concatenate([he, ho], axis=0))
        hps.append(jnp.concatenate(parts, axis=1).astype(BF16))

    def qk(rows, hb, lo, width, gain_ref, out_ref, scale):
        u = _dot(hb, w_ref[:, lo:lo + width])
        r = _head_norm(u, gmat_ref, width)
        un = u * r * gain_ref[0:1, :]
        if rope:
            even = lax.broadcasted_iota(jnp.int32, (un.shape[0], LANES), 1) % 2 == 0
            parts = []
            for j in range(width // LANES):
                s = un[:, j * LANES:(j + 1) * LANES]
                sw = jnp.where(even, pltpu.roll(s, LANES - 1, 1), pltpu.roll(s, 1, 1))
                parts.append(s * cos_ref[rows, :] + sw * sin_ref[rows, :])
            un = jnp.concatenate(parts, axis=1) if len(parts) > 1 else parts[0]
        out_ref[rows, :] = (un * scale).astype(out_ref.dtype)

    for rows, hb, hp in zip(chunks, hbs, hps):
        qk(rows, hb, A_Q, ATT_Q_W, qg_ref, q_ref, HEAD_DIM ** -0.5 * math.log2(math.e))
        qk(rows, hb, A_K, ATT_KV_W, kg_ref, k_ref, 1.0)
        v_ref[rows, :] = _dot(hb, w_ref[:, A_V:A_HY]).astype(v_ref.dtype)
        pairs = slice(rows.start // 2, rows.start // 2 + ck // 2)
        for out_ref, lo, hi in ((hy_ref, A_HY, A_FN), (fn_ref, A_FN, A_END)):
            u = _dot(hp, w_ref[:, lo:hi]).astype(out_ref.dtype)
            out_ref[pairs, 0:hi - lo] = u[0:ck // 2, :]
            out_ref[pairs, hi - lo:2 * (hi - lo)] = u[ck // 2:ck, :]


def _inproj(x, xb, mod, norm_g, w_in, layer, cos_t, sin_t, qg, kg, gmat, n, tm):
    m, d = x.shape
    rope = cos_t is not None
    combine = xb is not None
    per_b = mod.shape[0] > 1
    tpb = n // tm
    row = lambda i: (i, 0)
    const = lambda i: (0, 0)
    mod_map = (lambda i: (i // tpb, 0, 0)) if per_b else (lambda i: (0, 0, 0))
    args, specs = [x], [pl.BlockSpec((tm, d), row)]
    if combine:
        args.append(xb)
        specs.append(pl.BlockSpec((tm, d), row))
    args += [mod, norm_g, w_in]
    specs += [pl.BlockSpec((None, 8, d), mod_map), _layer_spec(norm_g, layer),
              pl.BlockSpec((None, d, A_END), lambda i: (layer, 0, 0))]
    if rope:
        args += [cos_t, sin_t]
        specs += [pl.BlockSpec((tm, LANES), lambda i: (i % tpb, 0)),
                  pl.BlockSpec((tm, LANES), lambda i: (i % tpb, 0))]
    args += [qg, kg, gmat]
    specs += [_layer_spec(qg, layer), _layer_spec(kg, layer), pl.BlockSpec(gmat.shape, const)]
    widths = [ATT_Q_W, ATT_KV_W, ATT_KV_W]
    out_shape = [jax.ShapeDtypeStruct((m, w), BF16) for w in widths]
    out_specs = [pl.BlockSpec((tm, w), row) for w in widths]
    for w in ((HY_ORDER + 1) * HY_W, FN_W):
        out_shape.append(jax.ShapeDtypeStruct((m // 2, 2 * w), BF16))
        out_specs.append(pl.BlockSpec((tm // 2, 2 * w), row))
    if combine:
        out_shape.insert(0, jax.ShapeDtypeStruct((m, d), F32))
        out_specs.insert(0, pl.BlockSpec((tm, d), row))
    return pl.pallas_call(
        functools.partial(_inproj_kernel, rope=rope, combine=combine),
        grid=(m // tm,), in_specs=specs, out_specs=out_specs, out_shape=out_shape,
        scratch_shapes=[pltpu.VMEM((d // LANES, tm, LANES), F32)],
        compiler_params=_cparams(("arbitrary",)), name="inproj",
    )(*args)


def _attn_kernel(q_ref, kt_ref, v_ref, o_ref):
    tq = q_ref.shape[0]
    hpc = ATTN_HEADS_PER_CHAIN
    chains = [(h0 // Q_GROUP, range(h0, h0 + hpc)) for h0 in range(0, N_Q_HEADS, hpc)]
    ss = []
    for g, heads in chains:
        qg = jnp.concatenate([q_ref[:, h * HEAD_DIM:(h + 1) * HEAD_DIM] for h in heads], axis=0)
        ss.append(_dot(qg, kt_ref[g * HEAD_DIM:(g + 1) * HEAD_DIM, :]))
    outs = []
    for (g, heads), s in zip(chains, ss):
        p = jnp.exp2(s - jnp.max(s, axis=-1, keepdims=True)).astype(BF16)
        o = _dot(p, v_ref[g])
        for j in range(hpc):
            oh = o[j * tq:(j + 1) * tq, :]
            outs.append(oh[:, 0:HEAD_DIM] / oh[:, HEAD_DIM:HEAD_DIM + 1])
    o_ref[...] = jnp.concatenate(outs, axis=1).astype(o_ref.dtype)


def _attention(q, kt, vaug, n, tq):
    m = q.shape[0]
    b, _, s = kt.shape
    tpb = n // tq
    return pl.pallas_call(
        _attn_kernel,
        grid=(b, tpb),
        in_specs=[pl.BlockSpec((tq, ATT_Q_W), lambda bi, i: (bi * tpb + i, 0)),
                  pl.BlockSpec((None, ATT_KV_W, s), lambda bi, i: (bi, 0, 0)),
                  pl.BlockSpec((None, N_KV_HEADS, s, LANES), lambda bi, i: (bi, 0, 0, 0))],
        out_specs=pl.BlockSpec((tq, ATT_Q_W), lambda bi, i: (bi * tpb + i, 0)),
        out_shape=jax.ShapeDtypeStruct((m, ATT_Q_W), BF16),
        compiler_params=_cparams(("arbitrary", "arbitrary")), name="attention",
    )(q, kt, vaug)


def _hyena_filter_kernel(feat_ref, w1_ref, b1_ref, w2_ref, b2_ref, w3_ref, freq_ref, delta_ref,
                         ce_ref, se_ref, co_ref, so_ref, hra_ref, hsa_ref, hrb_ref, hsb_ref, hm_ref):
    n = feat_ref.shape[0]
    half = n // 2
    hp = lax.Precision.HIGHEST
    freq = freq_ref[...]
    h = jnp.sin(freq * (jnp.dot(feat_ref[...], w1_ref[...], preferred_element_type=F32, precision=hp)
                        + b1_ref[...]))
    h = jnp.sin(freq * (jnp.dot(h, w2_ref[...], preferred_element_type=F32, precision=hp) + b2_ref[...]))
    h = jnp.dot(h, w3_ref[...], preferred_element_type=F32, precision=hp)
    w2o = HY_ORDER * HY_W
    row = lax.broadcasted_iota(jnp.int32, (n, w2o), 0)
    pos = jnp.where(row < half, 2 * row, 2 * (row - half) + 1)
    t = pos.astype(F32) / (n - 1)
    decay = jnp.exp(-t * delta_ref[...])
    hf = h[:, 0:w2o] * decay
    hb = jnp.where(pos == 0, 0.0, h[:, w2o:2 * w2o] * decay)
    inv = 1.0 / (jnp.sum(jnp.abs(hf), axis=0, keepdims=True) + jnp.sum(jnp.abs(hb), axis=0, keepdims=True))
    hsum = (hf + hb) * inv
    hdif = (hf - hb) * inv
    sum_e, sum_o = _split_bf16(hsum[0:half, :]), _split_bf16(hsum[half:n, :])
    dif_e, dif_o = _split_bf16(hdif[0:half, :]), _split_bf16(hdif[half:n, :])

    def dot2(m_ref, r, parts):
        return _dot(m_ref[r, :], parts[0]) + _dot(m_ref[r, :], parts[1])

    ck = min(half, HY_CHUNK)
    for r0 in range(0, half, ck):
        r = slice(r0, r0 + ck)
        ec, oc = dot2(ce_ref, r, sum_e), dot2(co_ref, r, sum_o)
        es, os_ = dot2(se_ref, r, dif_e), dot2(so_ref, r, dif_o)
        k = lax.broadcasted_iota(jnp.int32, (ck, w2o), 0) + r0
        wk = jnp.where(k == 0, 1.0, 2.0) * (1.0 / (2 * n))
        hra_ref[r, :] = (ec + oc) * wk
        hsa_ref[r, :] = (es + os_) * wk
        hrb_ref[r, :] = (ec - oc) * wk
        hsb_ref[r, :] = (os_ - es) * wk
    alt = jnp.where(lax.broadcasted_iota(jnp.int32, (half, w2o), 0) % 2 == 0, 1.0, -1.0)
    hm_ref[0:1, :] = jnp.sum(hsum[0:half, :] * alt, axis=0, keepdims=True) * (1.0 / n)
    hm_ref[1:2, :] = jnp.sum(hdif[half:n, :] * alt, axis=0, keepdims=True) * (1.0 / n)


def _hyena_filter(feats, w1, b1, w2, b2, w3, freq, layer, deltas, dft):
    n = feats.shape[0]
    half = n // 2
    w2o = HY_ORDER * HY_W
    args = (feats, w1, b1, w2, b2, w3, freq, deltas) + tuple(dft)
    whole = lambda a: pl.BlockSpec(a.shape, lambda i: (0, 0))
    return pl.pallas_call(
        _hyena_filter_kernel,
        grid=(1,),
        in_specs=[whole(feats)] + [_layer_spec(a, layer) for a in args[1:7]] + [whole(a) for a in args[7:]],
        out_specs=[pl.BlockSpec((half, w2o), lambda i: (0, 0))] * 4 + [pl.BlockSpec((2, w2o), lambda i: (0, 0))],
        out_shape=[jax.ShapeDtypeStruct((half, w2o), F32)] * 4 + [jax.ShapeDtypeStruct((2, w2o), F32)],
        compiler_params=_cparams(("arbitrary",)), name="hyena_filter",
    )(*args)


def _hyena_kernel(u_ref, sw_ref, sb_ref, hra_ref, hsa_ref, hrb_ref, hsb_ref, hm_ref, d_ref,
                  ce_ref, se_ref, co_ref, so_ref, cot_ref, sot_ref, o_ref,
                  zfe_ref, zfo_ref, zbe_ref, zbo_ref, g1e_ref, g1o_ref, g2e_ref, g2o_ref,
                  p_ref, q_ref, p2_ref, q2_ref):
    half = u_ref.shape[0]
    w3 = (HY_ORDER + 1) * HY_W
    ck = min(half, HY_CHUNK)
    chunks = [slice(r, r + ck) for r in range(0, half, ck)]
    row = lax.broadcasted_iota(jnp.int32, (half, HY_W), 0)
    alt = jnp.where(row % 2 == 0, 1.0, -1.0)
    calt = jnp.where(lax.broadcasted_iota(jnp.int32, (ck, HY_W), 0) % 2 == 0, 1.0, -1.0)

    def short(j):
        c = slice(j * HY_W, (j + 1) * HY_W)
        ue = u_ref[:, j * HY_W:(j + 1) * HY_W].astype(F32)
        uo = u_ref[:, w3 + j * HY_W:w3 + (j + 1) * HY_W].astype(F32)
        uo_prev = jnp.where(row == 0, 0.0, pltpu.roll(uo, 1, 0))
        ue_next = jnp.where(row == half - 1, 0.0, pltpu.roll(ue, half - 1, 0))
        w0, w1, w2, b = sw_ref[0:1, c], sw_ref[1:2, c], sw_ref[2:3, c], sb_ref[0:1, c]
        return b + uo_prev * w0 + ue * w1 + uo * w2, b + ue * w0 + uo * w1 + ue_next * w2

    ve, vo = short(0)
    zfe_ref[...], zfo_ref[...] = ve, vo
    zbe_ref[...], zbo_ref[...] = ve.astype(BF16), vo.astype(BF16)
    g1e_ref[...], g1o_ref[...] = short(1)
    g2e_ref[...], g2o_ref[...] = short(2)
    gates = ((g1e_ref, g1o_ref), (g2e_ref, g2o_ref))

    for o in range(HY_ORDER):
        c = slice(o * HY_W, (o + 1) * HY_W)
        xr_m = jnp.sum(zfe_ref[...] * alt, axis=0, keepdims=True)
        xs_m = jnp.sum(zfo_ref[...] * alt, axis=0, keepdims=True)
        hr_m, hs_m = hm_ref[0:1, c], hm_ref[1:2, c]
        yr_m, ys_m = xr_m * hr_m - xs_m * hs_m, xr_m * hs_m + xs_m * hr_m
        for r in chunks:
            ec, es = _dot(ce_ref[r, :], zbe_ref[...]), _dot(se_ref[r, :], zbe_ref[...])
            oc, os_ = _dot(co_ref[r, :], zbo_ref[...]), _dot(so_ref[r, :], zbo_ref[...])
            xra, xsa, xrb, xsb = ec + oc, es + os_, ec - oc, os_ - es
            hra, hsa, hrb, hsb = hra_ref[r, c], hsa_ref[r, c], hrb_ref[r, c], hsb_ref[r, c]
            yra, ysa = xra * hra - xsa * hsa, xra * hsa + xsa * hra
            yrb, ysb = xrb * hrb - xsb * hsb, xrb * hsb + xsb * hrb
            p_ref[r, :] = (yra + yrb).astype(BF16)
            q_ref[r, :] = (ysa - ysb).astype(BF16)
            p2_ref[r, :] = (yra - yrb).astype(BF16)
            q2_ref[r, :] = (ysa + ysb).astype(BF16)
        ge_ref, go_ref = gates[o]
        for r in chunks:
            ye = _dot(ce_ref[r, :], p_ref[...]) + _dot(se_ref[r, :], q_ref[...]) + calt * yr_m
            yo = _dot(cot_ref[r, :], p2_ref[...]) + _dot(sot_ref[r, :], q2_ref[...]) + calt * ys_m
            ye = (ye + zfe_ref[r, :] * d_ref[o:o + 1, :]) * ge_ref[r, :]
            yo = (yo + zfo_ref[r, :] * d_ref[o:o + 1, :]) * go_ref[r, :]
            if o + 1 < HY_ORDER:
                zfe_ref[r, :], zfo_ref[r, :] = ye, yo
                zbe_ref[r, :], zbo_ref[r, :] = ye.astype(BF16), yo.astype(BF16)
            else:
                o_ref[r, 0:HY_W] = ye.astype(o_ref.dtype)
                o_ref[r, HY_W:2 * HY_W] = yo.astype(o_ref.dtype)


def _hyena(hy, sw, sb, spec, dbias, layer, dft, n):
    m, w3 = 2 * hy.shape[0], hy.shape[1] // 2
    half = n // 2
    const = lambda b: (0, 0)
    once = lambda a: pl.BlockSpec(a.shape, const, pipeline_mode=pl.Buffered(1))
    f32s = pltpu.VMEM((half, HY_W), F32)
    bf16s = pltpu.VMEM((half, HY_W), BF16)
    out = pl.pallas_call(
        _hyena_kernel,
        grid=(m // n,),
        in_specs=[pl.BlockSpec((half, 2 * w3), lambda b: (b, 0)),
                  _layer_spec(sw, layer), _layer_spec(sb, layer)]
                 + [once(a) for a in spec[:4]] + [pl.BlockSpec(spec[4].shape, const), _layer_spec(dbias, layer)]
                 + [once(a) for a in dft],
        out_specs=pl.BlockSpec((half, 2 * HY_W), lambda b: (b, 0)),
        out_shape=jax.ShapeDtypeStruct((m // 2, 2 * HY_W), BF16),
        scratch_shapes=[f32s, f32s, bf16s, bf16s, f32s, f32s, f32s, f32s, bf16s, bf16s, bf16s, bf16s],
        compiler_params=_cparams(("arbitrary",)), name="hyena",
    )(hy, sw, sb, *spec, dbias, *dft)
    return out.reshape(m, HY_W)


def _fnet_kernel(u_ref, bc_ref, bs_ref, ce_ref, se_ref, co_ref, so_ref, o_ref):
    half = u_ref.shape[0]
    scale = 1.0 / math.sqrt(2 * half * FN_GROUP_W)
    ue, uo = u_ref[:, 0:FN_W], u_ref[:, FN_W:2 * FN_W]
    uce, use = _dot(ue, bc_ref[...]).astype(BF16), _dot(ue, bs_ref[...]).astype(BF16)
    uco, uso = _dot(uo, bc_ref[...]).astype(BF16), _dot(uo, bs_ref[...]).astype(BF16)
    ck = min(half, HY_CHUNK)
    for r0 in range(0, half, ck):
        r = slice(r0, r0 + ck)
        e = _dot(ce_ref[r, :], uce) - _dot(se_ref[r, :], use)
        o = _dot(co_ref[r, :], uco) - _dot(so_ref[r, :], uso)
        o_ref[r0:r0 + ck, :] = ((e + o) * scale).astype(o_ref.dtype)
        o_ref[half + r0:half + r0 + ck, :] = ((e - o) * scale).astype(o_ref.dtype)


def _fnet(fn, bc, bs, dft, n):
    m = 2 * fn.shape[0]
    half = n // 2
    const = lambda b: (0, 0)
    once = lambda a: pl.BlockSpec(a.shape, const, pipeline_mode=pl.Buffered(1))
    return pl.pallas_call(
        _fnet_kernel,
        grid=(m // n,),
        in_specs=[pl.BlockSpec((half, 2 * FN_W), lambda b: (b, 0)), pl.BlockSpec(bc.shape, const),
                  pl.BlockSpec(bs.shape, const)] + [once(a) for a in dft],
        out_specs=pl.BlockSpec((n, FN_W), lambda b: (b, 0)),
        out_shape=jax.ShapeDtypeStruct((m, FN_W), BF16),
        compiler_params=_cparams(("arbitrary",)), name="fnet",
    )(fn, bc, bs, *dft)


def _merge_kernel(x_ref, mod_ref, g1_ref, g2_ref, att_ref, hy_ref, fn_ref, wg_ref, wb_ref, wo_ref,
                  wrh_ref, wrl_ref, x1_ref, h2_ref, lg_ref):
    tm, d = x_ref.shape
    branches = ((att_ref, 0, ATT_Q_W), (hy_ref, ATT_Q_W, ATT_Q_W + HY_W), (fn_ref, ATT_Q_W + HY_W, MIX_W))
    rows = [slice(r, r + MERGE_CHUNK) for r in range(0, tm, MERGE_CHUNK)]
    xs = [x_ref[r, :] for r in rows]
    hbs = [_rms_mod(x, g1_ref[...], mod_ref[0:1, :], mod_ref[1:2, :]).astype(BF16) for x in xs]
    mixes = []
    for r, hb in zip(rows, hbs):
        mix = None
        for i, (b_ref, lo, hi) in enumerate(branches):
            gate = _dot(hb, wg_ref[:, OFF_GATE + i * d:OFF_GATE + (i + 1) * d])
            gate = 1.0 / (1.0 + jnp.exp(-gate))
            term = gate * _dot(b_ref[r, :], wb_ref[lo:hi, :])
            mix = term if mix is None else mix + term
        mixes.append(mix.astype(BF16))
    for r, x, mix in zip(rows, xs, mixes):
        y = _dot(mix, wo_ref[...])
        x1 = x + mod_ref[2:3, :] * y
        x1_ref[r, :] = x1
        h2 = _rms_mod(x1, g2_ref[...], mod_ref[3:4, :], mod_ref[4:5, :])
        hi, lo = _split_bf16(h2)
        h2_ref[r, :] = hi
        lg_ref[r, :] = _dot(hi, wrh_ref[...]) + _dot(lo, wrh_ref[...]) + _dot(hi, wrl_ref[...])


def _merge(x, mod, g1, g2, att, hyo, fno, wg, layer, wb, wo, wrh, wrl, n, tm):
    m, d = x.shape
    per_b = mod.shape[0] > 1
    tpb = n // tm
    row = lambda i: (i, 0)
    const = lambda i: (0, 0)
    mod_map = (lambda i: (i // tpb, 0, 0)) if per_b else (lambda i: (0, 0, 0))
    return pl.pallas_call(
        _merge_kernel,
        grid=(m // tm,),
        in_specs=[pl.BlockSpec((tm, d), row), pl.BlockSpec((None, 8, d), mod_map),
                  _layer_spec(g1, layer), _layer_spec(g2, layer),
                  pl.BlockSpec((tm, ATT_Q_W), row), pl.BlockSpec((tm, HY_W), row), pl.BlockSpec((tm, FN_W), row),
                  _layer_spec(wg, layer, pipeline_mode=pl.Buffered(1)),
                  _layer_spec(wb, layer), _layer_spec(wo, layer),
                  _layer_spec(wrh, layer), _layer_spec(wrl, layer)],
        out_specs=[pl.BlockSpec((tm, d), row), pl.BlockSpec((tm, d), row), pl.BlockSpec((tm, LANES), row)],
        out_shape=[jax.ShapeDtypeStruct((m, d), F32), jax.ShapeDtypeStruct((m, d), BF16),
                   jax.ShapeDtypeStruct((m, LANES), F32)],
        compiler_params=_cparams(("arbitrary",)), name="merge",
    )(x, mod, g1, g2, att, hyo, fno, wg, wb, wo, wrh, wrl)


def _route_kernel(lg_ref, tri_ref, rank_ref, aff_ref, rankt_ref, *, cap):
    ns, _, n = rank_ref.shape
    rows = ns * N_EXPERTS
    affs = []
    for s in range(ns):
        lt = lg_ref[s * n:(s + 1) * n, :].T[0:N_EXPERTS, :]
        e = jnp.exp(lt - jnp.max(lt, axis=0, keepdims=True))
        affs.append(e / jnp.sum(e, axis=0, keepdims=True))
    aff = jnp.concatenate(affs, axis=0) if ns > 1 else affs[0]
    bits = pltpu.bitcast(aff, jnp.int32)

    def step(i, thr):
        cand = thr | jnp.left_shift(jnp.int32(1), 30 - i)
        cnt = jnp.sum(jnp.where(bits >= cand, 1.0, 0.0), axis=1, keepdims=True)
        return jnp.where(cnt >= cap, cand, thr)

    thr = lax.fori_loop(0, 31, step, jnp.zeros((rows, 1), jnp.int32))
    gt = bits > thr
    eq = bits == thr
    need = cap - jnp.sum(jnp.where(gt, 1.0, 0.0), axis=1, keepdims=True)

    def excl_cumsum(mask):
        parts = []
        off = jnp.zeros((rows, 1), F32)
        for c in range(n // LANES):
            blk = mask[:, c * LANES:(c + 1) * LANES]
            parts.append(_dot(blk.astype(BF16), tri_ref[...]) + off)
            off = off + jnp.sum(blk, axis=1, keepdims=True)
        return jnp.concatenate(parts, axis=1)

    tie_rank = excl_cumsum(jnp.where(eq, 1.0, 0.0))
    sel = jnp.where(gt, 1.0, jnp.where(eq, jnp.where(tie_rank < need, 1.0, 0.0), 0.0))
    rank = jnp.where(sel > 0.5, excl_cumsum(sel), -1.0)
    pad = jnp.full((LANES - N_EXPERTS, n), -1.0, F32)
    for s in range(ns):
        r = rank[s * N_EXPERTS:(s + 1) * N_EXPERTS, :]
        rank_ref[s] = r.astype(jnp.int32)
        aff_ref[s] = affs[s]
        rankt_ref[s * n:(s + 1) * n, :] = jnp.concatenate([r, pad], axis=0).T.astype(BF16)


def _route(logits, tri, n, cap):
    m = logits.shape[0]
    b = m // n
    assert cap <= 256
    ns = math.gcd(b, ROUTE_SAMPLES)
    return pl.pallas_call(
        functools.partial(_route_kernel, cap=cap),
        grid=(b // ns,),
        in_specs=[pl.BlockSpec((ns * n, LANES), lambda i: (i, 0)), pl.BlockSpec(tri.shape, lambda i: (0, 0))],
        out_specs=[pl.BlockSpec((ns, N_EXPERTS, n), lambda i: (i, 0, 0)),
                   pl.BlockSpec((ns, N_EXPERTS, n), lambda i: (i, 0, 0)),
                   pl.BlockSpec((ns * n, LANES), lambda i: (i, 0))],
        out_shape=[jax.ShapeDtypeStruct((b, N_EXPERTS, n), jnp.int32),
                   jax.ShapeDtypeStruct((b, N_EXPERTS, n), F32),
                   jax.ShapeDtypeStruct((m, LANES), BF16)],
        compiler_params=_cparams(("arbitrary",)), name="route",
    )(logits, tri)


def _gather_kernel(h_ref, rank_ref, aff_ref, xg_ref, w_ref, sel_ref, *, cap):
    n, d = h_ref.shape
    slot = lax.broadcasted_iota(jnp.int32, (cap, n), 0)
    for e in range(N_EXPERTS):
        hit = slot == rank_ref[e:e + 1, :]
        sel_ref[e * cap:(e + 1) * cap, :] = jnp.where(hit, 1.0, 0.0).astype(BF16)
        wcol = jnp.sum(jnp.where(hit, aff_ref[e:e + 1, :], 0.0), axis=1, keepdims=True)
        w_ref[e] = jnp.broadcast_to(wcol, (cap, LANES))
    tn = 256
    for c in range(d // tn):
        xg = _dot(sel_ref[...], h_ref[:, c * tn:(c + 1) * tn]).astype(BF16)
        for e in range(N_EXPERTS):
            xg_ref[e, :, c * tn:(c + 1) * tn] = xg[e * cap:(e + 1) * cap, :]


def _gather(h2, rank, aff, n, cap):
    m, d = h2.shape
    b = m // n
    return pl.pallas_call(
        functools.partial(_gather_kernel, cap=cap),
        grid=(b,),
        in_specs=[pl.BlockSpec((n, d), lambda i: (i, 0)),
                  pl.BlockSpec((None, N_EXPERTS, n), lambda i: (i, 0, 0)),
                  pl.BlockSpec((None, N_EXPERTS, n), lambda i: (i, 0, 0))],
        out_specs=[pl.BlockSpec((N_EXPERTS, None, cap, d), lambda i: (0, i, 0, 0)),
                   pl.BlockSpec((N_EXPERTS, None, cap, LANES), lambda i: (0, i, 0, 0))],
        out_shape=[jax.ShapeDtypeStruct((N_EXPERTS, b, cap, d), BF16),
                   jax.ShapeDtypeStruct((N_EXPERTS, b, cap, LANES), F32)],
        scratch_shapes=[pltpu.VMEM((N_EXPERTS * cap, n), BF16)],
        compiler_params=_cparams(("arbitrary",)), name="moe_gather",
    )(h2, rank, aff)


def _ffn_kernel(*refs, extra):
    if extra:
        x_ref, w_ref, xc_ref, wc_ref, wg_ref, wu_ref, wd_ref, o_ref, oc_ref, acc_ref, accc_ref = refs
    else:
        x_ref, w_ref, wg_ref, wu_ref, wd_ref, o_ref, acc_ref = refs
    f = pl.program_id(2)

    def run(x_ref, w_ref, o_ref, acc_ref):
        @pl.when((pl.program_id(0) == 0) & (pl.program_id(1) == 0) & (f == 0))
        def _():
            acc_ref[...] = jnp.zeros_like(acc_ref)

        x = x_ref[...]
        d = acc_ref.shape[1]
        tf = 512
        hms, wds = [], []
        for c in range(wg_ref.shape[1] // tf):
            cols = slice(c * tf, (c + 1) * tf)
            a = _dot(x, wg_ref[:, cols].astype(BF16))
            u = _dot(x, wu_ref[:, cols].astype(BF16))
            hms.append((a * (1.0 / (1.0 + jnp.exp(-a))) * u).astype(BF16))
            wds.append(wd_ref[cols, :].astype(BF16))
        tn = 256
        for c in range(d // tn):
            cols = slice(c * tn, (c + 1) * tn)
            t = sum(_dot(hm, wd[:, cols]) for hm, wd in zip(hms, wds))
            s = jnp.where(f > 0, acc_ref[:, cols], 0.0) + t
            acc_ref[:, cols] = s
            o_ref[:, cols] = (s * w_ref[...] if tn == LANES else
                              s * jnp.concatenate([w_ref[...]] * (tn // LANES), axis=1)).astype(o_ref.dtype)

    run(x_ref, w_ref, o_ref, acc_ref)
    if extra:
        @pl.when(pl.program_id(1) == 0)
        def _():
            run(xc_ref, wc_ref, oc_ref, accc_ref)


def _ffn(xg, w, xgc, wc, wg, wu, wd, layer, tm, tf):
    ne, rows, d = xg.shape
    ff = wg.shape[3]
    extra = xgc is not None
    tile = lambda e, i, f: (e, i, 0)
    whole = lambda e, i, f: (e, 0, 0)
    args = [xg, w]
    in_specs = [pl.BlockSpec((None, tm, d), tile), pl.BlockSpec((None, tm, LANES), tile)]
    out_specs = [pl.BlockSpec((None, tm, d), tile)]
    out_shape = [jax.ShapeDtypeStruct((ne, rows, d), BF16)]
    scratch = [pltpu.VMEM((tm, d), F32)]
    if extra:
        rc = xgc.shape[1]
        args += [xgc, wc]
        in_specs += [pl.BlockSpec((None, rc, d), whole), pl.BlockSpec((None, rc, LANES), whole)]
        out_specs.append(pl.BlockSpec((None, rc, d), whole))
        out_shape.append(jax.ShapeDtypeStruct((ne, rc, d), BF16))
        scratch.append(pltpu.VMEM((rc, d), F32))
    args += [wg, wu, wd]
    in_specs += [pl.BlockSpec((None, None, d, tf), lambda e, i, f: (layer, e, 0, f)),
                 pl.BlockSpec((None, None, d, tf), lambda e, i, f: (layer, e, 0, f)),
                 pl.BlockSpec((None, None, tf, d), lambda e, i, f: (layer, e, f, 0))]
    outs = pl.pallas_call(
        functools.partial(_ffn_kernel, extra=extra),
        grid=(ne, rows // tm, ff // tf),
        in_specs=in_specs, out_specs=out_specs, out_shape=out_shape, scratch_shapes=scratch,
        compiler_params=_cparams(("arbitrary", "arbitrary", "arbitrary")), name="moe_ffn",
    )(*args)
    return (outs[0], outs[1]) if extra else (outs[0], None)


def _scatter_kernel(rankt_ref, y_ref, expand_ref, slot_ref, o_ref, selt_ref, *, cap):
    n = rankt_ref.shape[0]
    d = o_ref.shape[1]
    width = N_EXPERTS * cap
    if cap % LANES == 0:
        rank = rankt_ref[...].astype(F32)
        slot = lax.broadcasted_iota(jnp.int32, (n, cap), 1).astype(F32)
        for e in range(N_EXPERTS):
            r = jnp.broadcast_to(rank[:, e:e + 1], (n, cap))
            selt_ref[:, e * cap:(e + 1) * cap] = jnp.where(r == slot, 1.0, 0.0).astype(BF16)
    else:
        r = _dot(rankt_ref[...], expand_ref[...])
        selt_ref[...] = jnp.where(r == slot_ref[...], 1.0, 0.0).astype(BF16)
    y = y_ref[...].reshape(width, d)
    tn = 256
    for c in range(d // tn):
        o_ref[:, c * tn:(c + 1) * tn] = _dot(selt_ref[...], y[:, c * tn:(c + 1) * tn])


def _scatter(rankt, yw, expand, slotpat, n, cap):
    ne, b, _, d = yw.shape
    return pl.pallas_call(
        functools.partial(_scatter_kernel, cap=cap),
        grid=(b,),
        in_specs=[pl.BlockSpec((n, LANES), lambda i: (i, 0)),
                  pl.BlockSpec((ne, None, cap, d), lambda i: (0, i, 0, 0)),
                  pl.BlockSpec(expand.shape, lambda i: (0, 0)),
                  pl.BlockSpec(slotpat.shape, lambda i: (0, 0))],
        out_specs=pl.BlockSpec((n, d), lambda i: (i, 0)),
        out_shape=jax.ShapeDtypeStruct((b * n, d), F32),
        scratch_shapes=[pltpu.VMEM((n, ne * cap), BF16)],
        compiler_params=_cparams(("arbitrary",)), name="moe_scatter",
    )(rankt, yw, expand, slotpat)


def _moe(routed, routed_extra, wg, wu, wd, layer):
    def flat(r):
        rows = r["b"] * r["cap"]
        return r["xg"].reshape(N_EXPERTS, rows, D_MODEL), r["w"].reshape(N_EXPERTS, rows, LANES)

    def scatter(r, yw):
        yw = yw.reshape(N_EXPERTS, r["b"], r["cap"], D_MODEL)
        return _scatter(r["rankt"], yw, r["tabs"]["expand"], r["tabs"]["slotpat"], r["n"], r["cap"])

    xg, w = flat(routed)
    xgc, wc = flat(routed_extra) if routed_extra is not None else (None, None)
    yw, ywc = _ffn(xg, w, xgc, wc, wg, wu, wd, layer, min(xg.shape[1], FFN_ROW_TILE), FFN_HIDDEN_TILE)
    return scatter(routed, yw), (scatter(routed_extra, ywc) if routed_extra is not None else None)


def _combine_kernel(x_ref, y_ref, mod_ref, g_ref, o_ref, *, norm):
    x = x_ref[...] + mod_ref[5:6, :] * y_ref[...]
    if norm:
        x = x * lax.rsqrt(jnp.mean(x * x, axis=-1, keepdims=True) + EPS) * g_ref[...]
    o_ref[...] = x


def _combine(x, y, mod, g, n, tm, norm):
    m, d = x.shape
    per_b = mod.shape[0] > 1
    tpb = n // tm
    row = lambda i: (i, 0)
    mod_map = (lambda i: (i // tpb, 0, 0)) if per_b else (lambda i: (0, 0, 0))
    return pl.pallas_call(
        functools.partial(_combine_kernel, norm=norm),
        grid=(m // tm,),
        in_specs=[pl.BlockSpec((tm, d), row), pl.BlockSpec((tm, d), row),
                  pl.BlockSpec((None, 8, d), mod_map), pl.BlockSpec((1, d), lambda i: (0, 0))],
        out_specs=pl.BlockSpec((tm, d), row),
        out_shape=jax.ShapeDtypeStruct((m, d), F32),
        compiler_params=_cparams(("arbitrary",)), name="combine",
    )(x, y, mod, g)


def _fnet_dft_tables(n):
    j = jnp.arange(n // 2, dtype=jnp.int32)
    even = ((j[:, None] * (2 * j[None, :])) % n).astype(F32) * (2.0 * math.pi / n)
    odd = ((j[:, None] * (2 * j[None, :] + 1)) % n).astype(F32) * (2.0 * math.pi / n)
    return tuple(a.astype(BF16) for a in (jnp.cos(even), jnp.sin(even), jnp.cos(odd), jnp.sin(odd)))


def _rope_tables(n):
    rows = n // GRID_W
    row = jnp.repeat(jnp.arange(rows, dtype=F32), GRID_W)
    col = jnp.tile(jnp.arange(GRID_W, dtype=F32), rows)
    inv = ROPE_THETA ** (-jnp.arange(0, ROPE_AXIS_DIM, 2, dtype=F32) / ROPE_AXIS_DIM)
    ang = jnp.concatenate([row[:, None] * inv, col[:, None] * inv], axis=-1)
    cos = jnp.repeat(jnp.cos(ang), 2, axis=1)
    sin = jnp.repeat(jnp.sin(ang), 2, axis=1) * jnp.tile(jnp.array([-1.0, 1.0], F32), HEAD_DIM // 2)
    return jnp.tile(cos, (1, LANES // HEAD_DIM)), jnp.tile(sin, (1, LANES // HEAD_DIM))


def _hyena_dft_tables(n):
    j = jnp.arange(n // 2, dtype=jnp.int32)
    even = ((j[:, None] * j[None, :]) % n).astype(F32) * (2.0 * math.pi / n)
    odd = ((j[:, None] * (2 * j[None, :] + 1)) % (2 * n)).astype(F32) * (math.pi / n)
    ce, se, co, so = jnp.cos(even), jnp.sin(even), jnp.cos(odd), jnp.sin(odd)
    return tuple(a.astype(BF16) for a in (ce, se, co, so, co.T, so.T))


def _hyena_feats(n):
    pos = jnp.concatenate([jnp.arange(0, n, 2), jnp.arange(1, n, 2)]).astype(F32)
    t = pos / (n - 1)
    bands = jnp.linspace(1e-4, HY_BANDS - 1, HY_BANDS, dtype=F32)
    ang = (2.0 * math.pi / n) * pos[:, None] * bands[None, :]
    feats = jnp.concatenate([t[:, None], jnp.cos(ang), -jnp.sin(ang)], axis=-1)
    return jnp.pad(feats, ((0, 0), (0, LANES - HY_EMB)))


def _side(x, xb, n, mod, l, P, tabs, kv_ext, rope, tm, last_ctx):
    b = x.shape[0] // n
    outs = _inproj(x, xb, mod, P["norm1_g"], P["w_in"], l, tabs["cos"] if rope else None, tabs["sin"] if rope else None,
                   P["qg"], P["kg"], P["gmat"], n, tm)
    if xb is not None:
        x, outs = outs[0], outs[1:]
    q, k, v, hy, fn = outs
    k3, v3 = k.reshape(b, n, ATT_KV_W), v.reshape(b, n, ATT_KV_W)
    if last_ctx:
        return None, None, k3, v3
    if kv_ext is not None:
        kc, vc = jnp.concatenate([k3, kv_ext[0]], axis=1), jnp.concatenate([v3, kv_ext[1]], axis=1)
    else:
        kc, vc = k3, v3
    s = kc.shape[1]
    fill = jnp.concatenate([jnp.ones((b, s, 1), BF16), jnp.zeros((b, s, LANES - HEAD_DIM - 1), BF16)], axis=-1)
    vaug = jnp.stack([jnp.concatenate([vc[..., g * HEAD_DIM:(g + 1) * HEAD_DIM], fill], axis=-1)
                      for g in range(N_KV_HEADS)], axis=1)
    att = _attention(q, jnp.swapaxes(kc, 1, 2), vaug, n, min(n, ATTN_Q_TILE))
    spec = _hyena_filter(tabs["feats"], P["hy_w1"], P["hy_b1"], P["hy_w2"], P["hy_b2"],
                         P["hy_w3"], P["hy_freq"], l, tabs["deltas"], tabs["hy_dft"][:4])
    hyo = _hyena(hy, P["hy_sw"], P["hy_sb"], spec, P["hy_bias"], l, tabs["hy_dft"], n)
    fno = _fnet(fn, tabs["bc"], tabs["bs"], tabs["fn_dft"], n)
    x1, h2, logits = _merge(x, mod, P["norm1_g"], P["norm2_g"], att, hyo, fno, P["w_in"], l,
                            P["w_branch"], P["w_out"], P["wr_hi"], P["wr_lo"], n, tm)
    cap = CAPACITY_FACTOR * n // N_EXPERTS
    rank, aff, rankt = _route(logits, tabs["tri"], n, cap)
    xg, w = _gather(h2, rank, aff, n, cap)
    return x1, dict(xg=xg, w=w, rankt=rankt, tabs=tabs, n=n, cap=cap, b=b), k3, v3


def _tables(n, rope):
    a = jnp.arange(FN_W, dtype=jnp.int32)
    same = (a[:, None] // FN_GROUP_W) == (a[None, :] // FN_GROUP_W)
    ang = ((a[:, None] * a[None, :]) % FN_GROUP_W).astype(F32) * (2.0 * math.pi / FN_GROUP_W)
    deltas = jnp.abs(jnp.linspace(math.log(HY_DECAY_TARGET) / HY_DECAY_LONG_PCT,
                                  math.log(HY_DECAY_TARGET) / HY_DECAY_SHORT_PCT, HY_W, dtype=F32))
    i = jnp.arange(LANES, dtype=jnp.int32)
    cap = CAPACITY_FACTOR * n // N_EXPERTS
    j = jnp.arange(N_EXPERTS * cap, dtype=jnp.int32)
    tabs = dict(expand=(i[:, None] == j[None, :] // cap).astype(BF16),
                slotpat=(j % cap).astype(F32)[None, :],
                hy_dft=_hyena_dft_tables(n), fn_dft=_fnet_dft_tables(n),
                bc=jnp.where(same, jnp.cos(ang), 0.0).astype(BF16),
                bs=jnp.where(same, jnp.sin(ang), 0.0).astype(BF16),
                feats=_hyena_feats(n), deltas=jnp.tile(deltas, HY_ORDER)[None, :],
                tri=(i[:, None] < i[None, :]).astype(BF16))
    if rope:
        tabs["cos"], tabs["sin"] = _rope_tables(n)
    return tabs


def kernel(x, c, ctx, c_ctx, w_mod, b_mod, norm1_g, norm2_g, w_in, q_gain, k_gain, hy_short_w, hy_short_b,
           hy_f_w1, hy_f_b1, hy_f_w2, hy_f_b2, hy_f_w3, hy_f_freq, hy_bias, w_branch, w_out, w_router,
           w_gate, w_up, w_down, final_g):
    bsz, n_lat, d = x.shape
    n_ctx = ctx.shape[1]
    depth = w_mod.shape[0]
    assert d == D_MODEL and n_lat % LANES == 0 and n_ctx % LANES == 0

    hid = jnp.arange(ATT_Q_W, dtype=jnp.int32) // HEAD_DIM
    P = dict(
        norm1_g=norm1_g[:, None, :], norm2_g=norm2_g[:, None, :],
        w_in=w_in.astype(BF16),
        qg=jnp.tile(q_gain, (1, N_Q_HEADS))[:, None, :], kg=jnp.tile(k_gain, (1, N_KV_HEADS))[:, None, :],
        gmat=(hid[:, None] == hid[None, :]).astype(BF16),
        hy_sw=hy_short_w, hy_sb=hy_short_b[:, None, :],
        hy_w1=jnp.pad(hy_f_w1, ((0, 0), (0, LANES - HY_EMB), (0, 0))), hy_b1=hy_f_b1[:, None, :],
        hy_w2=hy_f_w2, hy_b2=hy_f_b2[:, None, :], hy_w3=hy_f_w3, hy_freq=hy_f_freq[:, None, :],
        hy_bias=hy_bias,
        w_branch=w_branch.astype(BF16), w_out=w_out.astype(BF16),
        w_gate=w_gate, w_up=w_up, w_down=w_down,
    )
    wr = jnp.pad(w_router, ((0, 0), (0, 0), (0, LANES - N_EXPERTS)))
    P["wr_hi"] = wr.astype(BF16)
    P["wr_lo"] = (wr - P["wr_hi"].astype(F32)).astype(BF16)

    tab_x = _tables(n_lat, True)
    tab_c = _tables(n_ctx, False)

    rows = -(-(bsz + 1) // 8) * 8
    c_all = jnp.concatenate([c, c_ctx[None, :], jnp.zeros((rows - bsz - 1, d), F32)], axis=0)
    mod = _modulation(c_all, w_mod, b_mod).reshape(depth, rows, 6, d)
    mod = jnp.pad(mod, ((0, 0), (0, 0), (0, 2), (0, 0)))

    tm_x, tm_c = min(n_lat, ROW_TILE), min(n_ctx, ROW_TILE)
    xs, xpend = x.reshape(bsz * n_lat, d), None
    cs, cpend = ctx.reshape(bsz * n_ctx, d), None
    for l in range(depth):
        last = l == depth - 1
        mod_x, mod_c = mod[l, :bsz], mod[l, bsz:bsz + 1]
        pm_x = None if l == 0 else jnp.concatenate([mod_x[:, :5], mod[l - 1, :bsz, 5:6], mod_x[:, 6:]], axis=1)
        pm_c = None if l == 0 else jnp.concatenate([mod_c[:, :5], mod[l - 1, bsz:bsz + 1, 5:6], mod_c[:, 6:]],
                                                   axis=1)
        c1, rc, kc, vc = _side(cs, cpend, n_ctx, mod_c if l == 0 else pm_c, l, P, tab_c, None, False, tm_c, last)
        x1, rx, _, _ = _side(xs, xpend, n_lat, mod_x if l == 0 else pm_x, l, P, tab_x, (kc, vc), True, tm_x, False)
        xy, cy = _moe(rx, rc, P["w_gate"], P["w_up"], P["w_down"], l)
        xs, xpend = x1, xy
        if not last:
            cs, cpend = c1, cy
    out = _combine(xs, xpend, mod[depth - 1, :bsz], final_g[None, :], n_lat, tm_x, True)
    return out.reshape(bsz, n_lat, d)
```

```python
import functools
import math

import jax
import jax.numpy as jnp
from jax import lax
from jax.experimental import pallas as pl
from jax.experimental.pallas import tpu as pltpu

F32 = jnp.float32
BF16 = jnp.bfloat16

D_MODEL = 1024
GRID_W = 64
HEAD_DIM = 64
N_Q_HEADS = 8
N_KV_HEADS = 2
Q_GROUP = N_Q_HEADS // N_KV_HEADS
ATT_Q_W = N_Q_HEADS * HEAD_DIM
ATT_KV_W = N_KV_HEADS * HEAD_DIM
ROPE_THETA = 10000.0
ROPE_AXIS_DIM = HEAD_DIM // 2
HY_W = D_MODEL // 4
HY_ORDER = 2
HY_SHORT = 3
HY_BANDS = 16
HY_EMB = 2 * HY_BANDS + 1
HY_FFN = 64
HY_DECAY_TARGET = 1e-2
HY_DECAY_SHORT_PCT = 0.3
HY_DECAY_LONG_PCT = 1.5
FN_GROUPS = 4
FN_GROUP_W = D_MODEL // 16
FN_W = FN_GROUPS * FN_GROUP_W
MIX_W = ATT_Q_W + HY_W + FN_W
N_BRANCH = 3
OFF_Q = 0
OFF_K = OFF_Q + ATT_Q_W
OFF_V = OFF_K + ATT_KV_W
OFF_HY = OFF_V + ATT_KV_W
OFF_FN = OFF_HY + (HY_ORDER + 1) * HY_W
OFF_GATE = OFF_FN + FN_W
N_EXPERTS = 16
CAPACITY_FACTOR = 2
EPS = 1e-6

LANES = 128
VMEM_LIMIT = 56 * 1024 * 1024
ROW_TILE = 512
ATTN_Q_TILE = 256
FFN_ROW_TILE = 1024
FFN_HIDDEN_TILE = 1024
HY_CHUNK = 512
ROUTE_SAMPLES = 4
INPROJ_CHUNK = 256
MERGE_CHUNK = 256
ATTN_HEADS_PER_CHAIN = 4

A_Q, A_K, A_V, A_HY, A_FN, A_END = 0, 512, 640, 768, 1536, 1792


def _cparams(sem):
    return pltpu.CompilerParams(dimension_semantics=sem, vmem_limit_bytes=VMEM_LIMIT)


def _dot(a, b):
    return jnp.dot(a, b, preferred_element_type=F32)


def _split_bf16(x):
    hi = x.astype(BF16)
    lo = (x - hi.astype(F32)).astype(BF16)
    return hi, lo


def _layer_spec(arr, layer, **kw):
    zeros = (0,) * (arr.ndim - 1)
    return pl.BlockSpec((None,) + arr.shape[1:], lambda *_: (layer,) + zeros, **kw)


def _rms_mod(x, g, shift, scale):
    y = x * lax.rsqrt(jnp.mean(x * x, axis=-1, keepdims=True) + EPS)
    return (y * g) * (1.0 + scale) + shift


def _mod_kernel(c_ref, w_ref, b_ref, o_ref):
    c = c_ref[...]
    sc = c * (1.0 / (1.0 + jnp.exp(-c)))
    o_ref[...] = jnp.dot(sc, w_ref[...], preferred_element_type=F32,
                         precision=lax.Precision.HIGHEST) + b_ref[...]


def _modulation(c_all, w_mod, b_mod):
    depth, d, n6 = w_mod.shape
    rows = c_all.shape[0]
    tn = 1536
    return pl.pallas_call(
        _mod_kernel,
        grid=(depth, n6 // tn),
        in_specs=[pl.BlockSpec((rows, d), lambda l, j: (0, 0)),
                  pl.BlockSpec((None, d, tn), lambda l, j: (l, 0, j)),
                  pl.BlockSpec((None, 1, tn), lambda l, j: (l, 0, j))],
        out_specs=pl.BlockSpec((None, rows, tn), lambda l, j: (l, 0, j)),
        out_shape=jax.ShapeDtypeStruct((depth, rows, n6), F32),
        compiler_params=_cparams(("arbitrary", "arbitrary")),
        name="modulation",
    )(c_all, w_mod, b_mod.reshape(depth, 1, n6))


def _head_norm(u, gmat_ref, width):
    gm = gmat_ref[0:width, 0:width]
    ms = _dot((u * u).astype(BF16), gm) * (1.0 / HEAD_DIM)
    return lax.rsqrt(ms + EPS)


def _inproj_kernel(*refs, rope, combine):
    it = iter(refs)
    x_ref = next(it)
    if combine:
        xb_ref = next(it)
    mod_ref = next(it)
    g_ref = next(it)
    w_ref = next(it)
    if rope:
        cos_ref = next(it)
        sin_ref = next(it)
    qg_ref = next(it)
    kg_ref = next(it)
    gmat_ref = next(it)
    if combine:
        xo_ref = next(it)
    q_ref = next(it)
    k_ref = next(it)
    v_ref = next(it)
    hy_ref = next(it)
    fn_ref = next(it)
    hs_ref = next(it)

    tm = x_ref.shape[0]
    ck = min(tm, INPROJ_CHUNK)
    chunks = [slice(r0, r0 + ck) for r0 in range(0, tm, ck)]
    hbs, hps = [], []
    for rows in chunks:
        x = x_ref[rows, :]
        if combine:
            x = x + mod_ref[5:6, :] * xb_ref[rows, :]
            xo_ref[rows, :] = x
        h = _rms_mod(x, g_ref[...], mod_ref[0:1, :], mod_ref[1:2, :])
        hbs.append(h.astype(BF16))
        parts = []
        for j in range(h.shape[1] // LANES):
            hs_ref[j, rows, :] = h[:, j * LANES:(j + 1) * LANES]
            he = hs_ref[j, pl.ds(rows.start, ck // 2, stride=2), :]
            ho = hs_ref[j, pl.ds(rows.start + 1, ck // 2, stride=2), :]
            parts.append(jnp.concatenate([he, ho], axis=0))
        hps.append(jnp.concatenate(parts, axis=1).astype(BF16))

    def qk(rows, hb, lo, width, gain_ref, out_ref, scale):
        u = _dot(hb, w_ref[:, lo:lo + width])
        r = _head_norm(u, gmat_ref, width)
        un = u * r * gain_ref[0:1, :]
        if rope:
            even = lax.broadcasted_iota(jnp.int32, (un.shape[0], LANES), 1) % 2 == 0
            parts = []
            for j in range(width // LANES):
                s = un[:, j * LANES:(j + 1) * LANES]
                sw = jnp.where(even, pltpu.roll(s, LANES - 1, 1), pltpu.roll(s, 1, 1))
                parts.append(s * cos_ref[rows, :] + sw * sin_ref[rows, :])
            un = jnp.concatenate(parts, axis=1) if len(parts) > 1 else parts[0]
        out_ref[rows, :] = (un * scale).astype(out_ref.dtype)

    for rows, hb, hp in zip(chunks, hbs, hps):
        qk(rows, hb, A_Q, ATT_Q_W, qg_ref, q_ref, HEAD_DIM ** -0.5 * math.log2(math.e))
        qk(rows, hb, A_K, ATT_KV_W, kg_ref, k_ref, 1.0)
        v_ref[rows, :] = _dot(hb, w_ref[:, A_V:A_HY]).astype(v_ref.dtype)
        pairs = slice(rows.start // 2, rows.start // 2 + ck // 2)
        for out_ref, lo, hi in ((hy_ref, A_HY, A_FN), (fn_ref, A_FN, A_END)):
            u = _dot(hp, w_ref[:, lo:hi]).astype(out_ref.dtype)
            out_ref[pairs, 0:hi - lo] = u[0:ck // 2, :]
            out_ref[pairs, hi - lo:2 * (hi - lo)] = u[ck // 2:ck, :]


def _inproj(x, xb, mod, norm_g, w_in, layer, cos_t, sin_t, qg, kg, gmat, n, tm):
    m, d = x.shape
    rope = cos_t is not None
    combine = xb is not None
    per_b = mod.shape[0] > 1
    tpb = n // tm
    row = lambda i: (i, 0)
    const = lambda i: (0, 0)
    mod_map = (lambda i: (i // tpb, 0, 0)) if per_b else (lambda i: (0, 0, 0))
    args, specs = [x], [pl.BlockSpec((tm, d), row)]
    if combine:
        args.append(xb)
        specs.append(pl.BlockSpec((tm, d), row))
    args += [mod, norm_g, w_in]
    specs += [pl.BlockSpec((None, 8, d), mod_map), _layer_spec(norm_g, layer),
              pl.BlockSpec((None, d, A_END), lambda i: (layer, 0, 0))]
    if rope:
        args += [cos_t, sin_t]
        specs += [pl.BlockSpec((tm, LANES), lambda i: (i % tpb, 0)),
                  pl.BlockSpec((tm, LANES), lambda i: (i % tpb, 0))]
    args += [qg, kg, gmat]
    specs += [_layer_spec(qg, layer), _layer_spec(kg, layer), pl.BlockSpec(gmat.shape, const)]
    widths = [ATT_Q_W, ATT_KV_W, ATT_KV_W]
    out_shape = [jax.ShapeDtypeStruct((m, w), BF16) for w in widths]
    out_specs = [pl.BlockSpec((tm, w), row) for w in widths]
    for w in ((HY_ORDER + 1) * HY_W, FN_W):
        out_shape.append(jax.ShapeDtypeStruct((m // 2, 2 * w), BF16))
        out_specs.append(pl.BlockSpec((tm // 2, 2 * w), row))
    if combine:
        out_shape.insert(0, jax.ShapeDtypeStruct((m, d), F32))
        out_specs.insert(0, pl.BlockSpec((tm, d), row))
    return pl.pallas_call(
        functools.partial(_inproj_kernel, rope=rope, combine=combine),
        grid=(m // tm,), in_specs=specs, out_specs=out_specs, out_shape=out_shape,
        scratch_shapes=[pltpu.VMEM((d // LANES, tm, LANES), F32)],
        compiler_params=_cparams(("arbitrary",)), name="inproj",
    )(*args)


def _attn_kernel(q_ref, kt_ref, v_ref, o_ref):
    tq = q_ref.shape[0]
    hpc = ATTN_HEADS_PER_CHAIN
    chains = [(h0 // Q_GROUP, range(h0, h0 + hpc)) for h0 in range(0, N_Q_HEADS, hpc)]
    ss = []
    for g, heads in chains:
        qg = jnp.concatenate([q_ref[:, h * HEAD_DIM:(h + 1) * HEAD_DIM] for h in heads], axis=0)
        ss.append(_dot(qg, kt_ref[g * HEAD_DIM:(g + 1) * HEAD_DIM, :]))
    outs = []
    for (g, heads), s in zip(chains, ss):
        p = jnp.exp2(s - jnp.max(s, axis=-1, keepdims=True)).astype(BF16)
        o = _dot(p, v_ref[g])
        for j in range(hpc):
            oh = o[j * tq:(j + 1) * tq, :]
            outs.append(oh[:, 0:HEAD_DIM] / oh[:, HEAD_DIM:HEAD_DIM + 1])
    o_ref[...] = jnp.concatenate(outs, axis=1).astype(o_ref.dtype)


def _attention(q, kt, vaug, n, tq):
    m = q.shape[0]
    b, _, s = kt.shape
    tpb = n // tq
    return pl.pallas_call(
        _attn_kernel,
        grid=(b, tpb),
        in_specs=[pl.BlockSpec((tq, ATT_Q_W), lambda bi, i: (bi * tpb + i, 0)),
                  pl.BlockSpec((None, ATT_KV_W, s), lambda bi, i: (bi, 0, 0)),
                  pl.BlockSpec((None, N_KV_HEADS, s, LANES), lambda bi, i: (bi, 0, 0, 0))],
        out_specs=pl.BlockSpec((tq, ATT_Q_W), lambda bi, i: (bi * tpb + i, 0)),
        out_shape=jax.ShapeDtypeStruct((m, ATT_Q_W), BF16),
        compiler_params=_cparams(("arbitrary", "arbitrary")), name="attention",
    )(q, kt, vaug)


def _hyena_filter_kernel(feat_ref, w1_ref, b1_ref, w2_ref, b2_ref, w3_ref, freq_ref, delta_ref,
                         ce_ref, se_ref, co_ref, so_ref, hra_ref, hsa_ref, hrb_ref, hsb_ref, hm_ref):
    n = feat_ref.shape[0]
    half = n // 2
    hp = lax.Precision.HIGHEST
    freq = freq_ref[...]
    h = jnp.sin(freq * (jnp.dot(feat_ref[...], w1_ref[...], preferred_element_type=F32, precision=hp)
                        + b1_ref[...]))
    h = jnp.sin(freq * (jnp.dot(h, w2_ref[...], preferred_element_type=F32, precision=hp) + b2_ref[...]))
    h = jnp.dot(h, w3_ref[...], preferred_element_type=F32, precision=hp)
    w2o = HY_ORDER * HY_W
    row = lax.broadcasted_iota(jnp.int32, (n, w2o), 0)
    pos = jnp.where(row < half, 2 * row, 2 * (row - half) + 1)
    t = pos.astype(F32) / (n - 1)
    decay = jnp.exp(-t * delta_ref[...])
    hf = h[:, 0:w2o] * decay
    hb = jnp.where(pos == 0, 0.0, h[:, w2o:2 * w2o] * decay)
    inv = 1.0 / (jnp.sum(jnp.abs(hf), axis=0, keepdims=True) + jnp.sum(jnp.abs(hb), axis=0, keepdims=True))
    hsum = (hf + hb) * inv
    hdif = (hf - hb) * inv
    sum_e, sum_o = _split_bf16(hsum[0:half, :]), _split_bf16(hsum[half:n, :])
    dif_e, dif_o = _split_bf16(hdif[0:half, :]), _split_bf16(hdif[half:n, :])

    def dot2(m_ref, r, parts):
        return _dot(m_ref[r, :], parts[0]) + _dot(m_ref[r, :], parts[1])

    ck = min(half, HY_CHUNK)
    for r0 in range(0, half, ck):
        r = slice(r0, r0 + ck)
        ec, oc = dot2(ce_ref, r, sum_e), dot2(co_ref, r, sum_o)
        es, os_ = dot2(se_ref, r, dif_e), dot2(so_ref, r, dif_o)
        k = lax.broadcasted_iota(jnp.int32, (ck, w2o), 0) + r0
        wk = jnp.where(k == 0, 1.0, 2.0) * (1.0 / (2 * n))
        hra_ref[r, :] = (ec + oc) * wk
        hsa_ref[r, :] = (es + os_) * wk
        hrb_ref[r, :] = (ec - oc) * wk
        hsb_ref[r, :] = (os_ - es) * wk
    alt = jnp.where(lax.broadcasted_iota(jnp.int32, (half, w2o), 0) % 2 == 0, 1.0, -1.0)
    hm_ref[0:1, :] = jnp.sum(hsum[0:half, :] * alt, axis=0, keepdims=True) * (1.0 / n)
    hm_ref[1:2, :] = jnp.sum(hdif[half:n, :] * alt, axis=0, keepdims=True) * (1.0 / n)


def _hyena_filter(feats, w1, b1, w2, b2, w3, freq, layer, deltas, dft):
    n = feats.shape[0]
    half = n // 2
    w2o = HY_ORDER * HY_W
    args = (feats, w1, b1, w2, b2, w3, freq, deltas) + tuple(dft)
    whole = lambda a: pl.BlockSpec(a.shape, lambda i: (0, 0))
    return pl.pallas_call(
        _hyena_filter_kernel,
        grid=(1,),
        in_specs=[whole(feats)] + [_layer_spec(a, layer) for a in args[1:7]] + [whole(a) for a in args[7:]],
        out_specs=[pl.BlockSpec((half, w2o), lambda i: (0, 0))] * 4 + [pl.BlockSpec((2, w2o), lambda i: (0, 0))],
        out_shape=[jax.ShapeDtypeStruct((half, w2o), F32)] * 4 + [jax.ShapeDtypeStruct((2, w2o), F32)],
        compiler_params=_cparams(("arbitrary",)), name="hyena_filter",
    )(*args)


def _hyena_kernel(u_ref, sw_ref, sb_ref, hra_ref, hsa_ref, hrb_ref, hsb_ref, hm_ref, d_ref,
                  ce_ref, se_ref, co_ref, so_ref, cot_ref, sot_ref, o_ref,
                  zfe_ref, zfo_ref, zbe_ref, zbo_ref, g1e_ref, g1o_ref, g2e_ref, g2o_ref,
                  p_ref, q_ref, p2_ref, q2_ref, ys_ref):
    half = u_ref.shape[0]
    w3 = (HY_ORDER + 1) * HY_W
    ck = min(half, HY_CHUNK)
    chunks = [slice(r, r + ck) for r in range(0, half, ck)]
    row = lax.broadcasted_iota(jnp.int32, (half, HY_W), 0)
    alt = jnp.where(row % 2 == 0, 1.0, -1.0)
    calt = jnp.where(lax.broadcasted_iota(jnp.int32, (ck, HY_W), 0) % 2 == 0, 1.0, -1.0)

    def short(j):
        c = slice(j * HY_W, (j + 1) * HY_W)
        ue = u_ref[:, j * HY_W:(j + 1) * HY_W].astype(F32)
        uo = u_ref[:, w3 + j * HY_W:w3 + (j + 1) * HY_W].astype(F32)
        uo_prev = jnp.where(row == 0, 0.0, pltpu.roll(uo, 1, 0))
        ue_next = jnp.where(row == half - 1, 0.0, pltpu.roll(ue, half - 1, 0))
        w0, w1, w2, b = sw_ref[0:1, c], sw_ref[1:2, c], sw_ref[2:3, c], sb_ref[0:1, c]
        return b + uo_prev * w0 + ue * w1 + uo * w2, b + ue * w0 + uo * w1 + ue_next * w2

    ve, vo = short(0)
    zfe_ref[...], zfo_ref[...] = ve, vo
    zbe_ref[...], zbo_ref[...] = ve.astype(BF16), vo.astype(BF16)
    g1e_ref[...], g1o_ref[...] = short(1)
    g2e_ref[...], g2o_ref[...] = short(2)
    gates = ((g1e_ref, g1o_ref), (g2e_ref, g2o_ref))

    for o in range(HY_ORDER):
        c = slice(o * HY_W, (o + 1) * HY_W)
        xr_m = jnp.sum(zfe_ref[...] * alt, axis=0, keepdims=True)
        xs_m = jnp.sum(zfo_ref[...] * alt, axis=0, keepdims=True)
        hr_m, hs_m = hm_ref[0:1, c], hm_ref[1:2, c]
        yr_m, ys_m = xr_m * hr_m - xs_m * hs_m, xr_m * hs_m + xs_m * hr_m
        for r in chunks:
            ec, es = _dot(ce_ref[r, :], zbe_ref[...]), _dot(se_ref[r, :], zbe_ref[...])
            oc, os_ = _dot(co_ref[r, :], zbo_ref[...]), _dot(so_ref[r, :], zbo_ref[...])
            xra, xsa, xrb, xsb = ec + oc, es + os_, ec - oc, os_ - es
            hra, hsa, hrb, hsb = hra_ref[r, c], hsa_ref[r, c], hrb_ref[r, c], hsb_ref[r, c]
            yra, ysa = xra * hra - xsa * hsa, xra * hsa + xsa * hra
            yrb, ysb = xrb * hrb - xsb * hsb, xrb * hsb + xsb * hrb
            p_ref[r, :] = (yra + yrb).astype(BF16)
            q_ref[r, :] = (ysa - ysb).astype(BF16)
            p2_ref[r, :] = (yra - yrb).astype(BF16)
            q2_ref[r, :] = (ysa + ysb).astype(BF16)
        ge_ref, go_ref = gates[o]
        for r in chunks:
            ye = _dot(ce_ref[r, :], p_ref[...]) + _dot(se_ref[r, :], q_ref[...]) + calt * yr_m
            yo = _dot(cot_ref[r, :], p2_ref[...]) + _dot(sot_ref[r, :], q2_ref[...]) + calt * ys_m
            ye = (ye + zfe_ref[r, :] * d_ref[o:o + 1, :]) * ge_ref[r, :]
            yo = (yo + zfo_ref[r, :] * d_ref[o:o + 1, :]) * go_ref[r, :]
            if o + 1 < HY_ORDER:
                zfe_ref[r, :], zfo_ref[r, :] = ye, yo
                zbe_ref[r, :], zbo_ref[r, :] = ye.astype(BF16), yo.astype(BF16)
            else:
                rows = slice(2 * r.start, 2 * r.start + 2 * ck)
                for j in range(HY_W // LANES):
                    lanes = slice(j * LANES, (j + 1) * LANES)
                    ys_ref[j, pl.ds(2 * r.start, ck, stride=2), :] = ye[:, lanes]
                    ys_ref[j, pl.ds(2 * r.start + 1, ck, stride=2), :] = yo[:, lanes]
                o_ref[rows, :] = jnp.concatenate([ys_ref[j, rows, :] for j in range(HY_W // LANES)],
                                                 axis=1).astype(o_ref.dtype)


def _hyena(hy, sw, sb, spec, dbias, layer, dft, n):
    m, w3 = 2 * hy.shape[0], hy.shape[1] // 2
    half = n // 2
    const = lambda b: (0, 0)
    once = lambda a: pl.BlockSpec(a.shape, const, pipeline_mode=pl.Buffered(1))
    f32s = pltpu.VMEM((half, HY_W), F32)
    bf16s = pltpu.VMEM((half, HY_W), BF16)
    return pl.pallas_call(
        _hyena_kernel,
        grid=(m // n,),
        in_specs=[pl.BlockSpec((half, 2 * w3), lambda b: (b, 0)),
                  _layer_spec(sw, layer), _layer_spec(sb, layer)]
                 + [once(a) for a in spec[:4]] + [pl.BlockSpec(spec[4].shape, const), _layer_spec(dbias, layer)]
                 + [once(a) for a in dft],
        out_specs=pl.BlockSpec((n, HY_W), lambda b: (b, 0)),
        out_shape=jax.ShapeDtypeStruct((m, HY_W), BF16),
        scratch_shapes=[f32s, f32s, bf16s, bf16s, f32s, f32s, f32s, f32s, bf16s, bf16s, bf16s, bf16s,
                        pltpu.VMEM((HY_W // LANES, n, LANES), F32)],
        compiler_params=_cparams(("arbitrary",)), name="hyena",
    )(hy, sw, sb, *spec, dbias, *dft)


def _fnet_kernel(u_ref, bc_ref, bs_ref, ce_ref, se_ref, co_ref, so_ref, o_ref):
    half = u_ref.shape[0]
    scale = 1.0 / math.sqrt(2 * half * FN_GROUP_W)
    ue, uo = u_ref[:, 0:FN_W], u_ref[:, FN_W:2 * FN_W]
    uce, use = _dot(ue, bc_ref[...]).astype(BF16), _dot(ue, bs_ref[...]).astype(BF16)
    uco, uso = _dot(uo, bc_ref[...]).astype(BF16), _dot(uo, bs_ref[...]).astype(BF16)
    ck = min(half, HY_CHUNK)
    for r0 in range(0, half, ck):
        r = slice(r0, r0 + ck)
        e = _dot(ce_ref[r, :], uce) - _dot(se_ref[r, :], use)
        o = _dot(co_ref[r, :], uco) - _dot(so_ref[r, :], uso)
        o_ref[r0:r0 + ck, :] = ((e + o) * scale).astype(o_ref.dtype)
        o_ref[half + r0:half + r0 + ck, :] = ((e - o) * scale).astype(o_ref.dtype)


def _fnet(fn, bc, bs, dft, n):
    m = 2 * fn.shape[0]
    half = n // 2
    const = lambda b: (0, 0)
    once = lambda a: pl.BlockSpec(a.shape, const, pipeline_mode=pl.Buffered(1))
    return pl.pallas_call(
        _fnet_kernel,
        grid=(m // n,),
        in_specs=[pl.BlockSpec((half, 2 * FN_W), lambda b: (b, 0)), pl.BlockSpec(bc.shape, const),
                  pl.BlockSpec(bs.shape, const)] + [once(a) for a in dft],
        out_specs=pl.BlockSpec((n, FN_W), lambda b: (b, 0)),
        out_shape=jax.ShapeDtypeStruct((m, FN_W), BF16),
        compiler_params=_cparams(("arbitrary",)), name="fnet",
    )(fn, bc, bs, *dft)


def _merge_kernel(x_ref, mod_ref, g1_ref, g2_ref, att_ref, hy_ref, fn_ref, wg_ref, wb_ref, wo_ref,
                  wrh_ref, wrl_ref, x1_ref, h2_ref, lg_ref):
    tm, d = x_ref.shape
    branches = ((att_ref, 0, ATT_Q_W), (hy_ref, ATT_Q_W, ATT_Q_W + HY_W), (fn_ref, ATT_Q_W + HY_W, MIX_W))
    rows = [slice(r, r + MERGE_CHUNK) for r in range(0, tm, MERGE_CHUNK)]
    xs = [x_ref[r, :] for r in rows]
    hbs = [_rms_mod(x, g1_ref[...], mod_ref[0:1, :], mod_ref[1:2, :]).astype(BF16) for x in xs]
    mixes = []
    for r, hb in zip(rows, hbs):
        mix = None
        for i, (b_ref, lo, hi) in enumerate(branches):
            gate = _dot(hb, wg_ref[:, OFF_GATE + i * d:OFF_GATE + (i + 1) * d])
            gate = 1.0 / (1.0 + jnp.exp(-gate))
            term = gate * _dot(b_ref[r, :], wb_ref[lo:hi, :])
            mix = term if mix is None else mix + term
        mixes.append(mix.astype(BF16))
    for r, x, mix in zip(rows, xs, mixes):
        y = _dot(mix, wo_ref[...])
        x1 = x + mod_ref[2:3, :] * y
        x1_ref[r, :] = x1
        h2 = _rms_mod(x1, g2_ref[...], mod_ref[3:4, :], mod_ref[4:5, :])
        hi, lo = _split_bf16(h2)
        h2_ref[r, :] = hi
        lg_ref[r, :] = _dot(hi, wrh_ref[...]) + _dot(lo, wrh_ref[...]) + _dot(hi, wrl_ref[...])


def _merge(x, mod, g1, g2, att, hyo, fno, wg, layer, wb, wo, wrh, wrl, n, tm):
    m, d = x.shape
    per_b = mod.shape[0] > 1
    tpb = n // tm
    row = lambda i: (i, 0)
    const = lambda i: (0, 0)
    mod_map = (lambda i: (i // tpb, 0, 0)) if per_b else (lambda i: (0, 0, 0))
    return pl.pallas_call(
        _merge_kernel,
        grid=(m // tm,),
        in_specs=[pl.BlockSpec((tm, d), row), pl.BlockSpec((None, 8, d), mod_map),
                  _layer_spec(g1, layer), _layer_spec(g2, layer),
                  pl.BlockSpec((tm, ATT_Q_W), row), pl.BlockSpec((tm, HY_W), row), pl.BlockSpec((tm, FN_W), row),
                  _layer_spec(wg, layer, pipeline_mode=pl.Buffered(1)),
                  _layer_spec(wb, layer), _layer_spec(wo, layer),
                  _layer_spec(wrh, layer), _layer_spec(wrl, layer)],
        out_specs=[pl.BlockSpec((tm, d), row), pl.BlockSpec((tm, d), row), pl.BlockSpec((tm, LANES), row)],
        out_shape=[jax.ShapeDtypeStruct((m, d), F32), jax.ShapeDtypeStruct((m, d), BF16),
                   jax.ShapeDtypeStruct((m, LANES), F32)],
        compiler_params=_cparams(("arbitrary",)), name="merge",
    )(x, mod, g1, g2, att, hyo, fno, wg, wb, wo, wrh, wrl)


def _route_kernel(lg_ref, tri_ref, rank_ref, aff_ref, rankt_ref, *, cap):
    ns, _, n = rank_ref.shape
    rows = ns * N_EXPERTS
    affs = []
    for s in range(ns):
        lt = lg_ref[s * n:(s + 1) * n, :].T[0:N_EXPERTS, :]
        e = jnp.exp(lt - jnp.max(lt, axis=0, keepdims=True))
        affs.append(e / jnp.sum(e, axis=0, keepdims=True))
    aff = jnp.concatenate(affs, axis=0) if ns > 1 else affs[0]
    bits = pltpu.bitcast(aff, jnp.int32)

    def step(i, thr):
        cand = thr | jnp.left_shift(jnp.int32(1), 30 - i)
        cnt = jnp.sum(jnp.where(bits >= cand, 1.0, 0.0), axis=1, keepdims=True)
        return jnp.where(cnt >= cap, cand, thr)

    thr = lax.fori_loop(0, 31, step, jnp.zeros((rows, 1), jnp.int32))
    gt = bits > thr
    eq = bits == thr
    need = cap - jnp.sum(jnp.where(gt, 1.0, 0.0), axis=1, keepdims=True)

    def excl_cumsum(mask):
        parts = []
        off = jnp.zeros((rows, 1), F32)
        for c in range(n // LANES):
            blk = mask[:, c * LANES:(c + 1) * LANES]
            parts.append(_dot(blk.astype(BF16), tri_ref[...]) + off)
            off = off + jnp.sum(blk, axis=1, keepdims=True)
        return jnp.concatenate(parts, axis=1)

    tie_rank = excl_cumsum(jnp.where(eq, 1.0, 0.0))
    sel = jnp.where(gt, 1.0, jnp.where(eq, jnp.where(tie_rank < need, 1.0, 0.0), 0.0))
    rank = jnp.where(sel > 0.5, excl_cumsum(sel), -1.0)
    pad = jnp.full((LANES - N_EXPERTS, n), -1.0, F32)
    for s in range(ns):
        r = rank[s * N_EXPERTS:(s + 1) * N_EXPERTS, :]
        rank_ref[s] = r.astype(jnp.int32)
        aff_ref[s] = affs[s]
        rankt_ref[s * n:(s + 1) * n, :] = jnp.concatenate([r, pad], axis=0).T.astype(BF16)


def _route(logits, tri, n, cap):
    m = logits.shape[0]
    b = m // n
    assert cap <= 256
    ns = math.gcd(b, ROUTE_SAMPLES)
    return pl.pallas_call(
        functools.partial(_route_kernel, cap=cap),
        grid=(b // ns,),
        in_specs=[pl.BlockSpec((ns * n, LANES), lambda i: (i, 0)), pl.BlockSpec(tri.shape, lambda i: (0, 0))],
        out_specs=[pl.BlockSpec((ns, N_EXPERTS, n), lambda i: (i, 0, 0)),
                   pl.BlockSpec((ns, N_EXPERTS, n), lambda i: (i, 0, 0)),
                   pl.BlockSpec((ns * n, LANES), lambda i: (i, 0))],
        out_shape=[jax.ShapeDtypeStruct((b, N_EXPERTS, n), jnp.int32),
                   jax.ShapeDtypeStruct((b, N_EXPERTS, n), F32),
                   jax.ShapeDtypeStruct((m, LANES), BF16)],
        compiler_params=_cparams(("arbitrary",)), name="route",
    )(logits, tri)


def _gather_kernel(h_ref, rank_ref, aff_ref, xg_ref, w_ref, sel_ref, *, cap):
    n, d = h_ref.shape
    slot = lax.broadcasted_iota(jnp.int32, (cap, n), 0)
    for e in range(N_EXPERTS):
        hit = slot == rank_ref[e:e + 1, :]
        sel_ref[e * cap:(e + 1) * cap, :] = jnp.where(hit, 1.0, 0.0).astype(BF16)
        wcol = jnp.sum(jnp.where(hit, aff_ref[e:e + 1, :], 0.0), axis=1, keepdims=True)
        w_ref[e] = jnp.broadcast_to(wcol, (cap, LANES))
    tn = 256
    for c in range(d // tn):
        xg = _dot(sel_ref[...], h_ref[:, c * tn:(c + 1) * tn]).astype(BF16)
        for e in range(N_EXPERTS):
            xg_ref[e, :, c * tn:(c + 1) * tn] = xg[e * cap:(e + 1) * cap, :]


def _gather(h2, rank, aff, n, cap):
    m, d = h2.shape
    b = m // n
    return pl.pallas_call(
        functools.partial(_gather_kernel, cap=cap),
        grid=(b,),
        in_specs=[pl.BlockSpec((n, d), lambda i: (i, 0)),
                  pl.BlockSpec((None, N_EXPERTS, n), lambda i: (i, 0, 0)),
                  pl.BlockSpec((None, N_EXPERTS, n), lambda i: (i, 0, 0))],
        out_specs=[pl.BlockSpec((N_EXPERTS, None, cap, d), lambda i: (0, i, 0, 0)),
                   pl.BlockSpec((N_EXPERTS, None, cap, LANES), lambda i: (0, i, 0, 0))],
        out_shape=[jax.ShapeDtypeStruct((N_EXPERTS, b, cap, d), BF16),
                   jax.ShapeDtypeStruct((N_EXPERTS, b, cap, LANES), F32)],
        scratch_shapes=[pltpu.VMEM((N_EXPERTS * cap, n), BF16)],
        compiler_params=_cparams(("arbitrary",)), name="moe_gather",
    )(h2, rank, aff)


def _ffn_kernel(*refs, extra):
    if extra:
        x_ref, w_ref, xc_ref, wc_ref, wg_ref, wu_ref, wd_ref, o_ref, oc_ref, acc_ref, accc_ref = refs
    else:
        x_ref, w_ref, wg_ref, wu_ref, wd_ref, o_ref, acc_ref = refs
    f = pl.program_id(2)

    def run(x_ref, w_ref, o_ref, acc_ref):
        @pl.when((pl.program_id(0) == 0) & (pl.program_id(1) == 0) & (f == 0))
        def _():
            acc_ref[...] = jnp.zeros_like(acc_ref)

        x = x_ref[...]
        d = acc_ref.shape[1]
        tf = 512
        hms, wds = [], []
        for c in range(wg_ref.shape[1] // tf):
            cols = slice(c * tf, (c + 1) * tf)
            a = _dot(x, wg_ref[:, cols].astype(BF16))
            u = _dot(x, wu_ref[:, cols].astype(BF16))
            hms.append((a * (1.0 / (1.0 + jnp.exp(-a))) * u).astype(BF16))
            wds.append(wd_ref[cols, :].astype(BF16))
        tn = 256
        for c in range(d // tn):
            cols = slice(c * tn, (c + 1) * tn)
            t = sum(_dot(hm, wd[:, cols]) for hm, wd in zip(hms, wds))
            s = jnp.where(f > 0, acc_ref[:, cols], 0.0) + t
            acc_ref[:, cols] = s
            o_ref[:, cols] = (s * w_ref[...] if tn == LANES else
                              s * jnp.concatenate([w_ref[...]] * (tn // LANES), axis=1)).astype(o_ref.dtype)

    run(x_ref, w_ref, o_ref, acc_ref)
    if extra:
        @pl.when(pl.program_id(1) == 0)
        def _():
            run(xc_ref, wc_ref, oc_ref, accc_ref)


def _ffn(xg, w, xgc, wc, wg, wu, wd, layer, tm, tf):
    ne, rows, d = xg.shape
    ff = wg.shape[3]
    extra = xgc is not None
    tile = lambda e, i, f: (e, i, 0)
    whole = lambda e, i, f: (e, 0, 0)
    args = [xg, w]
    in_specs = [pl.BlockSpec((None, tm, d), tile), pl.BlockSpec((None, tm, LANES), tile)]
    out_specs = [pl.BlockSpec((None, tm, d), tile)]
    out_shape = [jax.ShapeDtypeStruct((ne, rows, d), BF16)]
    scratch = [pltpu.VMEM((tm, d), F32)]
    if extra:
        rc = xgc.shape[1]
        args += [xgc, wc]
        in_specs += [pl.BlockSpec((None, rc, d), whole), pl.BlockSpec((None, rc, LANES), whole)]
        out_specs.append(pl.BlockSpec((None, rc, d), whole))
        out_shape.append(jax.ShapeDtypeStruct((ne, rc, d), BF16))
        scratch.append(pltpu.VMEM((rc, d), F32))
    args += [wg, wu, wd]
    in_specs += [pl.BlockSpec((None, None, d, tf), lambda e, i, f: (layer, e, 0, f)),
                 pl.BlockSpec((None, None, d, tf), lambda e, i, f: (layer, e, 0, f)),
                 pl.BlockSpec((None, None, tf, d), lambda e, i, f: (layer, e, f, 0))]
    outs = pl.pallas_call(
        functools.partial(_ffn_kernel, extra=extra),
        grid=(ne, rows // tm, ff // tf),
        in_specs=in_specs, out_specs=out_specs, out_shape=out_shape, scratch_shapes=scratch,
        compiler_params=_cparams(("arbitrary", "arbitrary", "arbitrary")), name="moe_ffn",
    )(*args)
    return (outs[0], outs[1]) if extra else (outs[0], None)


def _scatter_kernel(rankt_ref, y_ref, expand_ref, slot_ref, o_ref, selt_ref, *, cap):
    n = rankt_ref.shape[0]
    d = o_ref.shape[1]
    width = N_EXPERTS * cap
    if cap % LANES == 0:
        rank = rankt_ref[...].astype(F32)
        slot = lax.broadcasted_iota(jnp.int32, (n, cap), 1).astype(F32)
        for e in range(N_EXPERTS):
            r = jnp.broadcast_to(rank[:, e:e + 1], (n, cap))
            selt_ref[:, e * cap:(e + 1) * cap] = jnp.where(r == slot, 1.0, 0.0).astype(BF16)
    else:
        r = _dot(rankt_ref[...], expand_ref[...])
        selt_ref[...] = jnp.where(r == slot_ref[...], 1.0, 0.0).astype(BF16)
    y = y_ref[...].reshape(width, d)
    tn = 256
    for c in range(d // tn):
        o_ref[:, c * tn:(c + 1) * tn] = _dot(selt_ref[...], y[:, c * tn:(c + 1) * tn])


def _scatter(rankt, yw, expand, slotpat, n, cap):
    ne, b, _, d = yw.shape
    return pl.pallas_call(
        functools.partial(_scatter_kernel, cap=cap),
        grid=(b,),
        in_specs=[pl.BlockSpec((n, LANES), lambda i: (i, 0)),
                  pl.BlockSpec((ne, None, cap, d), lambda i: (0, i, 0, 0)),
                  pl.BlockSpec(expand.shape, lambda i: (0, 0)),
                  pl.BlockSpec(slotpat.shape, lambda i: (0, 0))],
        out_specs=pl.BlockSpec((n, d), lambda i: (i, 0)),
        out_shape=jax.ShapeDtypeStruct((b * n, d), F32),
        scratch_shapes=[pltpu.VMEM((n, ne * cap), BF16)],
        compiler_params=_cparams(("arbitrary",)), name="moe_scatter",
    )(rankt, yw, expand, slotpat)


def _moe(routed, routed_extra, wg, wu, wd, layer):
    def flat(r):
        rows = r["b"] * r["cap"]
        return r["xg"].reshape(N_EXPERTS, rows, D_MODEL), r["w"].reshape(N_EXPERTS, rows, LANES)

    def scatter(r, yw):
        yw = yw.reshape(N_EXPERTS, r["b"], r["cap"], D_MODEL)
        return _scatter(r["rankt"], yw, r["tabs"]["expand"], r["tabs"]["slotpat"], r["n"], r["cap"])

    xg, w = flat(routed)
    xgc, wc = flat(routed_extra) if routed_extra is not None else (None, None)
    yw, ywc = _ffn(xg, w, xgc, wc, wg, wu, wd, layer, min(xg.shape[1], FFN_ROW_TILE), FFN_HIDDEN_TILE)
    return scatter(routed, yw), (scatter(routed_extra, ywc) if routed_extra is not None else None)


def _combine_kernel(x_ref, y_ref, mod_ref, g_ref, o_ref, *, norm):
    x = x_ref[...] + mod_ref[5:6, :] * y_ref[...]
    if norm:
        x = x * lax.rsqrt(jnp.mean(x * x, axis=-1, keepdims=True) + EPS) * g_ref[...]
    o_ref[...] = x


def _combine(x, y, mod, g, n, tm, norm):
    m, d = x.shape
    per_b = mod.shape[0] > 1
    tpb = n // tm
    row = lambda i: (i, 0)
    mod_map = (lambda i: (i // tpb, 0, 0)) if per_b else (lambda i: (0, 0, 0))
    return pl.pallas_call(
        functools.partial(_combine_kernel, norm=norm),
        grid=(m // tm,),
        in_specs=[pl.BlockSpec((tm, d), row), pl.BlockSpec((tm, d), row),
                  pl.BlockSpec((None, 8, d), mod_map), pl.BlockSpec((1, d), lambda i: (0, 0))],
        out_specs=pl.BlockSpec((tm, d), row),
        out_shape=jax.ShapeDtypeStruct((m, d), F32),
        compiler_params=_cparams(("arbitrary",)), name="combine",
    )(x, y, mod, g)


def _fnet_dft_tables(n):
    j = jnp.arange(n // 2, dtype=jnp.int32)
    even = ((j[:, None] * (2 * j[None, :])) % n).astype(F32) * (2.0 * math.pi / n)
    odd = ((j[:, None] * (2 * j[None, :] + 1)) % n).astype(F32) * (2.0 * math.pi / n)
    return tuple(a.astype(BF16) for a in (jnp.cos(even), jnp.sin(even), jnp.cos(odd), jnp.sin(odd)))


def _rope_tables(n):
    rows = n // GRID_W
    row = jnp.repeat(jnp.arange(rows, dtype=F32), GRID_W)
    col = jnp.tile(jnp.arange(GRID_W, dtype=F32), rows)
    inv = ROPE_THETA ** (-jnp.arange(0, ROPE_AXIS_DIM, 2, dtype=F32) / ROPE_AXIS_DIM)
    ang = jnp.concatenate([row[:, None] * inv, col[:, None] * inv], axis=-1)
    cos = jnp.repeat(jnp.cos(ang), 2, axis=1)
    sin = jnp.repeat(jnp.sin(ang), 2, axis=1) * jnp.tile(jnp.array([-1.0, 1.0], F32), HEAD_DIM // 2)
    return jnp.tile(cos, (1, LANES // HEAD_DIM)), jnp.tile(sin, (1, LANES // HEAD_DIM))


def _hyena_dft_tables(n):
    j = jnp.arange(n // 2, dtype=jnp.int32)
    even = ((j[:, None] * j[None, :]) % n).astype(F32) * (2.0 * math.pi / n)
    odd = ((j[:, None] * (2 * j[None, :] + 1)) % (2 * n)).astype(F32) * (math.pi / n)
    ce, se, co, so = jnp.cos(even), jnp.sin(even), jnp.cos(odd), jnp.sin(odd)
    return tuple(a.astype(BF16) for a in (ce, se, co, so, co.T, so.T))


def _hyena_feats(n):
    pos = jnp.concatenate([jnp.arange(0, n, 2), jnp.arange(1, n, 2)]).astype(F32)
    t = pos / (n - 1)
    bands = jnp.linspace(1e-4, HY_BANDS - 1, HY_BANDS, dtype=F32)
    ang = (2.0 * math.pi / n) * pos[:, None] * bands[None, :]
    feats = jnp.concatenate([t[:, None], jnp.cos(ang), -jnp.sin(ang)], axis=-1)
    return jnp.pad(feats, ((0, 0), (0, LANES - HY_EMB)))


def _side(x, xb, n, mod, l, P, tabs, kv_ext, rope, tm, last_ctx):
    b = x.shape[0] // n
    outs = _inproj(x, xb, mod, P["norm1_g"], P["w_in"], l, tabs["cos"] if rope else None, tabs["sin"] if rope else None,
                   P["qg"], P["kg"], P["gmat"], n, tm)
    if xb is not None:
        x, outs = outs[0], outs[1:]
    q, k, v, hy, fn = outs
    k3, v3 = k.reshape(b, n, ATT_KV_W), v.reshape(b, n, ATT_KV_W)
    if last_ctx:
        return None, None, k3, v3
    if kv_ext is not None:
        kc, vc = jnp.concatenate([k3, kv_ext[0]], axis=1), jnp.concatenate([v3, kv_ext[1]], axis=1)
    else:
        kc, vc = k3, v3
    s = kc.shape[1]
    fill = jnp.concatenate([jnp.ones((b, s, 1), BF16), jnp.zeros((b, s, LANES - HEAD_DIM - 1), BF16)], axis=-1)
    vaug = jnp.stack([jnp.concatenate([vc[..., g * HEAD_DIM:(g + 1) * HEAD_DIM], fill], axis=-1)
                      for g in range(N_KV_HEADS)], axis=1)
    att = _attention(q, jnp.swapaxes(kc, 1, 2), vaug, n, min(n, ATTN_Q_TILE))
    spec = _hyena_filter(tabs["feats"], P["hy_w1"], P["hy_b1"], P["hy_w2"], P["hy_b2"],
                         P["hy_w3"], P["hy_freq"], l, tabs["deltas"], tabs["hy_dft"][:4])
    hyo = _hyena(hy, P["hy_sw"], P["hy_sb"], spec, P["hy_bias"], l, tabs["hy_dft"], n)
    fno = _fnet(fn, tabs["bc"], tabs["bs"], tabs["fn_dft"], n)
    x1, h2, logits = _merge(x, mod, P["norm1_g"], P["norm2_g"], att, hyo, fno, P["w_in"], l,
                            P["w_branch"], P["w_out"], P["wr_hi"], P["wr_lo"], n, tm)
    cap = CAPACITY_FACTOR * n // N_EXPERTS
    rank, aff, rankt = _route(logits, tabs["tri"], n, cap)
    xg, w = _gather(h2, rank, aff, n, cap)
    return x1, dict(xg=xg, w=w, rankt=rankt, tabs=tabs, n=n, cap=cap, b=b), k3, v3


def _tables(n, rope):
    a = jnp.arange(FN_W, dtype=jnp.int32)
    same = (a[:, None] // FN_GROUP_W) == (a[None, :] // FN_GROUP_W)
    ang = ((a[:, None] * a[None, :]) % FN_GROUP_W).astype(F32) * (2.0 * math.pi / FN_GROUP_W)
    deltas = jnp.abs(jnp.linspace(math.log(HY_DECAY_TARGET) / HY_DECAY_LONG_PCT,
                                  math.log(HY_DECAY_TARGET) / HY_DECAY_SHORT_PCT, HY_W, dtype=F32))
    i = jnp.arange(LANES, dtype=jnp.int32)
    cap = CAPACITY_FACTOR * n // N_EXPERTS
    j = jnp.arange(N_EXPERTS * cap, dtype=jnp.int32)
    tabs = dict(expand=(i[:, None] == j[None, :] // cap).astype(BF16),
                slotpat=(j % cap).astype(F32)[None, :],
                hy_dft=_hyena_dft_tables(n), fn_dft=_fnet_dft_tables(n),
                bc=jnp.where(same, jnp.cos(ang), 0.0).astype(BF16),
                bs=jnp.where(same, jnp.sin(ang), 0.0).astype(BF16),
                feats=_hyena_feats(n), deltas=jnp.tile(deltas, HY_ORDER)[None, :],
                tri=(i[:, None] < i[None, :]).astype(BF16))
    if rope:
        tabs["cos"], tabs["sin"] = _rope_tables(n)
    return tabs


def kernel(x, c, ctx, c_ctx, w_mod, b_mod, norm1_g, norm2_g, w_in, q_gain, k_gain, hy_short_w, hy_short_b,
           hy_f_w1, hy_f_b1, hy_f_w2, hy_f_b2, hy_f_w3, hy_f_freq, hy_bias, w_branch, w_out, w_router,
           w_gate, w_up, w_down, final_g):
    bsz, n_lat, d = x.shape
    n_ctx = ctx.shape[1]
    depth = w_mod.shape[0]
    assert d == D_MODEL and n_lat % LANES == 0 and n_ctx % LANES == 0

    hid = jnp.arange(ATT_Q_W, dtype=jnp.int32) // HEAD_DIM
    P = dict(
        norm1_g=norm1_g[:, None, :], norm2_g=norm2_g[:, None, :],
        w_in=w_in.astype(BF16),
        qg=jnp.tile(q_gain, (1, N_Q_HEADS))[:, None, :], kg=jnp.tile(k_gain, (1, N_KV_HEADS))[:, None, :],
        gmat=(hid[:, None] == hid[None, :]).astype(BF16),
        hy_sw=hy_short_w, hy_sb=hy_short_b[:, None, :],
        hy_w1=jnp.pad(hy_f_w1, ((0, 0), (0, LANES - HY_EMB), (0, 0))), hy_b1=hy_f_b1[:, None, :],
        hy_w2=hy_f_w2, hy_b2=hy_f_b2[:, None, :], hy_w3=hy_f_w3, hy_freq=hy_f_freq[:, None, :],
        hy_bias=hy_bias,
        w_branch=w_branch.astype(BF16), w_out=w_out.astype(BF16),
        w_gate=w_gate, w_up=w_up, w_down=w_down,
    )
    wr = jnp.pad(w_router, ((0, 0), (0, 0), (0, LANES - N_EXPERTS)))
    P["wr_hi"] = wr.astype(BF16)
    P["wr_lo"] = (wr - P["wr_hi"].astype(F32)).astype(BF16)

    tab_x = _tables(n_lat, True)
    tab_c = _tables(n_ctx, False)

    rows = -(-(bsz + 1) // 8) * 8
    c_all = jnp.concatenate([c, c_ctx[None, :], jnp.zeros((rows - bsz - 1, d), F32)], axis=0)
    mod = _modulation(c_all, w_mod, b_mod).reshape(depth, rows, 6, d)
    mod = jnp.pad(mod, ((0, 0), (0, 0), (0, 2), (0, 0)))

    tm_x, tm_c = min(n_lat, ROW_TILE), min(n_ctx, ROW_TILE)
    xs, xpend = x.reshape(bsz * n_lat, d), None
    cs, cpend = ctx.reshape(bsz * n_ctx, d), None
    for l in range(depth):
        last = l == depth - 1
        mod_x, mod_c = mod[l, :bsz], mod[l, bsz:bsz + 1]
        pm_x = None if l == 0 else jnp.concatenate([mod_x[:, :5], mod[l - 1, :bsz, 5:6], mod_x[:, 6:]], axis=1)
        pm_c = None if l == 0 else jnp.concatenate([mod_c[:, :5], mod[l - 1, bsz:bsz + 1, 5:6], mod_c[:, 6:]],
                                                   axis=1)
        c1, rc, kc, vc = _side(cs, cpend, n_ctx, mod_c if l == 0 else pm_c, l, P, tab_c, None, False, tm_c, last)
        x1, rx, _, _ = _side(xs, xpend, n_lat, mod_x if l == 0 else pm_x, l, P, tab_x, (kc, vc), True, tm_x, False)
        xy, cy = _moe(rx, rc, P["w_gate"], P["w_up"], P["w_down"], l)
        xs, xpend = x1, xy
        if not last:
            cs, cpend = c1, cy
    out = _combine(xs, xpend, mod[depth - 1, :bsz], final_g[None, :], n_lat, tm_x, True)
    return out.reshape(bsz, n_lat, d)
```

```python
import functools
import math

import jax
import jax.numpy as jnp
from jax import lax
from jax.experimental import pallas as pl
from jax.experimental.pallas import tpu as pltpu

F32 = jnp.float32
BF16 = jnp.bfloat16

D_MODEL = 1024
GRID_W = 64
HEAD_DIM = 64
N_Q_HEADS = 8
N_KV_HEADS = 2
Q_GROUP = N_Q_HEADS // N_KV_HEADS
ATT_Q_W = N_Q_HEADS * HEAD_DIM
ATT_KV_W = N_KV_HEADS * HEAD_DIM
ROPE_THETA = 10000.0
ROPE_AXIS_DIM = HEAD_DIM // 2
HY_W = D_MODEL // 4
HY_ORDER = 2
HY_SHORT = 3
HY_BANDS = 16
HY_EMB = 2 * HY_BANDS + 1
HY_FFN = 64
HY_DECAY_TARGET = 1e-2
HY_DECAY_SHORT_PCT = 0.3
HY_DECAY_LONG_PCT = 1.5
FN_GROUPS = 4
FN_GROUP_W = D_MODEL // 16
FN_W = FN_GROUPS * FN_GROUP_W
MIX_W = ATT_Q_W + HY_W + FN_W
N_BRANCH = 3
OFF_Q = 0
OFF_K = OFF_Q + ATT_Q_W
OFF_V = OFF_K + ATT_KV_W
OFF_HY = OFF_V + ATT_KV_W
OFF_FN = OFF_HY + (HY_ORDER + 1) * HY_W
OFF_GATE = OFF_FN + FN_W
N_EXPERTS = 16
CAPACITY_FACTOR = 2
EPS = 1e-6

LANES = 128
VMEM_LIMIT = 56 * 1024 * 1024
ROW_TILE = 512
ATTN_Q_TILE = 256
FFN_ROW_TILE = 1024
FFN_HIDDEN_TILE = 1024
HY_CHUNK = 512
ROUTE_SAMPLES = 4
INPROJ_CHUNK = 256
MERGE_CHUNK = 256
ATTN_HEADS_PER_CHAIN = 4

A_Q, A_K, A_V, A_HY, A_FN, A_END = 0, 512, 640, 768, 1536, 1792


def _cparams(sem):
    return pltpu.CompilerParams(dimension_semantics=sem, vmem_limit_bytes=VMEM_LIMIT)


def _dot(a, b):
    return jnp.dot(a, b, preferred_element_type=F32)


def _split_bf16(x):
    hi = x.astype(BF16)
    lo = (x - hi.astype(F32)).astype(BF16)
    return hi, lo


def _layer_spec(arr, layer, **kw):
    zeros = (0,) * (arr.ndim - 1)
    return pl.BlockSpec((None,) + arr.shape[1:], lambda *_: (layer,) + zeros, **kw)


def _rms_mod(x, g, shift, scale):
    y = x * lax.rsqrt(jnp.mean(x * x, axis=-1, keepdims=True) + EPS)
    return (y * g) * (1.0 + scale) + shift


def _mod_kernel(c_ref, w_ref, b_ref, o_ref):
    c = c_ref[...]
    sc = c * (1.0 / (1.0 + jnp.exp(-c)))
    o_ref[...] = jnp.dot(sc, w_ref[...], preferred_element_type=F32,
                         precision=lax.Precision.HIGHEST) + b_ref[...]


def _modulation(c_all, w_mod, b_mod):
    depth, d, n6 = w_mod.shape
    rows = c_all.shape[0]
    tn = 1536
    return pl.pallas_call(
        _mod_kernel,
        grid=(depth, n6 // tn),
        in_specs=[pl.BlockSpec((rows, d), lambda l, j: (0, 0)),
                  pl.BlockSpec((None, d, tn), lambda l, j: (l, 0, j)),
                  pl.BlockSpec((None, 1, tn), lambda l, j: (l, 0, j))],
        out_specs=pl.BlockSpec((None, rows, tn), lambda l, j: (l, 0, j)),
        out_shape=jax.ShapeDtypeStruct((depth, rows, n6), F32),
        compiler_params=_cparams(("arbitrary", "arbitrary")),
        name="modulation",
    )(c_all, w_mod, b_mod.reshape(depth, 1, n6))


def _head_norm(u, gmat_ref, width):
    gm = gmat_ref[0:width, 0:width]
    ms = _dot((u * u).astype(BF16), gm) * (1.0 / HEAD_DIM)
    return lax.rsqrt(ms + EPS)


def _inproj_kernel(*refs, rope, combine):
    it = iter(refs)
    x_ref = next(it)
    if combine:
        xb_ref = next(it)
    mod_ref = next(it)
    g_ref = next(it)
    w_ref = next(it)
    if rope:
        cos_ref = next(it)
        sin_ref = next(it)
    qg_ref = next(it)
    kg_ref = next(it)
    gmat_ref = next(it)
    if combine:
        xo_ref = next(it)
    q_ref = next(it)
    k_ref = next(it)
    v_ref = next(it)
    hy_ref = next(it)
    fn_ref = next(it)
    hs_ref = next(it)

    tm = x_ref.shape[0]
    ck = min(tm, INPROJ_CHUNK)
    chunks = [slice(r0, r0 + ck) for r0 in range(0, tm, ck)]
    hbs, hps = [], []
    for rows in chunks:
        x = x_ref[rows, :]
        if combine:
            x = x + mod_ref[5:6, :] * xb_ref[rows, :]
            xo_ref[rows, :] = x
        h = _rms_mod(x, g_ref[...], mod_ref[0:1, :], mod_ref[1:2, :])
        hbs.append(h.astype(BF16))
        parts = []
        for j in range(h.shape[1] // LANES):
            hs_ref[j, rows, :] = h[:, j * LANES:(j + 1) * LANES]
            he = hs_ref[j, pl.ds(rows.start, ck // 2, stride=2), :]
            ho = hs_ref[j, pl.ds(rows.start + 1, ck // 2, stride=2), :]
            parts.append(jnp.concatenate([he, ho], axis=0))
        hps.append(jnp.concatenate(parts, axis=1).astype(BF16))

    def qk(rows, hb, lo, width, gain_ref, out_ref, scale):
        u = _dot(hb, w_ref[:, lo:lo + width])
        r = _head_norm(u, gmat_ref, width)
        un = u * r * gain_ref[0:1, :]
        if rope:
            even = lax.broadcasted_iota(jnp.int32, (un.shape[0], LANES), 1) % 2 == 0
            parts = []
            for j in range(width // LANES):
                s = un[:, j * LANES:(j + 1) * LANES]
                sw = jnp.where(even, pltpu.roll(s, LANES - 1, 1), pltpu.roll(s, 1, 1))
                parts.append(s * cos_ref[rows, :] + sw * sin_ref[rows, :])
            un = jnp.concatenate(parts, axis=1) if len(parts) > 1 else parts[0]
        out_ref[rows, :] = (un * scale).astype(out_ref.dtype)

    for rows, hb, hp in zip(chunks, hbs, hps):
        qk(rows, hb, A_Q, ATT_Q_W, qg_ref, q_ref, HEAD_DIM ** -0.5 * math.log2(math.e))
        qk(rows, hb, A_K, ATT_KV_W, kg_ref, k_ref, 1.0)
        v_ref[rows, :] = _dot(hb, w_ref[:, A_V:A_HY]).astype(v_ref.dtype)
        pairs = slice(rows.start // 2, rows.start // 2 + ck // 2)
        for out_ref, lo, hi in ((hy_ref, A_HY, A_FN), (fn_ref, A_FN, A_END)):
            u = _dot(hp, w_ref[:, lo:hi]).astype(out_ref.dtype)
            out_ref[pairs, 0:hi - lo] = u[0:ck // 2, :]
            out_ref[pairs, hi - lo:2 * (hi - lo)] = u[ck // 2:ck, :]


def _inproj(x, xb, mod, norm_g, w_in, layer, cos_t, sin_t, qg, kg, gmat, n, tm):
    m, d = x.shape
    rope = cos_t is not None
    combine = xb is not None
    per_b = mod.shape[0] > 1
    tpb = n // tm
    row = lambda i: (i, 0)
    const = lambda i: (0, 0)
    mod_map = (lambda i: (i // tpb, 0, 0)) if per_b else (lambda i: (0, 0, 0))
    args, specs = [x], [pl.BlockSpec((tm, d), row)]
    if combine:
        args.append(xb)
        specs.append(pl.BlockSpec((tm, d), row))
    args += [mod, norm_g, w_in]
    specs += [pl.BlockSpec((None, 8, d), mod_map), _layer_spec(norm_g, layer),
              pl.BlockSpec((None, d, A_END), lambda i: (layer, 0, 0))]
    if rope:
        args += [cos_t, sin_t]
        specs += [pl.BlockSpec((tm, LANES), lambda i: (i % tpb, 0)),
                  pl.BlockSpec((tm, LANES), lambda i: (i % tpb, 0))]
    args += [qg, kg, gmat]
    specs += [_layer_spec(qg, layer), _layer_spec(kg, layer), pl.BlockSpec(gmat.shape, const)]
    widths = [ATT_Q_W, ATT_KV_W, ATT_KV_W]
    out_shape = [jax.ShapeDtypeStruct((m, w), BF16) for w in widths]
    out_specs = [pl.BlockSpec((tm, w), row) for w in widths]
    for w in ((HY_ORDER + 1) * HY_W, FN_W):
        out_shape.append(jax.ShapeDtypeStruct((m // 2, 2 * w), BF16))
        out_specs.append(pl.BlockSpec((tm // 2, 2 * w), row))
    if combine:
        out_shape.insert(0, jax.ShapeDtypeStruct((m, d), F32))
        out_specs.insert(0, pl.BlockSpec((tm, d), row))
    return pl.pallas_call(
        functools.partial(_inproj_kernel, rope=rope, combine=combine),
        grid=(m // tm,), in_specs=specs, out_specs=out_specs, out_shape=out_shape,
        scratch_shapes=[pltpu.VMEM((d // LANES, tm, LANES), F32)],
        compiler_params=_cparams(("arbitrary",)), name="inproj",
    )(*args)


def _attn_kernel(q_ref, kt_ref, v_ref, o_ref):
    tq = q_ref.shape[0]
    hpc = ATTN_HEADS_PER_CHAIN
    chains = [(h0 // Q_GROUP, range(h0, h0 + hpc)) for h0 in range(0, N_Q_HEADS, hpc)]
    ss = []
    for g, heads in chains:
        qg = jnp.concatenate([q_ref[:, h * HEAD_DIM:(h + 1) * HEAD_DIM] for h in heads], axis=0)
        ss.append(_dot(qg, kt_ref[g * HEAD_DIM:(g + 1) * HEAD_DIM, :]))
    outs = []
    for (g, heads), s in zip(chains, ss):
        p = jnp.exp2(s - jnp.max(s, axis=-1, keepdims=True)).astype(BF16)
        o = _dot(p, v_ref[g])
        for j in range(hpc):
            oh = o[j * tq:(j + 1) * tq, :]
            outs.append(oh[:, 0:HEAD_DIM] / oh[:, HEAD_DIM:HEAD_DIM + 1])
    o_ref[...] = jnp.concatenate(outs, axis=1).astype(o_ref.dtype)


def _attention(q, kt, vaug, n, tq):
    m = q.shape[0]
    b, _, s = kt.shape
    tpb = n // tq
    return pl.pallas_call(
        _attn_kernel,
        grid=(b, tpb),
        in_specs=[pl.BlockSpec((tq, ATT_Q_W), lambda bi, i: (bi * tpb + i, 0)),
                  pl.BlockSpec((None, ATT_KV_W, s), lambda bi, i: (bi, 0, 0)),
                  pl.BlockSpec((None, N_KV_HEADS, s, LANES), lambda bi, i: (bi, 0, 0, 0))],
        out_specs=pl.BlockSpec((tq, ATT_Q_W), lambda bi, i: (bi * tpb + i, 0)),
        out_shape=jax.ShapeDtypeStruct((m, ATT_Q_W), BF16),
        compiler_params=_cparams(("arbitrary", "arbitrary")), name="attention",
    )(q, kt, vaug)


def _hyena_filter_kernel(feat_ref, w1_ref, b1_ref, w2_ref, b2_ref, w3_ref, freq_ref, delta_ref,
                         ce_ref, se_ref, co_ref, so_ref, hra_ref, hsa_ref, hrb_ref, hsb_ref, hm_ref):
    n = feat_ref.shape[0]
    half = n // 2
    hp = lax.Precision.HIGHEST
    freq = freq_ref[...]
    h = jnp.sin(freq * (jnp.dot(feat_ref[...], w1_ref[...], preferred_element_type=F32, precision=hp)
                        + b1_ref[...]))
    h = jnp.sin(freq * (jnp.dot(h, w2_ref[...], preferred_element_type=F32, precision=hp) + b2_ref[...]))
    h = jnp.dot(h, w3_ref[...], preferred_element_type=F32, precision=hp)
    w2o = HY_ORDER * HY_W
    row = lax.broadcasted_iota(jnp.int32, (n, w2o), 0)
    pos = jnp.where(row < half, 2 * row, 2 * (row - half) + 1)
    t = pos.astype(F32) / (n - 1)
    decay = jnp.exp(-t * delta_ref[...])
    hf = h[:, 0:w2o] * decay
    hb = jnp.where(pos == 0, 0.0, h[:, w2o:2 * w2o] * decay)
    inv = 1.0 / (jnp.sum(jnp.abs(hf), axis=0, keepdims=True) + jnp.sum(jnp.abs(hb), axis=0, keepdims=True))
    hsum = (hf + hb) * inv
    hdif = (hf - hb) * inv
    sum_e, sum_o = _split_bf16(hsum[0:half, :]), _split_bf16(hsum[half:n, :])
    dif_e, dif_o = _split_bf16(hdif[0:half, :]), _split_bf16(hdif[half:n, :])

    def dot2(m_ref, r, parts):
        return _dot(m_ref[r, :], parts[0]) + _dot(m_ref[r, :], parts[1])

    ck = min(half, HY_CHUNK)
    for r0 in range(0, half, ck):
        r = slice(r0, r0 + ck)
        ec, oc = dot2(ce_ref, r, sum_e), dot2(co_ref, r, sum_o)
        es, os_ = dot2(se_ref, r, dif_e), dot2(so_ref, r, dif_o)
        k = lax.broadcasted_iota(jnp.int32, (ck, w2o), 0) + r0
        wk = jnp.where(k == 0, 1.0, 2.0) * (1.0 / (2 * n))
        hra_ref[r, :] = (ec + oc) * wk
        hsa_ref[r, :] = (es + os_) * wk
        hrb_ref[r, :] = (ec - oc) * wk
        hsb_ref[r, :] = (os_ - es) * wk
    alt = jnp.where(lax.broadcasted_iota(jnp.int32, (half, w2o), 0) % 2 == 0, 1.0, -1.0)
    hm_ref[0:1, :] = jnp.sum(hsum[0:half, :] * alt, axis=0, keepdims=True) * (1.0 / n)
    hm_ref[1:2, :] = jnp.sum(hdif[half:n, :] * alt, axis=0, keepdims=True) * (1.0 / n)


def _hyena_filter(feats, w1, b1, w2, b2, w3, freq, layer, deltas, dft):
    n = feats.shape[0]
    half = n // 2
    w2o = HY_ORDER * HY_W
    args = (feats, w1, b1, w2, b2, w3, freq, deltas) + tuple(dft)
    whole = lambda a: pl.BlockSpec(a.shape, lambda i: (0, 0))
    return pl.pallas_call(
        _hyena_filter_kernel,
        grid=(1,),
        in_specs=[whole(feats)] + [_layer_spec(a, layer) for a in args[1:7]] + [whole(a) for a in args[7:]],
        out_specs=[pl.BlockSpec((half, w2o), lambda i: (0, 0))] * 4 + [pl.BlockSpec((2, w2o), lambda i: (0, 0))],
        out_shape=[jax.ShapeDtypeStruct((half, w2o), F32)] * 4 + [jax.ShapeDtypeStruct((2, w2o), F32)],
        compiler_params=_cparams(("arbitrary",)), name="hyena_filter",
    )(*args)


def _hyena_kernel(u_ref, sw_ref, sb_ref, hra_ref, hsa_ref, hrb_ref, hsb_ref, hm_ref, d_ref,
                  ce_ref, se_ref, co_ref, so_ref, cot_ref, sot_ref, o_ref,
                  zfe_ref, zfo_ref, zbe_ref, zbo_ref, g1e_ref, g1o_ref, g2e_ref, g2o_ref,
                  p_ref, q_ref, p2_ref, q2_ref, ys_ref):
    half = u_ref.shape[0]
    w3 = (HY_ORDER + 1) * HY_W
    ck = min(half, HY_CHUNK)
    chunks = [slice(r, r + ck) for r in range(0, half, ck)]
    row = lax.broadcasted_iota(jnp.int32, (half, HY_W), 0)
    alt = jnp.where(row % 2 == 0, 1.0, -1.0)
    calt = jnp.where(lax.broadcasted_iota(jnp.int32, (ck, HY_W), 0) % 2 == 0, 1.0, -1.0)

    def short(j):
        c = slice(j * HY_W, (j + 1) * HY_W)
        ue = u_ref[:, j * HY_W:(j + 1) * HY_W].astype(F32)
        uo = u_ref[:, w3 + j * HY_W:w3 + (j + 1) * HY_W].astype(F32)
        uo_prev = jnp.where(row == 0, 0.0, pltpu.roll(uo, 1, 0))
        ue_next = jnp.where(row == half - 1, 0.0, pltpu.roll(ue, half - 1, 0))
        w0, w1, w2, b = sw_ref[0:1, c], sw_ref[1:2, c], sw_ref[2:3, c], sb_ref[0:1, c]
        return b + uo_prev * w0 + ue * w1 + uo * w2, b + ue * w0 + uo * w1 + ue_next * w2

    ve, vo = short(0)
    zfe_ref[...], zfo_ref[...] = ve, vo
    zbe_ref[...], zbo_ref[...] = ve.astype(BF16), vo.astype(BF16)
    g1e_ref[...], g1o_ref[...] = short(1)
    g2e_ref[...], g2o_ref[...] = short(2)
    gates = ((g1e_ref, g1o_ref), (g2e_ref, g2o_ref))

    for o in range(HY_ORDER):
        c = slice(o * HY_W, (o + 1) * HY_W)
        xr_m = jnp.sum(zfe_ref[...] * alt, axis=0, keepdims=True)
        xs_m = jnp.sum(zfo_ref[...] * alt, axis=0, keepdims=True)
        hr_m, hs_m = hm_ref[0:1, c], hm_ref[1:2, c]
        yr_m, ys_m = xr_m * hr_m - xs_m * hs_m, xr_m * hs_m + xs_m * hr_m
        for r in chunks:
            ec, es = _dot(ce_ref[r, :], zbe_ref[...]), _dot(se_ref[r, :], zbe_ref[...])
            oc, os_ = _dot(co_ref[r, :], zbo_ref[...]), _dot(so_ref[r, :], zbo_ref[...])
            xra, xsa, xrb, xsb = ec + oc, es + os_, ec - oc, os_ - es
            hra, hsa, hrb, hsb = hra_ref[r, c], hsa_ref[r, c], hrb_ref[r, c], hsb_ref[r, c]
            yra, ysa = xra * hra - xsa * hsa, xra * hsa + xsa * hra
            yrb, ysb = xrb * hrb - xsb * hsb, xrb * hsb + xsb * hrb
            p_ref[r, :] = (yra + yrb).astype(BF16)
            q_ref[r, :] = (ysa - ysb).astype(BF16)
            p2_ref[r, :] = (yra - yrb).astype(BF16)
            q2_ref[r, :] = (ysa + ysb).astype(BF16)
        ge_ref, go_ref = gates[o]
        for r in chunks:
            ye = _dot(ce_ref[r, :], p_ref[...]) + _dot(se_ref[r, :], q_ref[...]) + calt * yr_m
            yo = _dot(cot_ref[r, :], p2_ref[...]) + _dot(sot_ref[r, :], q2_ref[...]) + calt * ys_m
            ye = (ye + zfe_ref[r, :] * d_ref[o:o + 1, :]) * ge_ref[r, :]
            yo = (yo + zfo_ref[r, :] * d_ref[o:o + 1, :]) * go_ref[r, :]
            if o + 1 < HY_ORDER:
                zfe_ref[r, :], zfo_ref[r, :] = ye, yo
                zbe_ref[r, :], zbo_ref[r, :] = ye.astype(BF16), yo.astype(BF16)
            else:
                rows = slice(2 * r.start, 2 * r.start + 2 * ck)
                for j in range(HY_W // LANES):
                    lanes = slice(j * LANES, (j + 1) * LANES)
                    ys_ref[j, pl.ds(2 * r.start, ck, stride=2), :] = ye[:, lanes]
                    ys_ref[j, pl.ds(2 * r.start + 1, ck, stride=2), :] = yo[:, lanes]
                o_ref[rows, :] = jnp.concatenate([ys_ref[j, rows, :] for j in range(HY_W // LANES)],
                                                 axis=1).astype(o_ref.dtype)


def _hyena(hy, sw, sb, spec, dbias, layer, dft, n):
    m, w3 = 2 * hy.shape[0], hy.shape[1] // 2
    half = n // 2
    const = lambda b: (0, 0)
    once = lambda a: pl.BlockSpec(a.shape, const, pipeline_mode=pl.Buffered(1))
    f32s = pltpu.VMEM((half, HY_W), F32)
    bf16s = pltpu.VMEM((half, HY_W), BF16)
    return pl.pallas_call(
        _hyena_kernel,
        grid=(m // n,),
        in_specs=[pl.BlockSpec((half, 2 * w3), lambda b: (b, 0)),
                  _layer_spec(sw, layer), _layer_spec(sb, layer)]
                 + [once(a) for a in spec[:4]] + [pl.BlockSpec(spec[4].shape, const), _layer_spec(dbias, layer)]
                 + [once(a) for a in dft],
        out_specs=pl.BlockSpec((n, HY_W), lambda b: (b, 0)),
        out_shape=jax.ShapeDtypeStruct((m, HY_W), BF16),
        scratch_shapes=[f32s, f32s, bf16s, bf16s, f32s, f32s, f32s, f32s, bf16s, bf16s, bf16s, bf16s,
                        pltpu.VMEM((HY_W // LANES, n, LANES), F32)],
        compiler_params=_cparams(("arbitrary",)), name="hyena",
    )(hy, sw, sb, *spec, dbias, *dft)


def _fnet_kernel(u_ref, bc_ref, bs_ref, ce_ref, se_ref, co_ref, so_ref, o_ref):
    half = u_ref.shape[0]
    scale = 1.0 / math.sqrt(2 * half * FN_GROUP_W)
    ue, uo = u_ref[:, 0:FN_W], u_ref[:, FN_W:2 * FN_W]
    uce, use = _dot(ue, bc_ref[...]).astype(BF16), _dot(ue, bs_ref[...]).astype(BF16)
    uco, uso = _dot(uo, bc_ref[...]).astype(BF16), _dot(uo, bs_ref[...]).astype(BF16)
    ck = min(half, HY_CHUNK)
    for r0 in range(0, half, ck):
        r = slice(r0, r0 + ck)
        e = _dot(ce_ref[r, :], uce) - _dot(se_ref[r, :], use)
        o = _dot(co_ref[r, :], uco) - _dot(so_ref[r, :], uso)
        o_ref[r0:r0 + ck, :] = ((e + o) * scale).astype(o_ref.dtype)
        o_ref[half + r0:half + r0 + ck, :] = ((e - o) * scale).astype(o_ref.dtype)


def _fnet(fn, bc, bs, dft, n):
    m = 2 * fn.shape[0]
    half = n // 2
    const = lambda b: (0, 0)
    once = lambda a: pl.BlockSpec(a.shape, const, pipeline_mode=pl.Buffered(1))
    return pl.pallas_call(
        _fnet_kernel,
        grid=(m // n,),
        in_specs=[pl.BlockSpec((half, 2 * FN_W), lambda b: (b, 0)), pl.BlockSpec(bc.shape, const),
                  pl.BlockSpec(bs.shape, const)] + [once(a) for a in dft],
        out_specs=pl.BlockSpec((n, FN_W), lambda b: (b, 0)),
        out_shape=jax.ShapeDtypeStruct((m, FN_W), BF16),
        compiler_params=_cparams(("arbitrary",)), name="fnet",
    )(fn, bc, bs, *dft)


def _merge_kernel(x_ref, mod_ref, g1_ref, g2_ref, att_ref, hy_ref, fn_ref, wg_ref, wb_ref, wo_ref,
                  wr_ref, x1_ref, h2_ref, lg_ref):
    tm, d = x_ref.shape
    branches = ((att_ref, 0, ATT_Q_W), (hy_ref, ATT_Q_W, ATT_Q_W + HY_W), (fn_ref, ATT_Q_W + HY_W, MIX_W))
    rows = [slice(r, r + MERGE_CHUNK) for r in range(0, tm, MERGE_CHUNK)]
    xs = [x_ref[r, :] for r in rows]
    ys = [[_dot(b_ref[r, :], wb_ref[lo:hi, :]) for b_ref, lo, hi in branches] for r in rows]
    hbs = [_rms_mod(x, g1_ref[...], mod_ref[0:1, :], mod_ref[1:2, :]).astype(BF16) for x in xs]
    mixes = []
    for hb, y3 in zip(hbs, ys):
        mix = None
        for i, y in enumerate(y3):
            gate = _dot(hb, wg_ref[:, OFF_GATE + i * d:OFF_GATE + (i + 1) * d])
            term = y * (1.0 / (1.0 + jnp.exp(-gate)))
            mix = term if mix is None else mix + term
        mixes.append(mix.astype(BF16))
    for r, x, mix in zip(rows, xs, mixes):
        y = _dot(mix, wo_ref[...])
        x1 = x + mod_ref[2:3, :] * y
        x1_ref[r, :] = x1
        h2 = _rms_mod(x1, g2_ref[...], mod_ref[3:4, :], mod_ref[4:5, :])
        hi, lo = _split_bf16(h2)
        h2_ref[r, :] = hi
        t = _dot(hi, wr_ref[...]) + _dot(lo, wr_ref[...])
        lg_ref[r, :] = t + pltpu.roll(t, LANES - N_EXPERTS, 1)


def _merge(x, mod, g1, g2, att, hyo, fno, wg, layer, wb, wo, wr, n, tm):
    m, d = x.shape
    per_b = mod.shape[0] > 1
    tpb = n // tm
    row = lambda i: (i, 0)
    const = lambda i: (0, 0)
    mod_map = (lambda i: (i // tpb, 0, 0)) if per_b else (lambda i: (0, 0, 0))
    return pl.pallas_call(
        _merge_kernel,
        grid=(m // tm,),
        in_specs=[pl.BlockSpec((tm, d), row), pl.BlockSpec((None, 8, d), mod_map),
                  _layer_spec(g1, layer), _layer_spec(g2, layer),
                  pl.BlockSpec((tm, ATT_Q_W), row), pl.BlockSpec((tm, HY_W), row), pl.BlockSpec((tm, FN_W), row),
                  _layer_spec(wg, layer, pipeline_mode=pl.Buffered(1)),
                  _layer_spec(wb, layer), _layer_spec(wo, layer),
                  _layer_spec(wr, layer)],
        out_specs=[pl.BlockSpec((tm, d), row), pl.BlockSpec((tm, d), row), pl.BlockSpec((tm, LANES), row)],
        out_shape=[jax.ShapeDtypeStruct((m, d), F32), jax.ShapeDtypeStruct((m, d), BF16),
                   jax.ShapeDtypeStruct((m, LANES), F32)],
        compiler_params=_cparams(("arbitrary",)), name="merge",
    )(x, mod, g1, g2, att, hyo, fno, wg, wb, wo, wr)


def _route_kernel(lg_ref, tri_ref, rank_ref, aff_ref, rankt_ref, *, cap):
    ns, _, n = rank_ref.shape
    rows = ns * N_EXPERTS
    affs = []
    for s in range(ns):
        lt = lg_ref[s * n:(s + 1) * n, :].T[0:N_EXPERTS, :]
        e = jnp.exp(lt - jnp.max(lt, axis=0, keepdims=True))
        affs.append(e / jnp.sum(e, axis=0, keepdims=True))
    aff = jnp.concatenate(affs, axis=0) if ns > 1 else affs[0]
    bits = pltpu.bitcast(aff, jnp.int32)

    def step(i, thr):
        cand = thr | jnp.left_shift(jnp.int32(1), 30 - i)
        cnt = jnp.sum(jnp.where(bits >= cand, 1.0, 0.0), axis=1, keepdims=True)
        return jnp.where(cnt >= cap, cand, thr)

    thr = lax.fori_loop(0, 31, step, jnp.zeros((rows, 1), jnp.int32))
    gt = bits > thr
    eq = bits == thr
    need = cap - jnp.sum(jnp.where(gt, 1.0, 0.0), axis=1, keepdims=True)

    def excl_cumsum(mask):
        parts = []
        off = jnp.zeros((rows, 1), F32)
        for c in range(n // LANES):
            blk = mask[:, c * LANES:(c + 1) * LANES]
            parts.append(_dot(blk.astype(BF16), tri_ref[...]) + off)
            off = off + jnp.sum(blk, axis=1, keepdims=True)
        return jnp.concatenate(parts, axis=1)

    tie_rank = excl_cumsum(jnp.where(eq, 1.0, 0.0))
    sel = jnp.where(gt, 1.0, jnp.where(eq, jnp.where(tie_rank < need, 1.0, 0.0), 0.0))
    rank = jnp.where(sel > 0.5, excl_cumsum(sel), -1.0)
    pad = jnp.full((LANES - N_EXPERTS, n), -1.0, F32)
    for s in range(ns):
        r = rank[s * N_EXPERTS:(s + 1) * N_EXPERTS, :]
        rank_ref[s] = r.astype(jnp.int32)
        aff_ref[s] = affs[s]
        rankt_ref[s * n:(s + 1) * n, :] = jnp.concatenate([r, pad], axis=0).T.astype(BF16)


def _route(logits, tri, n, cap):
    m = logits.shape[0]
    b = m // n
    assert cap <= 256
    ns = math.gcd(b, ROUTE_SAMPLES)
    return pl.pallas_call(
        functools.partial(_route_kernel, cap=cap),
        grid=(b // ns,),
        in_specs=[pl.BlockSpec((ns * n, LANES), lambda i: (i, 0)), pl.BlockSpec(tri.shape, lambda i: (0, 0))],
        out_specs=[pl.BlockSpec((ns, N_EXPERTS, n), lambda i: (i, 0, 0)),
                   pl.BlockSpec((ns, N_EXPERTS, n), lambda i: (i, 0, 0)),
                   pl.BlockSpec((ns * n, LANES), lambda i: (i, 0))],
        out_shape=[jax.ShapeDtypeStruct((b, N_EXPERTS, n), jnp.int32),
                   jax.ShapeDtypeStruct((b, N_EXPERTS, n), F32),
                   jax.ShapeDtypeStruct((m, LANES), BF16)],
        compiler_params=_cparams(("arbitrary",)), name="route",
    )(logits, tri)


def _gather_kernel(h_ref, rank_ref, aff_ref, xg_ref, w_ref, sel_ref, *, cap):
    n, d = h_ref.shape
    slot = lax.broadcasted_iota(jnp.int32, (cap, n), 0)
    for e in range(N_EXPERTS):
        hit = slot == rank_ref[e:e + 1, :]
        sel_ref[e * cap:(e + 1) * cap, :] = jnp.where(hit, 1.0, 0.0).astype(BF16)
        wcol = jnp.sum(jnp.where(hit, aff_ref[e:e + 1, :], 0.0), axis=1, keepdims=True)
        w_ref[e] = jnp.broadcast_to(wcol, (cap, LANES))
    tn = 256
    for c in range(d // tn):
        xg = _dot(sel_ref[...], h_ref[:, c * tn:(c + 1) * tn]).astype(BF16)
        for e in range(N_EXPERTS):
            xg_ref[e, :, c * tn:(c + 1) * tn] = xg[e * cap:(e + 1) * cap, :]


def _gather(h2, rank, aff, n, cap):
    m, d = h2.shape
    b = m // n
    return pl.pallas_call(
        functools.partial(_gather_kernel, cap=cap),
        grid=(b,),
        in_specs=[pl.BlockSpec((n, d), lambda i: (i, 0)),
                  pl.BlockSpec((None, N_EXPERTS, n), lambda i: (i, 0, 0)),
                  pl.BlockSpec((None, N_EXPERTS, n), lambda i: (i, 0, 0))],
        out_specs=[pl.BlockSpec((N_EXPERTS, None, cap, d), lambda i: (0, i, 0, 0)),
                   pl.BlockSpec((N_EXPERTS, None, cap, LANES), lambda i: (0, i, 0, 0))],
        out_shape=[jax.ShapeDtypeStruct((N_EXPERTS, b, cap, d), BF16),
                   jax.ShapeDtypeStruct((N_EXPERTS, b, cap, LANES), F32)],
        scratch_shapes=[pltpu.VMEM((N_EXPERTS * cap, n), BF16)],
        compiler_params=_cparams(("arbitrary",)), name="moe_gather",
    )(h2, rank, aff)


def _ffn_kernel(*refs, extra):
    if extra:
        x_ref, w_ref, xc_ref, wc_ref, wg_ref, wu_ref, wd_ref, o_ref, oc_ref, acc_ref, accc_ref = refs
    else:
        x_ref, w_ref, wg_ref, wu_ref, wd_ref, o_ref, acc_ref = refs
    f = pl.program_id(2)

    def run(x_ref, w_ref, o_ref, acc_ref):
        @pl.when((pl.program_id(0) == 0) & (pl.program_id(1) == 0) & (f == 0))
        def _():
            acc_ref[...] = jnp.zeros_like(acc_ref)

        x = x_ref[...]
        d = acc_ref.shape[1]
        tf = 512
        hms, wds = [], []
        for c in range(wg_ref.shape[1] // tf):
            cols = slice(c * tf, (c + 1) * tf)
            a = _dot(x, wg_ref[:, cols].astype(BF16))
            u = _dot(x, wu_ref[:, cols].astype(BF16))
            hms.append((a * (1.0 / (1.0 + jnp.exp(-a))) * u).astype(BF16))
            wds.append(wd_ref[cols, :].astype(BF16))
        tn = 256
        for c in range(d // tn):
            cols = slice(c * tn, (c + 1) * tn)
            t = sum(_dot(hm, wd[:, cols]) for hm, wd in zip(hms, wds))
            s = jnp.where(f > 0, acc_ref[:, cols], 0.0) + t
            acc_ref[:, cols] = s
            o_ref[:, cols] = (s * w_ref[...] if tn == LANES else
                              s * jnp.concatenate([w_ref[...]] * (tn // LANES), axis=1)).astype(o_ref.dtype)

    run(x_ref, w_ref, o_ref, acc_ref)
    if extra:
        @pl.when(pl.program_id(1) == 0)
        def _():
            run(xc_ref, wc_ref, oc_ref, accc_ref)


def _ffn(xg, w, xgc, wc, wg, wu, wd, layer, tm, tf):
    ne, rows, d = xg.shape
    ff = wg.shape[3]
    extra = xgc is not None
    tile = lambda e, i, f: (e, i, 0)
    whole = lambda e, i, f: (e, 0, 0)
    args = [xg, w]
    in_specs = [pl.BlockSpec((None, tm, d), tile), pl.BlockSpec((None, tm, LANES), tile)]
    out_specs = [pl.BlockSpec((None, tm, d), tile)]
    out_shape = [jax.ShapeDtypeStruct((ne, rows, d), BF16)]
    scratch = [pltpu.VMEM((tm, d), F32)]
    if extra:
        rc = xgc.shape[1]
        args += [xgc, wc]
        in_specs += [pl.BlockSpec((None, rc, d), whole), pl.BlockSpec((None, rc, LANES), whole)]
        out_specs.append(pl.BlockSpec((None, rc, d), whole))
        out_shape.append(jax.ShapeDtypeStruct((ne, rc, d), BF16))
        scratch.append(pltpu.VMEM((rc, d), F32))
    args += [wg, wu, wd]
    in_specs += [pl.BlockSpec((None, None, d, tf), lambda e, i, f: (layer, e, 0, f)),
                 pl.BlockSpec((None, None, d, tf), lambda e, i, f: (layer, e, 0, f)),
                 pl.BlockSpec((None, None, tf, d), lambda e, i, f: (layer, e, f, 0))]
    outs = pl.pallas_call(
        functools.partial(_ffn_kernel, extra=extra),
        grid=(ne, rows // tm, ff // tf),
        in_specs=in_specs, out_specs=out_specs, out_shape=out_shape, scratch_shapes=scratch,
        compiler_params=_cparams(("arbitrary", "arbitrary", "arbitrary")), name="moe_ffn",
    )(*args)
    return (outs[0], outs[1]) if extra else (outs[0], None)


def _scatter_kernel(rankt_ref, y_ref, expand_ref, slot_ref, o_ref, selt_ref, *, cap):
    n = rankt_ref.shape[0]
    d = o_ref.shape[1]
    width = N_EXPERTS * cap
    if cap % LANES == 0:
        rank = rankt_ref[...].astype(F32)
        slot = lax.broadcasted_iota(jnp.int32, (n, cap), 1).astype(F32)
        for e in range(N_EXPERTS):
            r = jnp.broadcast_to(rank[:, e:e + 1], (n, cap))
            selt_ref[:, e * cap:(e + 1) * cap] = jnp.where(r == slot, 1.0, 0.0).astype(BF16)
    else:
        r = _dot(rankt_ref[...], expand_ref[...])
        selt_ref[...] = jnp.where(r == slot_ref[...], 1.0, 0.0).astype(BF16)
    y = y_ref[...].reshape(width, d)
    tn = 256
    for c in range(d // tn):
        o_ref[:, c * tn:(c + 1) * tn] = _dot(selt_ref[...], y[:, c * tn:(c + 1) * tn])


def _scatter(rankt, yw, expand, slotpat, n, cap):
    ne, b, _, d = yw.shape
    return pl.pallas_call(
        functools.partial(_scatter_kernel, cap=cap),
        grid=(b,),
        in_specs=[pl.BlockSpec((n, LANES), lambda i: (i, 0)),
                  pl.BlockSpec((ne, None, cap, d), lambda i: (0, i, 0, 0)),
                  pl.BlockSpec(expand.shape, lambda i: (0, 0)),
                  pl.BlockSpec(slotpat.shape, lambda i: (0, 0))],
        out_specs=pl.BlockSpec((n, d), lambda i: (i, 0)),
        out_shape=jax.ShapeDtypeStruct((b * n, d), F32),
        scratch_shapes=[pltpu.VMEM((n, ne * cap), BF16)],
        compiler_params=_cparams(("arbitrary",)), name="moe_scatter",
    )(rankt, yw, expand, slotpat)


def _moe(routed, routed_extra, wg, wu, wd, layer):
    def flat(r):
        rows = r["b"] * r["cap"]
        return r["xg"].reshape(N_EXPERTS, rows, D_MODEL), r["w"].reshape(N_EXPERTS, rows, LANES)

    def scatter(r, yw):
        yw = yw.reshape(N_EXPERTS, r["b"], r["cap"], D_MODEL)
        return _scatter(r["rankt"], yw, r["tabs"]["expand"], r["tabs"]["slotpat"], r["n"], r["cap"])

    xg, w = flat(routed)
    xgc, wc = flat(routed_extra) if routed_extra is not None else (None, None)
    yw, ywc = _ffn(xg, w, xgc, wc, wg, wu, wd, layer, min(xg.shape[1], FFN_ROW_TILE), FFN_HIDDEN_TILE)
    return scatter(routed, yw), (scatter(routed_extra, ywc) if routed_extra is not None else None)


def _combine_kernel(x_ref, y_ref, mod_ref, g_ref, o_ref, *, norm):
    x = x_ref[...] + mod_ref[5:6, :] * y_ref[...]
    if norm:
        x = x * lax.rsqrt(jnp.mean(x * x, axis=-1, keepdims=True) + EPS) * g_ref[...]
    o_ref[...] = x


def _combine(x, y, mod, g, n, tm, norm):
    m, d = x.shape
    per_b = mod.shape[0] > 1
    tpb = n // tm
    row = lambda i: (i, 0)
    mod_map = (lambda i: (i // tpb, 0, 0)) if per_b else (lambda i: (0, 0, 0))
    return pl.pallas_call(
        functools.partial(_combine_kernel, norm=norm),
        grid=(m // tm,),
        in_specs=[pl.BlockSpec((tm, d), row), pl.BlockSpec((tm, d), row),
                  pl.BlockSpec((None, 8, d), mod_map), pl.BlockSpec((1, d), lambda i: (0, 0))],
        out_specs=pl.BlockSpec((tm, d), row),
        out_shape=jax.ShapeDtypeStruct((m, d), F32),
        compiler_params=_cparams(("arbitrary",)), name="combine",
    )(x, y, mod, g)


def _fnet_dft_tables(n):
    j = jnp.arange(n // 2, dtype=jnp.int32)
    even = ((j[:, None] * (2 * j[None, :])) % n).astype(F32) * (2.0 * math.pi / n)
    odd = ((j[:, None] * (2 * j[None, :] + 1)) % n).astype(F32) * (2.0 * math.pi / n)
    return tuple(a.astype(BF16) for a in (jnp.cos(even), jnp.sin(even), jnp.cos(odd), jnp.sin(odd)))


def _rope_tables(n):
    rows = n // GRID_W
    row = jnp.repeat(jnp.arange(rows, dtype=F32), GRID_W)
    col = jnp.tile(jnp.arange(GRID_W, dtype=F32), rows)
    inv = ROPE_THETA ** (-jnp.arange(0, ROPE_AXIS_DIM, 2, dtype=F32) / ROPE_AXIS_DIM)
    ang = jnp.concatenate([row[:, None] * inv, col[:, None] * inv], axis=-1)
    cos = jnp.repeat(jnp.cos(ang), 2, axis=1)
    sin = jnp.repeat(jnp.sin(ang), 2, axis=1) * jnp.tile(jnp.array([-1.0, 1.0], F32), HEAD_DIM // 2)
    return jnp.tile(cos, (1, LANES // HEAD_DIM)), jnp.tile(sin, (1, LANES // HEAD_DIM))


def _hyena_dft_tables(n):
    j = jnp.arange(n // 2, dtype=jnp.int32)
    even = ((j[:, None] * j[None, :]) % n).astype(F32) * (2.0 * math.pi / n)
    odd = ((j[:, None] * (2 * j[None, :] + 1)) % (2 * n)).astype(F32) * (math.pi / n)
    ce, se, co, so = jnp.cos(even), jnp.sin(even), jnp.cos(odd), jnp.sin(odd)
    return tuple(a.astype(BF16) for a in (ce, se, co, so, co.T, so.T))


def _hyena_feats(n):
    pos = jnp.concatenate([jnp.arange(0, n, 2), jnp.arange(1, n, 2)]).astype(F32)
    t = pos / (n - 1)
    bands = jnp.linspace(1e-4, HY_BANDS - 1, HY_BANDS, dtype=F32)
    ang = (2.0 * math.pi / n) * pos[:, None] * bands[None, :]
    feats = jnp.concatenate([t[:, None], jnp.cos(ang), -jnp.sin(ang)], axis=-1)
    return jnp.pad(feats, ((0, 0), (0, LANES - HY_EMB)))


def _side(x, xb, n, mod, l, P, tabs, kv_ext, rope, tm, last_ctx):
    b = x.shape[0] // n
    outs = _inproj(x, xb, mod, P["norm1_g"], P["w_in"], l, tabs["cos"] if rope else None, tabs["sin"] if rope else None,
                   P["qg"], P["kg"], P["gmat"], n, tm)
    if xb is not None:
        x, outs = outs[0], outs[1:]
    q, k, v, hy, fn = outs
    k3, v3 = k.reshape(b, n, ATT_KV_W), v.reshape(b, n, ATT_KV_W)
    if last_ctx:
        return None, None, k3, v3
    if kv_ext is not None:
        kc, vc = jnp.concatenate([k3, kv_ext[0]], axis=1), jnp.concatenate([v3, kv_ext[1]], axis=1)
    else:
        kc, vc = k3, v3
    s = kc.shape[1]
    fill = jnp.concatenate([jnp.ones((b, s, 1), BF16), jnp.zeros((b, s, LANES - HEAD_DIM - 1), BF16)], axis=-1)
    vaug = jnp.stack([jnp.concatenate([vc[..., g * HEAD_DIM:(g + 1) * HEAD_DIM], fill], axis=-1)
                      for g in range(N_KV_HEADS)], axis=1)
    att = _attention(q, jnp.swapaxes(kc, 1, 2), vaug, n, min(n, ATTN_Q_TILE))
    spec = _hyena_filter(tabs["feats"], P["hy_w1"], P["hy_b1"], P["hy_w2"], P["hy_b2"],
                         P["hy_w3"], P["hy_freq"], l, tabs["deltas"], tabs["hy_dft"][:4])
    hyo = _hyena(hy, P["hy_sw"], P["hy_sb"], spec, P["hy_bias"], l, tabs["hy_dft"], n)
    fno = _fnet(fn, tabs["bc"], tabs["bs"], tabs["fn_dft"], n)
    x1, h2, logits = _merge(x, mod, P["norm1_g"], P["norm2_g"], att, hyo, fno, P["w_in"], l,
                            P["w_branch"], P["w_out"], P["w_router"], n, tm)
    cap = CAPACITY_FACTOR * n // N_EXPERTS
    rank, aff, rankt = _route(logits, tabs["tri"], n, cap)
    xg, w = _gather(h2, rank, aff, n, cap)
    return x1, dict(xg=xg, w=w, rankt=rankt, tabs=tabs, n=n, cap=cap, b=b), k3, v3


def _tables(n, rope):
    a = jnp.arange(FN_W, dtype=jnp.int32)
    same = (a[:, None] // FN_GROUP_W) == (a[None, :] // FN_GROUP_W)
    ang = ((a[:, None] * a[None, :]) % FN_GROUP_W).astype(F32) * (2.0 * math.pi / FN_GROUP_W)
    deltas = jnp.abs(jnp.linspace(math.log(HY_DECAY_TARGET) / HY_DECAY_LONG_PCT,
                                  math.log(HY_DECAY_TARGET) / HY_DECAY_SHORT_PCT, HY_W, dtype=F32))
    i = jnp.arange(LANES, dtype=jnp.int32)
    cap = CAPACITY_FACTOR * n // N_EXPERTS
    j = jnp.arange(N_EXPERTS * cap, dtype=jnp.int32)
    tabs = dict(expand=(i[:, None] == j[None, :] // cap).astype(BF16),
                slotpat=(j % cap).astype(F32)[None, :],
                hy_dft=_hyena_dft_tables(n), fn_dft=_fnet_dft_tables(n),
                bc=jnp.where(same, jnp.cos(ang), 0.0).astype(BF16),
                bs=jnp.where(same, jnp.sin(ang), 0.0).astype(BF16),
                feats=_hyena_feats(n), deltas=jnp.tile(deltas, HY_ORDER)[None, :],
                tri=(i[:, None] < i[None, :]).astype(BF16))
    if rope:
        tabs["cos"], tabs["sin"] = _rope_tables(n)
    return tabs


def kernel(x, c, ctx, c_ctx, w_mod, b_mod, norm1_g, norm2_g, w_in, q_gain, k_gain, hy_short_w, hy_short_b,
           hy_f_w1, hy_f_b1, hy_f_w2, hy_f_b2, hy_f_w3, hy_f_freq, hy_bias, w_branch, w_out, w_router,
           w_gate, w_up, w_down, final_g):
    bsz, n_lat, d = x.shape
    n_ctx = ctx.shape[1]
    depth = w_mod.shape[0]
    assert d == D_MODEL and n_lat % LANES == 0 and n_ctx % LANES == 0

    hid = jnp.arange(ATT_Q_W, dtype=jnp.int32) // HEAD_DIM
    P = dict(
        norm1_g=norm1_g[:, None, :], norm2_g=norm2_g[:, None, :],
        w_in=w_in.astype(BF16),
        qg=jnp.tile(q_gain, (1, N_Q_HEADS))[:, None, :], kg=jnp.tile(k_gain, (1, N_KV_HEADS))[:, None, :],
        gmat=(hid[:, None] == hid[None, :]).astype(BF16),
        hy_sw=hy_short_w, hy_sb=hy_short_b[:, None, :],
        hy_w1=jnp.pad(hy_f_w1, ((0, 0), (0, LANES - HY_EMB), (0, 0))), hy_b1=hy_f_b1[:, None, :],
        hy_w2=hy_f_w2, hy_b2=hy_f_b2[:, None, :], hy_w3=hy_f_w3, hy_freq=hy_f_freq[:, None, :],
        hy_bias=hy_bias,
        w_branch=w_branch.astype(BF16), w_out=w_out.astype(BF16),
        w_gate=w_gate, w_up=w_up, w_down=w_down,
    )
    wr_hi = w_router.astype(BF16)
    wr_lo = (w_router - wr_hi.astype(F32)).astype(BF16)
    P["w_router"] = jnp.pad(jnp.concatenate([wr_hi, wr_lo], axis=-1), ((0, 0), (0, 0), (0, LANES - 2 * N_EXPERTS)))

    tab_x = _tables(n_lat, True)
    tab_c = _tables(n_ctx, False)

    rows = -(-(bsz + 1) // 8) * 8
    c_all = jnp.concatenate([c, c_ctx[None, :], jnp.zeros((rows - bsz - 1, d), F32)], axis=0)
    mod = _modulation(c_all, w_mod, b_mod).reshape(depth, rows, 6, d)
    mod = jnp.pad(mod, ((0, 0), (0, 0), (0, 2), (0, 0)))

    tm_x, tm_c = min(n_lat, ROW_TILE), min(n_ctx, ROW_TILE)
    xs, xpend = x.reshape(bsz * n_lat, d), None
    cs, cpend = ctx.reshape(bsz * n_ctx, d), None
    for l in range(depth):
        last = l == depth - 1
        mod_x, mod_c = mod[l, :bsz], mod[l, bsz:bsz + 1]
        pm_x = None if l == 0 else jnp.concatenate([mod_x[:, :5], mod[l - 1, :bsz, 5:6], mod_x[:, 6:]], axis=1)
        pm_c = None if l == 0 else jnp.concatenate([mod_c[:, :5], mod[l - 1, bsz:bsz + 1, 5:6], mod_c[:, 6:]],
                                                   axis=1)
        c1, rc, kc, vc = _side(cs, cpend, n_ctx, mod_c if l == 0 else pm_c, l, P, tab_c, None, False, tm_c, last)
        x1, rx, _, _ = _side(xs, xpend, n_lat, mod_x if l == 0 else pm_x, l, P, tab_x, (kc, vc), True, tm_x, False)
        xy, cy = _moe(rx, rc, P["w_gate"], P["w_up"], P["w_down"], l)
        xs, xpend = x1, xy
        if not last:
            cs, cpend = c1, cy
    out = _combine(xs, xpend, mod[depth - 1, :bsz], final_g[None, :], n_lat, tm_x, True)
    return out.reshape(bsz, n_lat, d)
```

```python
import functools
import math

import jax
import jax.numpy as jnp
from jax import lax
from jax.experimental import pallas as pl
from jax.experimental.pallas import tpu as pltpu

F32 = jnp.float32
BF16 = jnp.bfloat16

D_MODEL = 1024
GRID_W = 64
HEAD_DIM = 64
N_Q_HEADS = 8
N_KV_HEADS = 2
Q_GROUP = N_Q_HEADS // N_KV_HEADS
ATT_Q_W = N_Q_HEADS * HEAD_DIM
ATT_KV_W = N_KV_HEADS * HEAD_DIM
ROPE_THETA = 10000.0
ROPE_AXIS_DIM = HEAD_DIM // 2
HY_W = D_MODEL // 4
HY_ORDER = 2
HY_SHORT = 3
HY_BANDS = 16
HY_EMB = 2 * HY_BANDS + 1
HY_FFN = 64
HY_DECAY_TARGET = 1e-2
HY_DECAY_SHORT_PCT = 0.3
HY_DECAY_LONG_PCT = 1.5
FN_GROUPS = 4
FN_GROUP_W = D_MODEL // 16
FN_W = FN_GROUPS * FN_GROUP_W
MIX_W = ATT_Q_W + HY_W + FN_W
N_BRANCH = 3
OFF_Q = 0
OFF_K = OFF_Q + ATT_Q_W
OFF_V = OFF_K + ATT_KV_W
OFF_HY = OFF_V + ATT_KV_W
OFF_FN = OFF_HY + (HY_ORDER + 1) * HY_W
OFF_GATE = OFF_FN + FN_W
N_EXPERTS = 16
CAPACITY_FACTOR = 2
EPS = 1e-6

LANES = 128
VMEM_LIMIT = 56 * 1024 * 1024
SCATTER_COL_TILE = 256
ROW_TILE = 512
ATTN_Q_TILE = 256
FFN_ROW_TILE = 1024
FFN_HIDDEN_TILE = 1024
HY_CHUNK = 512
ROUTE_SAMPLES = 4
INPROJ_CHUNK = 256
MERGE_CHUNK = 256
ATTN_HEADS_PER_CHAIN = 4

A_Q, A_K, A_V, A_HY, A_FN, A_END = 0, 512, 640, 768, 1536, 1792


def _cparams(sem):
    return pltpu.CompilerParams(dimension_semantics=sem, vmem_limit_bytes=VMEM_LIMIT)


def _dot(a, b):
    return jnp.dot(a, b, preferred_element_type=F32)


def _split_bf16(x):
    hi = x.astype(BF16)
    lo = (x - hi.astype(F32)).astype(BF16)
    return hi, lo


def _layer_spec(arr, layer, **kw):
    zeros = (0,) * (arr.ndim - 1)
    return pl.BlockSpec((None,) + arr.shape[1:], lambda *_: (layer,) + zeros, **kw)


def _rms_mod(x, g, shift, scale):
    y = x * lax.rsqrt(jnp.mean(x * x, axis=-1, keepdims=True) + EPS)
    return (y * g) * (1.0 + scale) + shift


def _mod_kernel(c_ref, w_ref, b_ref, o_ref):
    c = c_ref[...]
    sc = c * (1.0 / (1.0 + jnp.exp(-c)))
    o_ref[...] = jnp.dot(sc, w_ref[...], preferred_element_type=F32,
                         precision=lax.Precision.HIGHEST) + b_ref[...]


def _modulation(c_all, w_mod, b_mod):
    depth, d, n6 = w_mod.shape
    rows = c_all.shape[0]
    tn = 1536
    return pl.pallas_call(
        _mod_kernel,
        grid=(depth, n6 // tn),
        in_specs=[pl.BlockSpec((rows, d), lambda l, j: (0, 0)),
                  pl.BlockSpec((None, d, tn), lambda l, j: (l, 0, j)),
                  pl.BlockSpec((None, 1, tn), lambda l, j: (l, 0, j))],
        out_specs=pl.BlockSpec((None, rows, tn), lambda l, j: (l, 0, j)),
        out_shape=jax.ShapeDtypeStruct((depth, rows, n6), F32),
        compiler_params=_cparams(("arbitrary", "arbitrary")),
        name="modulation",
    )(c_all, w_mod, b_mod.reshape(depth, 1, n6))


def _head_norm(u, gmat_ref, width):
    gm = gmat_ref[0:width, 0:width]
    ms = _dot((u * u).astype(BF16), gm) * (1.0 / HEAD_DIM)
    return lax.rsqrt(ms + EPS)


def _inproj_kernel(*refs, rope, combine):
    it = iter(refs)
    x_ref = next(it)
    if combine:
        xb_ref = next(it)
    mod_ref = next(it)
    g_ref = next(it)
    w_ref = next(it)
    if rope:
        cos_ref = next(it)
        sin_ref = next(it)
    qg_ref = next(it)
    kg_ref = next(it)
    gmat_ref = next(it)
    if combine:
        xo_ref = next(it)
    q_ref = next(it)
    k_ref = next(it)
    v_ref = next(it)
    hy_ref = next(it)
    fn_ref = next(it)
    hs_ref = next(it)

    tm = x_ref.shape[0]
    ck = min(tm, INPROJ_CHUNK)
    chunks = [slice(r0, r0 + ck) for r0 in range(0, tm, ck)]
    hbs, hps = [], []
    for rows in chunks:
        x = x_ref[rows, :]
        if combine:
            x = x + mod_ref[5:6, :] * xb_ref[rows, :]
            xo_ref[rows, :] = x
        h = _rms_mod(x, g_ref[...], mod_ref[0:1, :], mod_ref[1:2, :])
        hbs.append(h.astype(BF16))
        parts = []
        for j in range(h.shape[1] // LANES):
            hs_ref[j, rows, :] = h[:, j * LANES:(j + 1) * LANES]
            he = hs_ref[j, pl.ds(rows.start, ck // 2, stride=2), :]
            ho = hs_ref[j, pl.ds(rows.start + 1, ck // 2, stride=2), :]
            parts.append(jnp.concatenate([he, ho], axis=0))
        hps.append(jnp.concatenate(parts, axis=1).astype(BF16))

    def qk(rows, hb, lo, width, gain_ref, out_ref, scale):
        u = _dot(hb, w_ref[:, lo:lo + width])
        r = _head_norm(u, gmat_ref, width)
        un = u * r * gain_ref[0:1, :]
        if rope:
            even = lax.broadcasted_iota(jnp.int32, (un.shape[0], LANES), 1) % 2 == 0
            parts = []
            for j in range(width // LANES):
                s = un[:, j * LANES:(j + 1) * LANES]
                sw = jnp.where(even, pltpu.roll(s, LANES - 1, 1), pltpu.roll(s, 1, 1))
                parts.append(s * cos_ref[rows, :] + sw * sin_ref[rows, :])
            un = jnp.concatenate(parts, axis=1) if len(parts) > 1 else parts[0]
        out_ref[rows, :] = (un * scale).astype(out_ref.dtype)

    for rows, hb, hp in zip(chunks, hbs, hps):
        qk(rows, hb, A_Q, ATT_Q_W, qg_ref, q_ref, HEAD_DIM ** -0.5 * math.log2(math.e))
        qk(rows, hb, A_K, ATT_KV_W, kg_ref, k_ref, 1.0)
        v_ref[rows, :] = _dot(hb, w_ref[:, A_V:A_HY]).astype(v_ref.dtype)
        pairs = slice(rows.start // 2, rows.start // 2 + ck // 2)
        for out_ref, lo, hi in ((hy_ref, A_HY, A_FN), (fn_ref, A_FN, A_END)):
            u = _dot(hp, w_ref[:, lo:hi]).astype(out_ref.dtype)
            out_ref[pairs, 0:hi - lo] = u[0:ck // 2, :]
            out_ref[pairs, hi - lo:2 * (hi - lo)] = u[ck // 2:ck, :]


def _inproj(x, xb, mod, norm_g, w_in, layer, cos_t, sin_t, qg, kg, gmat, n, tm):
    m, d = x.shape
    rope = cos_t is not None
    combine = xb is not None
    per_b = mod.shape[0] > 1
    tpb = n // tm
    row = lambda i: (i, 0)
    const = lambda i: (0, 0)
    mod_map = (lambda i: (i // tpb, 0, 0)) if per_b else (lambda i: (0, 0, 0))
    args, specs = [x], [pl.BlockSpec((tm, d), row)]
    if combine:
        args.append(xb)
        specs.append(pl.BlockSpec((tm, d), row))
    args += [mod, norm_g, w_in]
    specs += [pl.BlockSpec((None, 8, d), mod_map), _layer_spec(norm_g, layer),
              pl.BlockSpec((None, d, A_END), lambda i: (layer, 0, 0))]
    if rope:
        args += [cos_t, sin_t]
        specs += [pl.BlockSpec((tm, LANES), lambda i: (i % tpb, 0)),
                  pl.BlockSpec((tm, LANES), lambda i: (i % tpb, 0))]
    args += [qg, kg, gmat]
    specs += [_layer_spec(qg, layer), _layer_spec(kg, layer), pl.BlockSpec(gmat.shape, const)]
    widths = [ATT_Q_W, ATT_KV_W, ATT_KV_W]
    out_shape = [jax.ShapeDtypeStruct((m, w), BF16) for w in widths]
    out_specs = [pl.BlockSpec((tm, w), row) for w in widths]
    for w in ((HY_ORDER + 1) * HY_W, FN_W):
        out_shape.append(jax.ShapeDtypeStruct((m // 2, 2 * w), BF16))
        out_specs.append(pl.BlockSpec((tm // 2, 2 * w), row))
    if combine:
        out_shape.insert(0, jax.ShapeDtypeStruct((m, d), F32))
        out_specs.insert(0, pl.BlockSpec((tm, d), row))
    return pl.pallas_call(
        functools.partial(_inproj_kernel, rope=rope, combine=combine),
        grid=(m // tm,), in_specs=specs, out_specs=out_specs, out_shape=out_shape,
        scratch_shapes=[pltpu.VMEM((d // LANES, tm, LANES), F32)],
        compiler_params=_cparams(("arbitrary",)), name="inproj",
    )(*args)


def _attn_kernel(q_ref, kt_ref, v_ref, o_ref):
    tq = q_ref.shape[0]
    hpc = ATTN_HEADS_PER_CHAIN
    chains = [(h0 // Q_GROUP, range(h0, h0 + hpc)) for h0 in range(0, N_Q_HEADS, hpc)]
    ss = []
    for g, heads in chains:
        qg = jnp.concatenate([q_ref[:, h * HEAD_DIM:(h + 1) * HEAD_DIM] for h in heads], axis=0)
        ss.append(_dot(qg, kt_ref[g * HEAD_DIM:(g + 1) * HEAD_DIM, :]))
    outs = []
    for (g, heads), s in zip(chains, ss):
        p = jnp.exp2(s - jnp.max(s, axis=-1, keepdims=True)).astype(BF16)
        o = _dot(p, v_ref[g])
        for j in range(hpc):
            oh = o[j * tq:(j + 1) * tq, :]
            outs.append(oh[:, 0:HEAD_DIM] / oh[:, HEAD_DIM:HEAD_DIM + 1])
    o_ref[...] = jnp.concatenate(outs, axis=1).astype(o_ref.dtype)


def _attention(q, kt, vaug, n, tq):
    m = q.shape[0]
    b, _, s = kt.shape
    tpb = n // tq
    return pl.pallas_call(
        _attn_kernel,
        grid=(b, tpb),
        in_specs=[pl.BlockSpec((tq, ATT_Q_W), lambda bi, i: (bi * tpb + i, 0)),
                  pl.BlockSpec((None, ATT_KV_W, s), lambda bi, i: (bi, 0, 0)),
                  pl.BlockSpec((None, N_KV_HEADS, s, LANES), lambda bi, i: (bi, 0, 0, 0))],
        out_specs=pl.BlockSpec((tq, ATT_Q_W), lambda bi, i: (bi * tpb + i, 0)),
        out_shape=jax.ShapeDtypeStruct((m, ATT_Q_W), BF16),
        compiler_params=_cparams(("arbitrary", "arbitrary")), name="attention",
    )(q, kt, vaug)


def _hyena_filter_kernel(feat_ref, w1_ref, b1_ref, w2_ref, b2_ref, w3_ref, freq_ref, delta_ref,
                         ce_ref, se_ref, co_ref, so_ref, hra_ref, hsa_ref, hrb_ref, hsb_ref, hm_ref):
    n = feat_ref.shape[0]
    half = n // 2
    hp = lax.Precision.HIGHEST
    freq = freq_ref[...]
    h = jnp.sin(freq * (jnp.dot(feat_ref[...], w1_ref[...], preferred_element_type=F32, precision=hp)
                        + b1_ref[...]))
    h = jnp.sin(freq * (jnp.dot(h, w2_ref[...], preferred_element_type=F32, precision=hp) + b2_ref[...]))
    h = jnp.dot(h, w3_ref[...], preferred_element_type=F32, precision=hp)
    w2o = HY_ORDER * HY_W
    row = lax.broadcasted_iota(jnp.int32, (n, w2o), 0)
    pos = jnp.where(row < half, 2 * row, 2 * (row - half) + 1)
    t = pos.astype(F32) / (n - 1)
    decay = jnp.exp(-t * delta_ref[...])
    hf = h[:, 0:w2o] * decay
    hb = jnp.where(pos == 0, 0.0, h[:, w2o:2 * w2o] * decay)
    inv = 1.0 / (jnp.sum(jnp.abs(hf), axis=0, keepdims=True) + jnp.sum(jnp.abs(hb), axis=0, keepdims=True))
    hsum = (hf + hb) * inv
    hdif = (hf - hb) * inv
    sum_e, sum_o = _split_bf16(hsum[0:half, :]), _split_bf16(hsum[half:n, :])
    dif_e, dif_o = _split_bf16(hdif[0:half, :]), _split_bf16(hdif[half:n, :])

    def dot2(m_ref, r, parts):
        return _dot(m_ref[r, :], parts[0]) + _dot(m_ref[r, :], parts[1])

    ck = min(half, HY_CHUNK)
    for r0 in range(0, half, ck):
        r = slice(r0, r0 + ck)
        ec, oc = dot2(ce_ref, r, sum_e), dot2(co_ref, r, sum_o)
        es, os_ = dot2(se_ref, r, dif_e), dot2(so_ref, r, dif_o)
        k = lax.broadcasted_iota(jnp.int32, (ck, w2o), 0) + r0
        wk = jnp.where(k == 0, 1.0, 2.0) * (1.0 / (2 * n))
        hra_ref[r, :] = (ec + oc) * wk
        hsa_ref[r, :] = (es + os_) * wk
        hrb_ref[r, :] = (ec - oc) * wk
        hsb_ref[r, :] = (os_ - es) * wk
    alt = jnp.where(lax.broadcasted_iota(jnp.int32, (half, w2o), 0) % 2 == 0, 1.0, -1.0)
    hm_ref[0:1, :] = jnp.sum(hsum[0:half, :] * alt, axis=0, keepdims=True) * (1.0 / n)
    hm_ref[1:2, :] = jnp.sum(hdif[half:n, :] * alt, axis=0, keepdims=True) * (1.0 / n)


def _hyena_filter(feats, w1, b1, w2, b2, w3, freq, layer, deltas, dft):
    n = feats.shape[0]
    half = n // 2
    w2o = HY_ORDER * HY_W
    args = (feats, w1, b1, w2, b2, w3, freq, deltas) + tuple(dft)
    whole = lambda a: pl.BlockSpec(a.shape, lambda i: (0, 0))
    return pl.pallas_call(
        _hyena_filter_kernel,
        grid=(1,),
        in_specs=[whole(feats)] + [_layer_spec(a, layer) for a in args[1:7]] + [whole(a) for a in args[7:]],
        out_specs=[pl.BlockSpec((half, w2o), lambda i: (0, 0))] * 4 + [pl.BlockSpec((2, w2o), lambda i: (0, 0))],
        out_shape=[jax.ShapeDtypeStruct((half, w2o), F32)] * 4 + [jax.ShapeDtypeStruct((2, w2o), F32)],
        compiler_params=_cparams(("arbitrary",)), name="hyena_filter",
    )(*args)


def _hyena_kernel(u_ref, sw_ref, sb_ref, hra_ref, hsa_ref, hrb_ref, hsb_ref, hm_ref, d_ref,
                  ce_ref, se_ref, co_ref, so_ref, cot_ref, sot_ref, o_ref,
                  zfe_ref, zfo_ref, zbe_ref, zbo_ref, g1e_ref, g1o_ref, g2e_ref, g2o_ref,
                  p_ref, q_ref, p2_ref, q2_ref, ys_ref):
    half = u_ref.shape[0]
    w3 = (HY_ORDER + 1) * HY_W
    ck = min(half, HY_CHUNK)
    chunks = [slice(r, r + ck) for r in range(0, half, ck)]
    row = lax.broadcasted_iota(jnp.int32, (half, HY_W), 0)
    alt = jnp.where(row % 2 == 0, 1.0, -1.0)
    calt = jnp.where(lax.broadcasted_iota(jnp.int32, (ck, HY_W), 0) % 2 == 0, 1.0, -1.0)

    def short(j):
        c = slice(j * HY_W, (j + 1) * HY_W)
        ue = u_ref[:, j * HY_W:(j + 1) * HY_W].astype(F32)
        uo = u_ref[:, w3 + j * HY_W:w3 + (j + 1) * HY_W].astype(F32)
        uo_prev = jnp.where(row == 0, 0.0, pltpu.roll(uo, 1, 0))
        ue_next = jnp.where(row == half - 1, 0.0, pltpu.roll(ue, half - 1, 0))
        w0, w1, w2, b = sw_ref[0:1, c], sw_ref[1:2, c], sw_ref[2:3, c], sb_ref[0:1, c]
        return b + uo_prev * w0 + ue * w1 + uo * w2, b + ue * w0 + uo * w1 + ue_next * w2

    ve, vo = short(0)
    zfe_ref[...], zfo_ref[...] = ve, vo
    zbe_ref[...], zbo_ref[...] = ve.astype(BF16), vo.astype(BF16)
    g1e_ref[...], g1o_ref[...] = short(1)
    g2e_ref[...], g2o_ref[...] = short(2)
    gates = ((g1e_ref, g1o_ref), (g2e_ref, g2o_ref))

    for o in range(HY_ORDER):
        c = slice(o * HY_W, (o + 1) * HY_W)
        xr_m = jnp.sum(zfe_ref[...] * alt, axis=0, keepdims=True)
        xs_m = jnp.sum(zfo_ref[...] * alt, axis=0, keepdims=True)
        hr_m, hs_m = hm_ref[0:1, c], hm_ref[1:2, c]
        yr_m, ys_m = xr_m * hr_m - xs_m * hs_m, xr_m * hs_m + xs_m * hr_m
        for r in chunks:
            ec, es = _dot(ce_ref[r, :], zbe_ref[...]), _dot(se_ref[r, :], zbe_ref[...])
            oc, os_ = _dot(co_ref[r, :], zbo_ref[...]), _dot(so_ref[r, :], zbo_ref[...])
            xra, xsa, xrb, xsb = ec + oc, es + os_, ec - oc, os_ - es
            hra, hsa, hrb, hsb = hra_ref[r, c], hsa_ref[r, c], hrb_ref[r, c], hsb_ref[r, c]
            yra, ysa = xra * hra - xsa * hsa, xra * hsa + xsa * hra
            yrb, ysb = xrb * hrb - xsb * hsb, xrb * hsb + xsb * hrb
            p_ref[r, :] = (yra + yrb).astype(BF16)
            q_ref[r, :] = (ysa - ysb).astype(BF16)
            p2_ref[r, :] = (yra - yrb).astype(BF16)
            q2_ref[r, :] = (ysa + ysb).astype(BF16)
        ge_ref, go_ref = gates[o]
        for r in chunks:
            ye = _dot(ce_ref[r, :], p_ref[...]) + _dot(se_ref[r, :], q_ref[...]) + calt * yr_m
            yo = _dot(cot_ref[r, :], p2_ref[...]) + _dot(sot_ref[r, :], q2_ref[...]) + calt * ys_m
            ye = (ye + zfe_ref[r, :] * d_ref[o:o + 1, :]) * ge_ref[r, :]
            yo = (yo + zfo_ref[r, :] * d_ref[o:o + 1, :]) * go_ref[r, :]
            if o + 1 < HY_ORDER:
                zfe_ref[r, :], zfo_ref[r, :] = ye, yo
                zbe_ref[r, :], zbo_ref[r, :] = ye.astype(BF16), yo.astype(BF16)
            else:
                rows = slice(2 * r.start, 2 * r.start + 2 * ck)
                for j in range(HY_W // LANES):
                    lanes = slice(j * LANES, (j + 1) * LANES)
                    ys_ref[j, pl.ds(2 * r.start, ck, stride=2), :] = ye[:, lanes]
                    ys_ref[j, pl.ds(2 * r.start + 1, ck, stride=2), :] = yo[:, lanes]
                o_ref[rows, :] = jnp.concatenate([ys_ref[j, rows, :] for j in range(HY_W // LANES)],
                                                 axis=1).astype(o_ref.dtype)


def _hyena(hy, sw, sb, spec, dbias, layer, dft, n):
    m, w3 = 2 * hy.shape[0], hy.shape[1] // 2
    half = n // 2
    const = lambda b: (0, 0)
    once = lambda a: pl.BlockSpec(a.shape, const, pipeline_mode=pl.Buffered(1))
    f32s = pltpu.VMEM((half, HY_W), F32)
    bf16s = pltpu.VMEM((half, HY_W), BF16)
    return pl.pallas_call(
        _hyena_kernel,
        grid=(m // n,),
        in_specs=[pl.BlockSpec((half, 2 * w3), lambda b: (b, 0)),
                  _layer_spec(sw, layer), _layer_spec(sb, layer)]
                 + [once(a) for a in spec[:4]] + [pl.BlockSpec(spec[4].shape, const), _layer_spec(dbias, layer)]
                 + [once(a) for a in dft],
        out_specs=pl.BlockSpec((n, HY_W), lambda b: (b, 0)),
        out_shape=jax.ShapeDtypeStruct((m, HY_W), BF16),
        scratch_shapes=[f32s, f32s, bf16s, bf16s, f32s, f32s, f32s, f32s, bf16s, bf16s, bf16s, bf16s,
                        pltpu.VMEM((HY_W // LANES, n, LANES), F32)],
        compiler_params=_cparams(("arbitrary",)), name="hyena",
    )(hy, sw, sb, *spec, dbias, *dft)


def _fnet_kernel(u_ref, bc_ref, bs_ref, ce_ref, se_ref, co_ref, so_ref, o_ref):
    half = u_ref.shape[0]
    scale = 1.0 / math.sqrt(2 * half * FN_GROUP_W)
    ue, uo = u_ref[:, 0:FN_W], u_ref[:, FN_W:2 * FN_W]
    uce, use = _dot(ue, bc_ref[...]).astype(BF16), _dot(ue, bs_ref[...]).astype(BF16)
    uco, uso = _dot(uo, bc_ref[...]).astype(BF16), _dot(uo, bs_ref[...]).astype(BF16)
    ck = min(half, HY_CHUNK)
    for r0 in range(0, half, ck):
        r = slice(r0, r0 + ck)
        e = _dot(ce_ref[r, :], uce) - _dot(se_ref[r, :], use)
        o = _dot(co_ref[r, :], uco) - _dot(so_ref[r, :], uso)
        o_ref[r0:r0 + ck, :] = ((e + o) * scale).astype(o_ref.dtype)
        o_ref[half + r0:half + r0 + ck, :] = ((e - o) * scale).astype(o_ref.dtype)


def _fnet(fn, bc, bs, dft, n):
    m = 2 * fn.shape[0]
    half = n // 2
    const = lambda b: (0, 0)
    once = lambda a: pl.BlockSpec(a.shape, const, pipeline_mode=pl.Buffered(1))
    return pl.pallas_call(
        _fnet_kernel,
        grid=(m // n,),
        in_specs=[pl.BlockSpec((half, 2 * FN_W), lambda b: (b, 0)), pl.BlockSpec(bc.shape, const),
                  pl.BlockSpec(bs.shape, const)] + [once(a) for a in dft],
        out_specs=pl.BlockSpec((n, FN_W), lambda b: (b, 0)),
        out_shape=jax.ShapeDtypeStruct((m, FN_W), BF16),
        compiler_params=_cparams(("arbitrary",)), name="fnet",
    )(fn, bc, bs, *dft)


def _merge_kernel(x_ref, mod_ref, g1_ref, g2_ref, att_ref, hy_ref, fn_ref, wg_ref, wb_ref, wo_ref,
                  wr_ref, x1_ref, h2_ref, lg_ref):
    tm, d = x_ref.shape
    branches = ((att_ref, 0, ATT_Q_W), (hy_ref, ATT_Q_W, ATT_Q_W + HY_W), (fn_ref, ATT_Q_W + HY_W, MIX_W))
    rows = [slice(r, r + MERGE_CHUNK) for r in range(0, tm, MERGE_CHUNK)]
    xs = [x_ref[r, :] for r in rows]
    ys = [[_dot(b_ref[r, :], wb_ref[lo:hi, :]) for b_ref, lo, hi in branches] for r in rows]
    hbs = [_rms_mod(x, g1_ref[...], mod_ref[0:1, :], mod_ref[1:2, :]).astype(BF16) for x in xs]
    mixes = []
    for hb, y3 in zip(hbs, ys):
        mix = None
        for i, y in enumerate(y3):
            gate = _dot(hb, wg_ref[:, OFF_GATE + i * d:OFF_GATE + (i + 1) * d])
            term = y * (1.0 / (1.0 + jnp.exp(-gate)))
            mix = term if mix is None else mix + term
        mixes.append(mix.astype(BF16))
    for r, x, mix in zip(rows, xs, mixes):
        y = _dot(mix, wo_ref[...])
        x1 = x + mod_ref[2:3, :] * y
        x1_ref[r, :] = x1
        h2 = _rms_mod(x1, g2_ref[...], mod_ref[3:4, :], mod_ref[4:5, :])
        hi, lo = _split_bf16(h2)
        h2_ref[r, :] = hi
        t = _dot(hi, wr_ref[...]) + _dot(lo, wr_ref[...])
        lg_ref[r, :] = t + pltpu.roll(t, LANES - N_EXPERTS, 1)


def _merge(x, mod, g1, g2, att, hyo, fno, wg, layer, wb, wo, wr, n, tm):
    m, d = x.shape
    per_b = mod.shape[0] > 1
    tpb = n // tm
    row = lambda i: (i, 0)
    const = lambda i: (0, 0)
    mod_map = (lambda i: (i // tpb, 0, 0)) if per_b else (lambda i: (0, 0, 0))
    return pl.pallas_call(
        _merge_kernel,
        grid=(m // tm,),
        in_specs=[pl.BlockSpec((tm, d), row), pl.BlockSpec((None, 8, d), mod_map),
                  _layer_spec(g1, layer), _layer_spec(g2, layer),
                  pl.BlockSpec((tm, ATT_Q_W), row), pl.BlockSpec((tm, HY_W), row), pl.BlockSpec((tm, FN_W), row),
                  _layer_spec(wg, layer, pipeline_mode=pl.Buffered(1)),
                  _layer_spec(wb, layer), _layer_spec(wo, layer),
                  _layer_spec(wr, layer)],
        out_specs=[pl.BlockSpec((tm, d), row), pl.BlockSpec((tm, d), row), pl.BlockSpec((tm, LANES), row)],
        out_shape=[jax.ShapeDtypeStruct((m, d), F32), jax.ShapeDtypeStruct((m, d), BF16),
                   jax.ShapeDtypeStruct((m, LANES), F32)],
        compiler_params=_cparams(("arbitrary",)), name="merge",
    )(x, mod, g1, g2, att, hyo, fno, wg, wb, wo, wr)


def _route_kernel(lg_ref, tri_ref, rank_ref, aff_ref, rankt_ref, *, cap):
    ns, _, n = rank_ref.shape
    rows = ns * N_EXPERTS
    affs = []
    for s in range(ns):
        lt = lg_ref[s * n:(s + 1) * n, :].T[0:N_EXPERTS, :]
        e = jnp.exp(lt - jnp.max(lt, axis=0, keepdims=True))
        affs.append(e / jnp.sum(e, axis=0, keepdims=True))
    aff = jnp.concatenate(affs, axis=0) if ns > 1 else affs[0]
    bits = pltpu.bitcast(aff, jnp.int32)

    def step(i, thr):
        cand = thr | jnp.left_shift(jnp.int32(1), 30 - i)
        cnt = jnp.sum(jnp.where(bits >= cand, 1.0, 0.0), axis=1, keepdims=True)
        return jnp.where(cnt >= cap, cand, thr)

    thr = lax.fori_loop(0, 31, step, jnp.zeros((rows, 1), jnp.int32))
    gt = bits > thr
    eq = bits == thr
    need = cap - jnp.sum(jnp.where(gt, 1.0, 0.0), axis=1, keepdims=True)

    def excl_cumsum(mask):
        parts = []
        off = jnp.zeros((rows, 1), F32)
        for c in range(n // LANES):
            blk = mask[:, c * LANES:(c + 1) * LANES]
            parts.append(_dot(blk.astype(BF16), tri_ref[...]) + off)
            off = off + jnp.sum(blk, axis=1, keepdims=True)
        return jnp.concatenate(parts, axis=1)

    tie_rank = excl_cumsum(jnp.where(eq, 1.0, 0.0))
    sel = jnp.where(gt, 1.0, jnp.where(eq, jnp.where(tie_rank < need, 1.0, 0.0), 0.0))
    rank = jnp.where(sel > 0.5, excl_cumsum(sel), -1.0)
    pad = jnp.full((LANES - N_EXPERTS, n), -1.0, F32)
    for s in range(ns):
        r = rank[s * N_EXPERTS:(s + 1) * N_EXPERTS, :]
        rank_ref[s] = r.astype(jnp.int32)
        aff_ref[s] = affs[s]
        rankt_ref[s * n:(s + 1) * n, :] = jnp.concatenate([r, pad], axis=0).T.astype(BF16)


def _route(logits, tri, n, cap):
    m = logits.shape[0]
    b = m // n
    assert cap <= 256
    ns = math.gcd(b, ROUTE_SAMPLES)
    return pl.pallas_call(
        functools.partial(_route_kernel, cap=cap),
        grid=(b // ns,),
        in_specs=[pl.BlockSpec((ns * n, LANES), lambda i: (i, 0)), pl.BlockSpec(tri.shape, lambda i: (0, 0))],
        out_specs=[pl.BlockSpec((ns, N_EXPERTS, n), lambda i: (i, 0, 0)),
                   pl.BlockSpec((ns, N_EXPERTS, n), lambda i: (i, 0, 0)),
                   pl.BlockSpec((ns * n, LANES), lambda i: (i, 0))],
        out_shape=[jax.ShapeDtypeStruct((b, N_EXPERTS, n), jnp.int32),
                   jax.ShapeDtypeStruct((b, N_EXPERTS, n), F32),
                   jax.ShapeDtypeStruct((m, LANES), BF16)],
        compiler_params=_cparams(("arbitrary",)), name="route",
    )(logits, tri)


def _gather_kernel(h_ref, rank_ref, aff_ref, xg_ref, w_ref, sel_ref, *, cap):
    n, d = h_ref.shape
    slot = lax.broadcasted_iota(jnp.int32, (cap, n), 0)
    for e in range(N_EXPERTS):
        hit = slot == rank_ref[e:e + 1, :]
        sel_ref[e * cap:(e + 1) * cap, :] = jnp.where(hit, 1.0, 0.0).astype(BF16)
        wcol = jnp.sum(jnp.where(hit, aff_ref[e:e + 1, :], 0.0), axis=1, keepdims=True)
        w_ref[e] = jnp.broadcast_to(wcol, (cap, LANES))
    tn = 256
    for c in range(d // tn):
        xg = _dot(sel_ref[...], h_ref[:, c * tn:(c + 1) * tn]).astype(BF16)
        for e in range(N_EXPERTS):
            xg_ref[e, :, c * tn:(c + 1) * tn] = xg[e * cap:(e + 1) * cap, :]


def _gather(h2, rank, aff, n, cap):
    m, d = h2.shape
    b = m // n
    return pl.pallas_call(
        functools.partial(_gather_kernel, cap=cap),
        grid=(b,),
        in_specs=[pl.BlockSpec((n, d), lambda i: (i, 0)),
                  pl.BlockSpec((None, N_EXPERTS, n), lambda i: (i, 0, 0)),
                  pl.BlockSpec((None, N_EXPERTS, n), lambda i: (i, 0, 0))],
        out_specs=[pl.BlockSpec((N_EXPERTS, None, cap, d), lambda i: (0, i, 0, 0)),
                   pl.BlockSpec((N_EXPERTS, None, cap, LANES), lambda i: (0, i, 0, 0))],
        out_shape=[jax.ShapeDtypeStruct((N_EXPERTS, b, cap, d), BF16),
                   jax.ShapeDtypeStruct((N_EXPERTS, b, cap, LANES), F32)],
        scratch_shapes=[pltpu.VMEM((N_EXPERTS * cap, n), BF16)],
        compiler_params=_cparams(("arbitrary",)), name="moe_gather",
    )(h2, rank, aff)


def _ffn_kernel(*refs, extra):
    if extra:
        x_ref, w_ref, xc_ref, wc_ref, wg_ref, wu_ref, wd_ref, o_ref, oc_ref, acc_ref, accc_ref = refs
    else:
        x_ref, w_ref, wg_ref, wu_ref, wd_ref, o_ref, acc_ref = refs
    f = pl.program_id(2)

    def run(x_ref, w_ref, o_ref, acc_ref):
        @pl.when((pl.program_id(0) == 0) & (pl.program_id(1) == 0) & (f == 0))
        def _():
            acc_ref[...] = jnp.zeros_like(acc_ref)

        x = x_ref[...]
        d = acc_ref.shape[1]
        tf = 512
        hms, wds = [], []
        for c in range(wg_ref.shape[1] // tf):
            cols = slice(c * tf, (c + 1) * tf)
            a = _dot(x, wg_ref[:, cols].astype(BF16))
            u = _dot(x, wu_ref[:, cols].astype(BF16))
            hms.append((a * (1.0 / (1.0 + jnp.exp(-a))) * u).astype(BF16))
            wds.append(wd_ref[cols, :].astype(BF16))
        tn = 256
        for c in range(d // tn):
            cols = slice(c * tn, (c + 1) * tn)
            t = sum(_dot(hm, wd[:, cols]) for hm, wd in zip(hms, wds))
            s = jnp.where(f > 0, acc_ref[:, cols], 0.0) + t
            acc_ref[:, cols] = s
            o_ref[:, cols] = (s * w_ref[...] if tn == LANES else
                              s * jnp.concatenate([w_ref[...]] * (tn // LANES), axis=1)).astype(o_ref.dtype)

    run(x_ref, w_ref, o_ref, acc_ref)
    if extra:
        @pl.when(pl.program_id(1) == 0)
        def _():
            run(xc_ref, wc_ref, oc_ref, accc_ref)


def _ffn(xg, w, xgc, wc, wg, wu, wd, layer, tm, tf):
    ne, rows, d = xg.shape
    ff = wg.shape[3]
    extra = xgc is not None
    tile = lambda e, i, f: (e, i, 0)
    whole = lambda e, i, f: (e, 0, 0)
    args = [xg, w]
    in_specs = [pl.BlockSpec((None, tm, d), tile), pl.BlockSpec((None, tm, LANES), tile)]
    out_specs = [pl.BlockSpec((None, tm, d), tile)]
    out_shape = [jax.ShapeDtypeStruct((ne, rows, d), BF16)]
    scratch = [pltpu.VMEM((tm, d), F32)]
    if extra:
        rc = xgc.shape[1]
        args += [xgc, wc]
        in_specs += [pl.BlockSpec((None, rc, d), whole), pl.BlockSpec((None, rc, LANES), whole)]
        out_specs.append(pl.BlockSpec((None, rc, d), whole))
        out_shape.append(jax.ShapeDtypeStruct((ne, rc, d), BF16))
        scratch.append(pltpu.VMEM((rc, d), F32))
    args += [wg, wu, wd]
    in_specs += [pl.BlockSpec((None, None, d, tf), lambda e, i, f: (layer, e, 0, f)),
                 pl.BlockSpec((None, None, d, tf), lambda e, i, f: (layer, e, 0, f)),
                 pl.BlockSpec((None, None, tf, d), lambda e, i, f: (layer, e, f, 0))]
    outs = pl.pallas_call(
        functools.partial(_ffn_kernel, extra=extra),
        grid=(ne, rows // tm, ff // tf),
        in_specs=in_specs, out_specs=out_specs, out_shape=out_shape, scratch_shapes=scratch,
        compiler_params=_cparams(("arbitrary", "arbitrary", "arbitrary")), name="moe_ffn",
    )(*args)
    return (outs[0], outs[1]) if extra else (outs[0], None)


def _scatter_kernel(rankt_ref, y_ref, expand_ref, slot_ref, x_ref, mod_ref, o_ref, selt_ref, *, cap):
    n = rankt_ref.shape[0]
    width = N_EXPERTS * cap

    @pl.when(pl.program_id(1) == 0)
    def _():
        if cap % LANES == 0:
            rank = rankt_ref[...].astype(F32)
            slot = lax.broadcasted_iota(jnp.int32, (n, cap), 1).astype(F32)
            for e in range(N_EXPERTS):
                r = jnp.broadcast_to(rank[:, e:e + 1], (n, cap))
                selt_ref[:, e * cap:(e + 1) * cap] = jnp.where(r == slot, 1.0, 0.0).astype(BF16)
        else:
            r = _dot(rankt_ref[...], expand_ref[...])
            selt_ref[...] = jnp.where(r == slot_ref[...], 1.0, 0.0).astype(BF16)

    y = y_ref[...].reshape(width, y_ref.shape[2])
    o_ref[...] = x_ref[...] + mod_ref[5:6, :] * _dot(selt_ref[...], y)


def _scatter(rankt, yw, expand, slotpat, x, mod, n, cap):
    ne, b, _, d = yw.shape
    tn = SCATTER_COL_TILE
    mod_map = (lambda i, j: (i, 0, j)) if mod.shape[0] > 1 else (lambda i, j: (0, 0, j))
    return pl.pallas_call(
        functools.partial(_scatter_kernel, cap=cap),
        grid=(b, d // tn),
        in_specs=[pl.BlockSpec((n, LANES), lambda i, j: (i, 0)),
                  pl.BlockSpec((ne, None, cap, tn), lambda i, j: (0, i, 0, j)),
                  pl.BlockSpec(expand.shape, lambda i, j: (0, 0)),
                  pl.BlockSpec(slotpat.shape, lambda i, j: (0, 0)),
                  pl.BlockSpec((n, tn), lambda i, j: (i, j)),
                  pl.BlockSpec((None, 8, tn), mod_map)],
        out_specs=pl.BlockSpec((n, tn), lambda i, j: (i, j)),
        out_shape=jax.ShapeDtypeStruct((b * n, d), F32),
        scratch_shapes=[pltpu.VMEM((n, ne * cap), BF16)],
        compiler_params=_cparams(("arbitrary", "arbitrary")), name="moe_scatter",
    )(rankt, yw, expand, slotpat, x, mod)


def _moe(routed, routed_extra, wg, wu, wd, layer):
    def flat(r):
        rows = r["b"] * r["cap"]
        return r["xg"].reshape(N_EXPERTS, rows, D_MODEL), r["w"].reshape(N_EXPERTS, rows, LANES)

    def scatter(r, yw):
        yw = yw.reshape(N_EXPERTS, r["b"], r["cap"], D_MODEL)
        return _scatter(r["rankt"], yw, r["tabs"]["expand"], r["tabs"]["slotpat"], r["x1"], r["mod"],
                        r["n"], r["cap"])

    xg, w = flat(routed)
    xgc, wc = flat(routed_extra) if routed_extra is not None else (None, None)
    yw, ywc = _ffn(xg, w, xgc, wc, wg, wu, wd, layer, min(xg.shape[1], FFN_ROW_TILE), FFN_HIDDEN_TILE)
    return scatter(routed, yw), (scatter(routed_extra, ywc) if routed_extra is not None else None)


def _final_norm_kernel(x_ref, g_ref, o_ref):
    x = x_ref[...]
    o_ref[...] = x * lax.rsqrt(jnp.mean(x * x, axis=-1, keepdims=True) + EPS) * g_ref[...]


def _final_norm(x, g, tm):
    m, d = x.shape
    row = lambda i: (i, 0)
    return pl.pallas_call(
        _final_norm_kernel,
        grid=(m // tm,),
        in_specs=[pl.BlockSpec((tm, d), row), pl.BlockSpec((1, d), lambda i: (0, 0))],
        out_specs=pl.BlockSpec((tm, d), row),
        out_shape=jax.ShapeDtypeStruct((m, d), F32),
        compiler_params=_cparams(("arbitrary",)), name="final_norm",
    )(x, g)


def _fnet_dft_tables(n):
    j = jnp.arange(n // 2, dtype=jnp.int32)
    even = ((j[:, None] * (2 * j[None, :])) % n).astype(F32) * (2.0 * math.pi / n)
    odd = ((j[:, None] * (2 * j[None, :] + 1)) % n).astype(F32) * (2.0 * math.pi / n)
    return tuple(a.astype(BF16) for a in (jnp.cos(even), jnp.sin(even), jnp.cos(odd), jnp.sin(odd)))


def _rope_tables(n):
    rows = n // GRID_W
    row = jnp.repeat(jnp.arange(rows, dtype=F32), GRID_W)
    col = jnp.tile(jnp.arange(GRID_W, dtype=F32), rows)
    inv = ROPE_THETA ** (-jnp.arange(0, ROPE_AXIS_DIM, 2, dtype=F32) / ROPE_AXIS_DIM)
    ang = jnp.concatenate([row[:, None] * inv, col[:, None] * inv], axis=-1)
    cos = jnp.repeat(jnp.cos(ang), 2, axis=1)
    sin = jnp.repeat(jnp.sin(ang), 2, axis=1) * jnp.tile(jnp.array([-1.0, 1.0], F32), HEAD_DIM // 2)
    return jnp.tile(cos, (1, LANES // HEAD_DIM)), jnp.tile(sin, (1, LANES // HEAD_DIM))


def _hyena_dft_tables(n):
    j = jnp.arange(n // 2, dtype=jnp.int32)
    even = ((j[:, None] * j[None, :]) % n).astype(F32) * (2.0 * math.pi / n)
    odd = ((j[:, None] * (2 * j[None, :] + 1)) % (2 * n)).astype(F32) * (math.pi / n)
    ce, se, co, so = jnp.cos(even), jnp.sin(even), jnp.cos(odd), jnp.sin(odd)
    return tuple(a.astype(BF16) for a in (ce, se, co, so, co.T, so.T))


def _hyena_feats(n):
    pos = jnp.concatenate([jnp.arange(0, n, 2), jnp.arange(1, n, 2)]).astype(F32)
    t = pos / (n - 1)
    bands = jnp.linspace(1e-4, HY_BANDS - 1, HY_BANDS, dtype=F32)
    ang = (2.0 * math.pi / n) * pos[:, None] * bands[None, :]
    feats = jnp.concatenate([t[:, None], jnp.cos(ang), -jnp.sin(ang)], axis=-1)
    return jnp.pad(feats, ((0, 0), (0, LANES - HY_EMB)))


def _side(x, xb, n, mod, l, P, tabs, kv_ext, rope, tm, last_ctx):
    b = x.shape[0] // n
    outs = _inproj(x, xb, mod, P["norm1_g"], P["w_in"], l, tabs["cos"] if rope else None, tabs["sin"] if rope else None,
                   P["qg"], P["kg"], P["gmat"], n, tm)
    if xb is not None:
        x, outs = outs[0], outs[1:]
    q, k, v, hy, fn = outs
    k3, v3 = k.reshape(b, n, ATT_KV_W), v.reshape(b, n, ATT_KV_W)
    if last_ctx:
        return None, None, k3, v3
    if kv_ext is not None:
        kc, vc = jnp.concatenate([k3, kv_ext[0]], axis=1), jnp.concatenate([v3, kv_ext[1]], axis=1)
    else:
        kc, vc = k3, v3
    s = kc.shape[1]
    fill = jnp.concatenate([jnp.ones((b, s, 1), BF16), jnp.zeros((b, s, LANES - HEAD_DIM - 1), BF16)], axis=-1)
    vaug = jnp.stack([jnp.concatenate([vc[..., g * HEAD_DIM:(g + 1) * HEAD_DIM], fill], axis=-1)
                      for g in range(N_KV_HEADS)], axis=1)
    att = _attention(q, jnp.swapaxes(kc, 1, 2), vaug, n, min(n, ATTN_Q_TILE))
    spec = _hyena_filter(tabs["feats"], P["hy_w1"], P["hy_b1"], P["hy_w2"], P["hy_b2"],
                         P["hy_w3"], P["hy_freq"], l, tabs["deltas"], tabs["hy_dft"][:4])
    hyo = _hyena(hy, P["hy_sw"], P["hy_sb"], spec, P["hy_bias"], l, tabs["hy_dft"], n)
    fno = _fnet(fn, tabs["bc"], tabs["bs"], tabs["fn_dft"], n)
    x1, h2, logits = _merge(x, mod, P["norm1_g"], P["norm2_g"], att, hyo, fno, P["w_in"], l,
                            P["w_branch"], P["w_out"], P["w_router"], n, tm)
    cap = CAPACITY_FACTOR * n // N_EXPERTS
    rank, aff, rankt = _route(logits, tabs["tri"], n, cap)
    xg, w = _gather(h2, rank, aff, n, cap)
    return x1, dict(xg=xg, w=w, rankt=rankt, tabs=tabs, n=n, cap=cap, b=b), k3, v3


def _tables(n, rope):
    a = jnp.arange(FN_W, dtype=jnp.int32)
    same = (a[:, None] // FN_GROUP_W) == (a[None, :] // FN_GROUP_W)
    ang = ((a[:, None] * a[None, :]) % FN_GROUP_W).astype(F32) * (2.0 * math.pi / FN_GROUP_W)
    deltas = jnp.abs(jnp.linspace(math.log(HY_DECAY_TARGET) / HY_DECAY_LONG_PCT,
                                  math.log(HY_DECAY_TARGET) / HY_DECAY_SHORT_PCT, HY_W, dtype=F32))
    i = jnp.arange(LANES, dtype=jnp.int32)
    cap = CAPACITY_FACTOR * n // N_EXPERTS
    j = jnp.arange(N_EXPERTS * cap, dtype=jnp.int32)
    tabs = dict(expand=(i[:, None] == j[None, :] // cap).astype(BF16),
                slotpat=(j % cap).astype(F32)[None, :],
                hy_dft=_hyena_dft_tables(n), fn_dft=_fnet_dft_tables(n),
                bc=jnp.where(same, jnp.cos(ang), 0.0).astype(BF16),
                bs=jnp.where(same, jnp.sin(ang), 0.0).astype(BF16),
                feats=_hyena_feats(n), deltas=jnp.tile(deltas, HY_ORDER)[None, :],
                tri=(i[:, None] < i[None, :]).astype(BF16))
    if rope:
        tabs["cos"], tabs["sin"] = _rope_tables(n)
    return tabs


def kernel(x, c, ctx, c_ctx, w_mod, b_mod, norm1_g, norm2_g, w_in, q_gain, k_gain, hy_short_w, hy_short_b,
           hy_f_w1, hy_f_b1, hy_f_w2, hy_f_b2, hy_f_w3, hy_f_freq, hy_bias, w_branch, w_out, w_router,
           w_gate, w_up, w_down, final_g):
    bsz, n_lat, d = x.shape
    n_ctx = ctx.shape[1]
    depth = w_mod.shape[0]
    assert d == D_MODEL and n_lat % LANES == 0 and n_ctx % LANES == 0

    hid = jnp.arange(ATT_Q_W, dtype=jnp.int32) // HEAD_DIM
    P = dict(
        norm1_g=norm1_g[:, None, :], norm2_g=norm2_g[:, None, :],
        w_in=w_in.astype(BF16),
        qg=jnp.tile(q_gain, (1, N_Q_HEADS))[:, None, :], kg=jnp.tile(k_gain, (1, N_KV_HEADS))[:, None, :],
        gmat=(hid[:, None] == hid[None, :]).astype(BF16),
        hy_sw=hy_short_w, hy_sb=hy_short_b[:, None, :],
        hy_w1=jnp.pad(hy_f_w1, ((0, 0), (0, LANES - HY_EMB), (0, 0))), hy_b1=hy_f_b1[:, None, :],
        hy_w2=hy_f_w2, hy_b2=hy_f_b2[:, None, :], hy_w3=hy_f_w3, hy_freq=hy_f_freq[:, None, :],
        hy_bias=hy_bias,
        w_branch=w_branch.astype(BF16), w_out=w_out.astype(BF16),
        w_gate=w_gate, w_up=w_up, w_down=w_down,
    )
    wr_hi = w_router.astype(BF16)
    wr_lo = (w_router - wr_hi.astype(F32)).astype(BF16)
    P["w_router"] = jnp.pad(jnp.concatenate([wr_hi, wr_lo], axis=-1), ((0, 0), (0, 0), (0, LANES - 2 * N_EXPERTS)))

    tab_x = _tables(n_lat, True)
    tab_c = _tables(n_ctx, False)

    rows = -(-(bsz + 1) // 8) * 8
    c_all = jnp.concatenate([c, c_ctx[None, :], jnp.zeros((rows - bsz - 1, d), F32)], axis=0)
    mod = _modulation(c_all, w_mod, b_mod).reshape(depth, rows, 6, d)
    mod = jnp.pad(mod, ((0, 0), (0, 0), (0, 2), (0, 0)))

    tm_x, tm_c = min(n_lat, ROW_TILE), min(n_ctx, ROW_TILE)
    xs = x.reshape(bsz * n_lat, d)
    cs = ctx.reshape(bsz * n_ctx, d)
    for l in range(depth):
        last = l == depth - 1
        mod_x, mod_c = mod[l, :bsz], mod[l, bsz:bsz + 1]
        c1, rc, kc, vc = _side(cs, None, n_ctx, mod_c, l, P, tab_c, None, False, tm_c, last)
        x1, rx, _, _ = _side(xs, None, n_lat, mod_x, l, P, tab_x, (kc, vc), True, tm_x, False)
        rx.update(x1=x1, mod=mod_x)
        if rc is not None:
            rc.update(x1=c1, mod=mod_c)
        xs, cnew = _moe(rx, rc, P["w_gate"], P["w_up"], P["w_down"], l)
        if not last:
            cs = cnew
    return _final_norm(xs, final_g[None, :], tm_x).reshape(bsz, n_lat, d)
```
